```python
import jax
import jax.numpy as jnp
from jax import lax
import numpy as np

D_MODEL = 1024
BATCH = 8
SEQ = 4096
DEPTH = 1

HEAD_DIM = 64
N_FOX_HEADS = 8
N_SB_HEADS = 8
FOX_WIDTH = N_FOX_HEADS * HEAD_DIM
SB_WIDTH = N_SB_HEADS * HEAD_DIM
N_GROUPS = 4
EXPERTS_PER_GROUP = 4
N_EXPERTS = N_GROUPS * EXPERTS_PER_GROUP
TOP_K_EXPERTS = 2
D_FF_EXPERT = D_MODEL // 2
Q_BLOCK = 128
RMS_EPS = 1e-6
N_MOD = 6
IN_SPLITS = (
    FOX_WIDTH, 2 * FOX_WIDTH, 3 * FOX_WIDTH,
    3 * FOX_WIDTH + N_FOX_HEADS,
    3 * FOX_WIDTH + N_FOX_HEADS + SB_WIDTH,
    3 * FOX_WIDTH + N_FOX_HEADS + 2 * SB_WIDTH,
    3 * FOX_WIDTH + N_FOX_HEADS + 3 * SB_WIDTH,
    3 * FOX_WIDTH + N_FOX_HEADS + 3 * SB_WIDTH + D_MODEL,
)
N_IN = 3 * FOX_WIDTH + N_FOX_HEADS + 3 * SB_WIDTH + 2 * D_MODEL

kernel_name = "hybrid_fox_stickbreak_hmoe"


def rmsnorm(x, g):
    xf = x.astype(jnp.float32)
    xf = xf * lax.rsqrt(jnp.mean(xf * xf, axis=-1, keepdims=True) + RMS_EPS)
    return xf.astype(x.dtype) * g


def to_heads(t, n_heads):
    b, s, _ = t.shape
    return t.reshape(b, s, n_heads, HEAD_DIM).transpose(0, 2, 1, 3)


def from_heads(t):
    b, h, s, d = t.shape
    return t.transpose(0, 2, 1, 3).reshape(b, s, h * d)


def sweep_query_blocks(block_fn, seq):
    n_blocks = seq // Q_BLOCK
    out = lax.map(block_fn, jnp.arange(n_blocks))
    nb, b, h, qb, d = out.shape
    return out.transpose(1, 2, 0, 3, 4).reshape(b, h, nb * qb, d)


def forgetting_attention(q, k, v, log_f):
    seq = q.shape[2]
    scale = HEAD_DIM ** -0.5
    F = jnp.cumsum(log_f, axis=-1)
    kpos = jnp.arange(seq)

    def block(i):
        start = i * Q_BLOCK
        qs = lax.dynamic_slice_in_dim(q, start, Q_BLOCK, axis=2)
        Fq = lax.dynamic_slice_in_dim(F, start, Q_BLOCK, axis=2)
        logits = jnp.einsum("bhqd,bhkd->bhqk", qs, k).astype(jnp.float32) * scale
        logits = logits + Fq[..., :, None] - F[..., None, :]
        qpos = start + jnp.arange(Q_BLOCK)
        mask = kpos[None, :] <= qpos[:, None]
        logits = jnp.where(mask, logits, -jnp.inf)
        p = jax.nn.softmax(logits, axis=-1)
        return jnp.einsum("bhqk,bhkd->bhqd", p.astype(v.dtype), v)

    return sweep_query_blocks(block, seq)


def stick_breaking_attention(q, k, v):
    seq = q.shape[2]
    scale = HEAD_DIM ** -0.5
    kpos = jnp.arange(seq)

    def block(i):
        start = i * Q_BLOCK
        qs = lax.dynamic_slice_in_dim(q, start, Q_BLOCK, axis=2)
        z = jnp.einsum("bhqd,bhkd->bhqk", qs, k).astype(jnp.float32) * scale
        qpos = start + jnp.arange(Q_BLOCK)
        mask = kpos[None, :] < qpos[:, None]
        log_beta = jax.nn.log_sigmoid(z)
        log_one_minus = jnp.where(mask, jax.nn.log_sigmoid(-z), 0.0)
        suffix = lax.cumsum(log_one_minus, axis=3, reverse=True) - log_one_minus
        A = jnp.where(mask, jnp.exp(log_beta + suffix), 0.0)
        return jnp.einsum("bhqk,bhkd->bhqd", A.astype(v.dtype), v)

    return sweep_query_blocks(block, seq)


def hierarchical_moe(h, rg_w, rg_b, re_w, re_b, w_gate, w_up, w_down):
    b, s, d = h.shape
    hf = h.reshape(b * s, d)
    g_prob = jax.nn.softmax((hf @ rg_w + rg_b).astype(jnp.float32), axis=-1)
    g_top_p, g_idx = lax.top_k(g_prob, 1)
    e_logits = (hf @ re_w + re_b).astype(jnp.float32).reshape(b * s, N_GROUPS, EXPERTS_PER_GROUP)
    e_sel = jnp.take_along_axis(e_logits, g_idx[:, :, None], axis=1)[:, 0, :]
    e_prob = jax.nn.softmax(e_sel, axis=-1)
    e_top_p, e_idx = lax.top_k(e_prob, TOP_K_EXPERTS)
    e_top_p = e_top_p / jnp.sum(e_top_p, axis=-1, keepdims=True)
    within = jnp.sum(jax.nn.one_hot(e_idx, EXPERTS_PER_GROUP, dtype=jnp.float32)
                     * e_top_p[..., None], axis=1)
    combine = (jax.nn.one_hot(g_idx[:, 0], N_GROUPS, dtype=jnp.float32)[:, :, None]
               * within[:, None, :] * g_top_p[:, :, None])
    combine = combine.reshape(b * s, N_EXPERTS).astype(h.dtype)
    out = jnp.zeros_like(hf)
    for e in range(N_EXPERTS):
        act = jax.nn.silu(hf @ w_gate[e]) * (hf @ w_up[e])
        out = out + combine[:, e:e + 1] * (act @ w_down[e])
    return out.reshape(b, s, d)


def setup_inputs(seed: int = 0) -> dict:
    key = jax.random.key(seed)
    ks = jax.random.split(key, 20)
    D, L = D_MODEL, DEPTH

    def nrm(k, shape, scale):
        return jax.random.normal(k, shape, jnp.float32) * scale

    return {
        "x": nrm(ks[0], (BATCH, SEQ, D), 1.0),
        "c": nrm(ks[1], (BATCH, D), 1.0),
        "ada_w": nrm(ks[2], (L, D, N_MOD * D), 0.5 * D ** -0.5),
        "ada_b": nrm(ks[3], (L, N_MOD * D), 0.02),
        "norm1_g": 1.0 + nrm(ks[4], (L, D), 0.02),
        "w_in": nrm(ks[5], (L, D, N_IN), D ** -0.5),
        "b_forget": 2.0 + nrm(ks[6], (L, N_FOX_HEADS), 0.1),
        "w_branch_fox": nrm(ks[7], (L, FOX_WIDTH, D), FOX_WIDTH ** -0.5),
        "w_branch_sb": nrm(ks[8], (L, SB_WIDTH, D), SB_WIDTH ** -0.5),
        "w_out": nrm(ks[9], (L, D, D), D ** -0.5),
        "norm2_g": 1.0 + nrm(ks[10], (L, D), 0.02),
        "router_group_w": nrm(ks[11], (L, D, N_GROUPS), D ** -0.5),
        "router_group_b": nrm(ks[12], (L, N_GROUPS), 0.01),
        "router_expert_w": nrm(ks[13], (L, D, N_EXPERTS), D ** -0.5),
        "router_expert_b": nrm(ks[14], (L, N_EXPERTS), 0.01),
        "expert_w_gate": nrm(ks[15], (L, N_EXPERTS, D, D_FF_EXPERT), D ** -0.5),
        "expert_w_up": nrm(ks[16], (L, N_EXPERTS, D, D_FF_EXPERT), D ** -0.5),
        "expert_w_down": nrm(ks[17], (L, N_EXPERTS, D_FF_EXPERT, D), D_FF_EXPERT ** -0.5),
        "final_g": 1.0 + nrm(ks[18], (D,), 0.02),
    }


def reference(x, c, ada_w, ada_b, norm1_g, w_in, b_forget, w_branch_fox, w_branch_sb, w_out,
              norm2_g, router_group_w, router_group_b, router_expert_w, router_expert_b,
              expert_w_gate, expert_w_up, expert_w_down, final_g):
    c_act = jax.nn.silu(c)
    for l in range(DEPTH):
        mod = c_act @ ada_w[l] + ada_b[l]
        shift1, scale1, gate1, shift2, scale2, gate2 = [m[:, None, :] for m in jnp.split(mod, N_MOD, axis=-1)]

        h = rmsnorm(x, norm1_g[l]) * (1.0 + scale1) + shift1
        proj = h @ w_in[l]
        qa, ka, va, fa, qb, kb, vb, ga, gb = jnp.split(proj, IN_SPLITS, axis=-1)
        log_f = jax.nn.log_sigmoid((fa + b_forget[l]).astype(jnp.float32)).transpose(0, 2, 1)
        o_fox = forgetting_attention(to_heads(qa, N_FOX_HEADS), to_heads(ka, N_FOX_HEADS),
                                     to_heads(va, N_FOX_HEADS), log_f)
        o_sb = stick_breaking_attention(to_heads(qb, N_SB_HEADS), to_heads(kb, N_SB_HEADS),
                                        to_heads(vb, N_SB_HEADS))
        merged = (jax.nn.sigmoid(ga) * (from_heads(o_fox) @ w_branch_fox[l])
                  + jax.nn.sigmoid(gb) * (from_heads(o_sb) @ w_branch_sb[l]))
        x = x + gate1 * (merged @ w_out[l])

        h2 = rmsnorm(x, norm2_g[l]) * (1.0 + scale2) + shift2
        y = hierarchical_moe(h2, router_group_w[l], router_group_b[l], router_expert_w[l],
                             router_expert_b[l], expert_w_gate[l], expert_w_up[l], expert_w_down[l])
        x = x + gate2 * y
    return rmsnorm(x, final_g)
```

```python
import functools

import jax
import jax.numpy as jnp
import numpy as np
from jax import lax
from jax.experimental import pallas as pl
from jax.experimental.pallas import tpu as pltpu

F32 = jnp.float32
BF16 = jnp.bfloat16

HEAD_DIM = 64
LANES = 128
RMS_EPS = 1e-6
N_MOD = 6
N_GROUPS = 4
EXPERTS_PER_GROUP = 4
NEG_BIG = -1e30
VMEM_LIMIT = 56 * 1024 * 1024


def _dot(a, b):
    return jnp.dot(a, b, preferred_element_type=F32)


def _dot_nt(a, b):
    return lax.dot_general(a, b, (((1,), (1,)), ((), ())), preferred_element_type=F32)


def _split2(x):
    hi = x.astype(BF16)
    lo = (x - hi.astype(F32)).astype(BF16)
    return hi, lo


def _split3(x):
    hi = x.astype(BF16)
    r = x - hi.astype(F32)
    mid = r.astype(BF16)
    lo = (r - mid.astype(F32)).astype(BF16)
    return hi, mid, lo


def _log_sigmoid(z):
    return jnp.minimum(z, 0.0) - jnp.log1p(jnp.exp(-jnp.abs(z)))


def _rms_scale(x):
    return x * lax.rsqrt(jnp.mean(x * x, axis=-1, keepdims=True) + RMS_EPS)


def _mod_kernel(c_ref, w_ref, b_ref, o_ref):
    c = c_ref[...]
    ca = c * jax.nn.sigmoid(c)
    c_hi, c_mid, c_lo = _split3(ca)
    w_hi, w_mid, w_lo = _split3(w_ref[...])
    acc = _dot(c_hi, w_hi) + _dot(c_hi, w_mid) + _dot(c_mid, w_hi)
    acc += _dot(c_mid, w_mid) + _dot(c_hi, w_lo) + _dot(c_lo, w_hi)
    o_ref[...] = acc + b_ref[...]


def _mod_call(c, ada_w, ada_b):
    b, d = c.shape
    n = ada_w.shape[1]
    bp = 16
    tn = n // N_MOD
    c_pad = jnp.pad(c, ((0, bp - b), (0, 0)))
    out = pl.pallas_call(
        _mod_kernel,
        out_shape=jax.ShapeDtypeStruct((bp, n), F32),
        grid=(n // tn,),
        in_specs=[
            pl.BlockSpec((bp, d), lambda j: (0, 0)),
            pl.BlockSpec((d, tn), lambda j: (0, j)),
            pl.BlockSpec((1, tn), lambda j: (0, j)),
        ],
        out_specs=pl.BlockSpec((bp, tn), lambda j: (0, j)),
        compiler_params=pltpu.CompilerParams(
            dimension_semantics=("arbitrary",), vmem_limit_bytes=VMEM_LIMIT),
        name="mod",
    )(c_pad, ada_w, ada_b.reshape(1, n))
    return out[:b]


def _in_kernel(n_heads, n_aug, chunk, x_ref, mod_ref, g_ref, w_ref, wg_ref, wf_ref, bf_ref,
               sel_ref, proj_ref, gates_ref, carry_ref):
    i = pl.program_id(1)
    tm = x_ref.shape[0]

    @pl.when(i == 0)
    def _():
        carry_ref[...] = jnp.zeros_like(carry_ref)

    x = x_ref[...]
    shift = mod_ref[0:1, :]
    scale = mod_ref[1:2, :]
    h = _rms_scale(x) * g_ref[...] * (1.0 + scale) + shift
    hb = h.astype(BF16)

    lane = lax.broadcasted_iota(jnp.int32, (tm, LANES), 1)
    lf = _log_sigmoid(_dot(hb, wf_ref[...]) + bf_ref[...])
    lf = jnp.where(lane < n_heads, lf, 0.0)
    row = lax.broadcasted_iota(jnp.int32, (tm, tm), 0)
    col = lax.broadcasted_iota(jnp.int32, (tm, tm), 1)
    tri = jnp.where(col <= row, 1.0, 0.0).astype(BF16)
    l_hi, l_mid, l_lo = _split3(lf)
    fcum = _dot(tri, l_hi) + _dot(tri, l_mid) + _dot(tri, l_lo) + carry_ref[0:1, :]
    carry_ref[...] = jnp.broadcast_to(fcum[tm - 1:tm, :], carry_ref.shape)
    fx = jnp.where(lane == n_heads, 1.0, fcum)
    f_hi, f_mid, f_lo = _split3(fx)
    hml = jnp.concatenate([f_hi, f_mid, f_lo], axis=1)

    n_proj = w_ref.shape[1]
    for c0 in range(0, n_proj, chunk):
        acc = _dot(hb, w_ref[:, c0:c0 + chunk])
        if c0 < n_aug:
            acc += _dot(hml, sel_ref[:, c0:c0 + chunk])
        proj_ref[:, c0:c0 + chunk] = acc.astype(BF16)
    n_gate = wg_ref.shape[1]
    for c0 in range(0, n_gate, chunk):
        gates_ref[:, c0:c0 + chunk] = _dot(hb, wg_ref[:, c0:c0 + chunk])


def _in_call(x, mod3, norm_g, w_all, w_gates, w_f, b_f, sel, n_heads, tm):
    b, s, d = x.shape
    n_proj = w_all.shape[1]
    n_gate = w_gates.shape[1]
    n_aug = sel.shape[1]
    const = lambda shape: pl.BlockSpec(shape, lambda bi, i: (0,) * len(shape),
                                       pipeline_mode=pl.Buffered(1))
    return pl.pallas_call(
        functools.partial(_in_kernel, n_heads, n_aug, 512),
        out_shape=(jax.ShapeDtypeStruct((b, s, n_proj), BF16),
                   jax.ShapeDtypeStruct((b, s, n_gate), F32)),
        grid=(b, s // tm),
        in_specs=[
            pl.BlockSpec((None, tm, d), lambda bi, i: (bi, i, 0)),
            pl.BlockSpec((None, N_MOD, d), lambda bi, i: (bi, 0, 0)),
            const((1, d)),
            const((d, n_proj)),
            const((d, n_gate)),
            const((d, LANES)),
            const((1, LANES)),
            const((3 * LANES, n_aug)),
        ],
        out_specs=(pl.BlockSpec((None, tm, n_proj), lambda bi, i: (bi, i, 0)),
                   pl.BlockSpec((None, tm, n_gate), lambda bi, i: (bi, i, 0))),
        scratch_shapes=[pltpu.VMEM((8, LANES), F32)],
        compiler_params=pltpu.CompilerParams(
            dimension_semantics=("arbitrary", "arbitrary"), vmem_limit_bytes=VMEM_LIMIT),
        name="in_proj",
    )(x, mod3, norm_g, w_all, w_gates, w_f, b_f, sel)


def _fox_kernel(tk, q_ref, k_ref, v_ref, o_ref):
    i = pl.program_id(2)
    tq = q_ref.shape[0]
    n_diag = tq // tk
    row = lax.broadcasted_iota(jnp.int32, (tq, tk), 0)
    col = lax.broadcasted_iota(jnp.int32, (tq, tk), 1)
    lane = lax.broadcasted_iota(jnp.int32, (tq, LANES), 1)

    outs = []
    for hh in range(2):
        q = q_ref[:, hh * LANES:(hh + 1) * LANES]

        def step(j, carry, masked):
            m, l, acc = carry
            k0 = pl.multiple_of(j * tk, tk)
            k = k_ref[pl.ds(k0, tk), hh * LANES:(hh + 1) * LANES]
            s = _dot_nt(q, k)
            if masked:
                s = jnp.where(col + (j - i * n_diag) * tk <= row, s, NEG_BIG)
            m_new = jnp.maximum(m, jnp.max(s, axis=1, keepdims=True))
            alpha = jnp.exp(m - m_new)
            p = jnp.exp(s - m_new)
            l = alpha * l + jnp.sum(p, axis=1, keepdims=True)
            acc = alpha * acc + _dot(p.astype(BF16), v_ref[pl.ds(k0, tk), :])
            return m_new, l, acc

        carry = (jnp.full((tq, 1), NEG_BIG, F32), jnp.zeros((tq, 1), F32),
                 jnp.zeros((tq, LANES), F32))
        carry = lax.fori_loop(0, i * n_diag, functools.partial(step, masked=False), carry)
        for dj in range(n_diag):
            carry = step(i * n_diag + dj, carry, True)
        _, l, acc = carry
        outs.append(acc / l)
    o_ref[...] = jnp.where(lane < HEAD_DIM, outs[0], outs[1]).astype(o_ref.dtype)


def _fox_call(proj, n_heads, q_off, k_off, v_off, tq, tk):
    b, s, _ = proj.shape
    n_pairs = n_heads // 2
    pw = 2 * LANES
    return pl.pallas_call(
        functools.partial(_fox_kernel, tk),
        out_shape=jax.ShapeDtypeStruct((b, s, n_heads * HEAD_DIM), BF16),
        grid=(b, n_pairs, s // tq),
        in_specs=[
            pl.BlockSpec((None, tq, pw), lambda bi, p, i: (bi, i, q_off // pw + p)),
            pl.BlockSpec((None, s, pw), lambda bi, p, i: (bi, 0, k_off // pw + p)),
            pl.BlockSpec((None, s, LANES), lambda bi, p, i: (bi, 0, v_off // LANES + p)),
        ],
        out_specs=pl.BlockSpec((None, tq, LANES), lambda bi, p, i: (bi, i, p)),
        compiler_params=pltpu.CompilerParams(
            dimension_semantics=("arbitrary", "arbitrary", "arbitrary"),
            vmem_limit_bytes=VMEM_LIMIT),
        name="fox_attn",
    )(proj, proj, proj)


def _sb_kernel(tk, q_ref, k_ref, v_ref, o_ref):
    i = pl.program_id(2)
    tq = q_ref.shape[0]
    n_diag = tq // tk
    row = lax.broadcasted_iota(jnp.int32, (tq, tk), 0)
    col = lax.broadcasted_iota(jnp.int32, (tq, tk), 1)
    lane = lax.broadcasted_iota(jnp.int32, (tq, LANES), 1)
    ur = lax.broadcasted_iota(jnp.int32, (tk, tk), 0)
    uc = lax.broadcasted_iota(jnp.int32, (tk, tk), 1)
    upper = jnp.where(ur > uc, 1.0, 0.0).astype(BF16)

    outs = []
    for hh in range(2):
        q = q_ref[:, hh * LANES:(hh + 1) * LANES]

        def step(j, carry, masked):
            c, acc = carry
            k0 = pl.multiple_of(j * tk, tk)
            k = k_ref[pl.ds(k0, tk), hh * LANES:(hh + 1) * LANES]
            z = _dot_nt(q, k)
            ls = _log_sigmoid(z)
            lom = ls - z
            if masked:
                mask = col + (j - i * n_diag) * tk < row
                lom = jnp.where(mask, lom, 0.0)
            l_hi, l_lo = _split2(lom)
            within = _dot(l_hi, upper) + _dot(l_lo, upper)
            a = jnp.exp(ls + within + c)
            if masked:
                a = jnp.where(mask, a, 0.0)
            acc = acc + _dot(a.astype(BF16), v_ref[pl.ds(k0, tk), :])
            c = c + within[:, 0:1] + lom[:, 0:1]
            return c, acc

        carry = (jnp.zeros((tq, 1), F32), jnp.zeros((tq, LANES), F32))
        for dj in reversed(range(n_diag)):
            carry = step(i * n_diag + dj, carry, True)
        n_full = i * n_diag
        carry = lax.fori_loop(
            0, n_full, lambda t, cr: step(n_full - 1 - t, cr, False), carry)
        outs.append(carry[1])
    o_ref[...] = jnp.where(lane < HEAD_DIM, outs[0], outs[1]).astype(o_ref.dtype)


def _sb_call(proj, n_heads, q_off, k_off, v_off, tq, tk):
    b, s, _ = proj.shape
    n_pairs = n_heads // 2
    pw = 2 * LANES
    return pl.pallas_call(
        functools.partial(_sb_kernel, tk),
        out_shape=jax.ShapeDtypeStruct((b, s, n_heads * HEAD_DIM), BF16),
        grid=(b, n_pairs, s // tq),
        in_specs=[
            pl.BlockSpec((None, tq, pw), lambda bi, p, i: (bi, i, q_off // pw + p)),
            pl.BlockSpec((None, s, pw), lambda bi, p, i: (bi, 0, k_off // pw + p)),
            pl.BlockSpec((None, s, LANES), lambda bi, p, i: (bi, 0, v_off // LANES + p)),
        ],
        out_specs=pl.BlockSpec((None, tq, LANES), lambda bi, p, i: (bi, i, p)),
        compiler_params=pltpu.CompilerParams(
            dimension_semantics=("arbitrary", "arbitrary", "arbitrary"),
            vmem_limit_bytes=VMEM_LIMIT),
        name="sb_attn",
    )(proj, proj, proj)


def _route(logits):
    tm = logits.shape[0]
    lane = lax.broadcasted_iota(jnp.int32, (tm, LANES), 1)
    big = jnp.int32(LANES)
    g_mask = lane < N_GROUPS
    lg = jnp.where(g_mask, logits, -jnp.inf)
    g_max = jnp.max(lg, axis=1, keepdims=True)
    g_exp = jnp.exp(lg - g_max)
    g_prob = g_exp / jnp.sum(g_exp, axis=1, keepdims=True)
    g_top = jnp.max(g_prob, axis=1, keepdims=True)
    g_idx = jnp.min(jnp.where(g_mask & (g_prob == g_top), lane, big), axis=1, keepdims=True)
    lo = N_GROUPS + EXPERTS_PER_GROUP * g_idx
    e_mask = (lane >= lo) & (lane < lo + EXPERTS_PER_GROUP)
    le = jnp.where(e_mask, logits, -jnp.inf)
    e_max = jnp.max(le, axis=1, keepdims=True)
    e_exp = jnp.exp(le - e_max)
    e_prob = e_exp / jnp.sum(e_exp, axis=1, keepdims=True)
    p1 = jnp.max(e_prob, axis=1, keepdims=True)
    i1 = jnp.min(jnp.where(e_mask & (e_prob == p1), lane, big), axis=1, keepdims=True)
    rest = e_mask & (lane != i1)
    p2 = jnp.max(jnp.where(rest, e_prob, -1.0), axis=1, keepdims=True)
    i2 = jnp.min(jnp.where(rest & (e_prob == p2), lane, big), axis=1, keepdims=True)
    tot = p1 + p2
    within = jnp.where(lane == i1, p1 / tot, 0.0) + jnp.where(lane == i2, p2 / tot, 0.0)
    return within * g_top


def _post_kernel(of_ref, os_ref, gates_ref, x_ref, mod_ref, wbf_ref, wbs_ref, wo_ref, g2_ref,
                 wr_hi_ref, wr_lo_ref, rb_ref, x1_ref, h2_ref, comb_ref):
    d = x_ref.shape[1]
    bf = _dot(of_ref[...], wbf_ref[...])
    bs = _dot(os_ref[...], wbs_ref[...])
    merged = jax.nn.sigmoid(gates_ref[:, :d]) * bf + jax.nn.sigmoid(gates_ref[:, d:]) * bs
    y = _dot(merged.astype(BF16), wo_ref[...])
    x1 = x_ref[...] + mod_ref[2:3, :] * y
    x1_ref[...] = x1
    h2 = _rms_scale(x1) * g2_ref[...] * (1.0 + mod_ref[4:5, :]) + mod_ref[3:4, :]
    h_hi, h_lo = _split2(h2)
    h2_ref[...] = h_hi
    w_hi = wr_hi_ref[...]
    logits = _dot(h_hi, w_hi) + _dot(h_hi, wr_lo_ref[...]) + _dot(h_lo, w_hi) + rb_ref[...]
    comb_ref[...] = _route(logits)


def _post_call(o_fox, o_sb, gates, x, mod3, w_bf, w_bs, w_o, norm_g, wr_hi, wr_lo, rb, tm):
    b, s, d = x.shape
    wdt = o_fox.shape[2]
    const = lambda shape: pl.BlockSpec(shape, lambda bi, i: (0,) * len(shape),
                                       pipeline_mode=pl.Buffered(1))
    tile = lambda w: pl.BlockSpec((None, tm, w), lambda bi, i: (bi, i, 0))
    return pl.pallas_call(
        _post_kernel,
        out_shape=(jax.ShapeDtypeStruct((b, s, d), F32),
                   jax.ShapeDtypeStruct((b, s, d), BF16),
                   jax.ShapeDtypeStruct((b, s, LANES), F32)),
        grid=(b, s // tm),
        in_specs=[
            tile(wdt), tile(wdt), tile(2 * d), tile(d),
            pl.BlockSpec((None, N_MOD, d), lambda bi, i: (bi, 0, 0)),
            const((wdt, d)), const((wdt, d)), const((d, d)), const((1, d)),
            const((d, LANES)), const((d, LANES)), const((1, LANES)),
        ],
        out_specs=(tile(d), tile(d), tile(LANES)),
        compiler_params=pltpu.CompilerParams(
            dimension_semantics=("arbitrary", "arbitrary"), vmem_limit_bytes=VMEM_LIMIT),
        name="post",
    )(o_fox, o_sb, gates, x, mod3, w_bf, w_bs, w_o, norm_g, wr_hi, wr_lo, rb)


def _moe_kernel(h_ref, comb_ref, x1_ref, mod_ref, wg_ref, wu_ref, wd_ref, fg_ref, o_ref, acc_ref):
    e = pl.program_id(1)
    tm = h_ref.shape[0]

    @pl.when(e == 0)
    def _():
        acc_ref[...] = jnp.zeros_like(acc_ref)

    h = h_ref[...]
    g = _dot(h, wg_ref[...])
    u = _dot(h, wu_ref[...])
    act = (g * jax.nn.sigmoid(g) * u).astype(BF16)
    dn = _dot(act, wd_ref[...])
    lane = lax.broadcasted_iota(jnp.int32, (tm, LANES), 1)
    w = jnp.sum(jnp.where(lane == N_GROUPS + e, comb_ref[...], 0.0), axis=1, keepdims=True)
    acc_ref[...] += w * dn

    @pl.when(e == pl.num_programs(1) - 1)
    def _():
        x2 = x1_ref[...] + mod_ref[5:6, :] * acc_ref[...]
        o_ref[...] = _rms_scale(x2) * fg_ref[...]


def _moe_call(h2, comb, x1, mod3, w_gate, w_up, w_down, final_g, seq, tm):
    n, d = h2.shape
    n_exp, _, f = w_gate.shape
    per_seq = seq // tm
    return pl.pallas_call(
        _moe_kernel,
        out_shape=jax.ShapeDtypeStruct((n, d), F32),
        grid=(n // tm, n_exp),
        in_specs=[
            pl.BlockSpec((tm, d), lambda i, e: (i, 0)),
            pl.BlockSpec((tm, LANES), lambda i, e: (i, 0)),
            pl.BlockSpec((tm, d), lambda i, e: (i, 0)),
            pl.BlockSpec((None, N_MOD, d), lambda i, e: (i // per_seq, 0, 0)),
            pl.BlockSpec((None, d, f), lambda i, e: (e, 0, 0)),
            pl.BlockSpec((None, d, f), lambda i, e: (e, 0, 0)),
            pl.BlockSpec((None, f, d), lambda i, e: (e, 0, 0)),
            pl.BlockSpec((1, d), lambda i, e: (0, 0)),
        ],
        out_specs=pl.BlockSpec((tm, d), lambda i, e: (i, 0)),
        scratch_shapes=[pltpu.VMEM((tm, d), F32)],
        compiler_params=pltpu.CompilerParams(
            dimension_semantics=("arbitrary", "arbitrary"), vmem_limit_bytes=VMEM_LIMIT),
        name="moe",
    )(h2, comb, x1, mod3, w_gate, w_up, w_down, final_g)


def _pad_lanes(a):
    return jnp.pad(a.astype(F32), ((0, 0), (0, LANES - a.shape[1])))


def _pad_heads(w, n_heads, scale=1.0):
    d = w.shape[0]
    w = (w * scale).reshape(d, n_heads, HEAD_DIM)
    w = jnp.concatenate([w, jnp.zeros_like(w)], axis=2)
    return w.reshape(d, n_heads * LANES)


def _bias_selectors(n_heads):
    rows = 3 * LANES
    sel_q = np.zeros((rows, n_heads * LANES), np.float32)
    sel_k = np.zeros((rows, n_heads * LANES), np.float32)
    ones_row = n_heads
    for h in range(n_heads):
        base = h * LANES + HEAD_DIM
        for j in range(3):
            sel_q[j * LANES + h, base + j] = 1.0
            sel_q[ones_row, base + 3 + j] = 1.0
            sel_k[ones_row, base + j] = 1.0
            sel_k[j * LANES + h, base + 3 + j] = -1.0
    return sel_q, sel_k


def kernel(x, c, ada_w, ada_b, norm1_g, w_in, b_forget, w_branch_fox, w_branch_sb, w_out,
           norm2_g, router_group_w, router_group_b, router_expert_w, router_expert_b,
           expert_w_gate, expert_w_up, expert_w_down, final_g):
    b, s, d = x.shape
    depth = ada_w.shape[0]
    hf = b_forget.shape[1]
    fw = hf * HEAD_DIM
    sw = w_branch_sb.shape[1]
    hs = sw // HEAD_DIM
    scale = HEAD_DIM ** -0.5
    n_exp = expert_w_gate.shape[1]

    assert depth == 1, "the final RMSNorm is fused into the single layer's MoE kernel"
    for l in range(depth):
        mod = _mod_call(c, ada_w[l], ada_b[l])
        mod3 = mod.reshape(b, N_MOD, d)

        wl = w_in[l]
        o = 0
        wq_a = wl[:, o:o + fw]; o += fw
        wk_a = wl[:, o:o + fw]; o += fw
        wv_a = wl[:, o:o + fw]; o += fw
        wf = wl[:, o:o + hf]; o += hf
        wq_b = wl[:, o:o + sw]; o += sw
        wk_b = wl[:, o:o + sw]; o += sw
        wv_b = wl[:, o:o + sw]; o += sw
        w_gates = wl[:, o:].astype(BF16)
        w_all = jnp.concatenate([
            _pad_heads(wq_a, hf, scale), _pad_heads(wk_a, hf),
            _pad_heads(wq_b, hs, scale), _pad_heads(wk_b, hs), wv_a, wv_b], axis=1).astype(BF16)
        sel_q, sel_k = _bias_selectors(hf)
        sel = jnp.concatenate([sel_q, sel_k], axis=1).astype(BF16)
        w_f = _pad_lanes(wf).astype(BF16)
        b_f = _pad_lanes(b_forget[l].reshape(1, hf))

        q_a_off = 0
        k_a_off = hf * LANES
        q_b_off = 2 * hf * LANES
        k_b_off = q_b_off + hs * LANES
        v_a_off = k_b_off + hs * LANES
        v_b_off = v_a_off + fw

        proj, gates = _in_call(x, mod3, norm1_g[l].reshape(1, d), w_all, w_gates, w_f, b_f, sel,
                               hf, tm=min(s, 512))
        tq = min(s, 512)
        o_fox = _fox_call(proj, hf, q_a_off, k_a_off, v_a_off, tq, min(tq, 512))
        o_sb = _sb_call(proj, hs, q_b_off, k_b_off, v_b_off, min(s, 256), min(s, 256))

        wr = _pad_lanes(jnp.concatenate([router_group_w[l], router_expert_w[l]], axis=1))
        wr_hi = wr.astype(BF16)
        wr_lo = (wr - wr_hi.astype(F32)).astype(BF16)
        rb = _pad_lanes(jnp.concatenate([router_group_b[l], router_expert_b[l]]).reshape(1, -1))

        x1, h2, comb = _post_call(
            o_fox, o_sb, gates, x, mod3, w_branch_fox[l].astype(BF16), w_branch_sb[l].astype(BF16),
            w_out[l].astype(BF16), norm2_g[l].reshape(1, d), wr_hi, wr_lo, rb, tm=min(s, 512))

        fin = final_g
        out = _moe_call(h2.reshape(b * s, d), comb.reshape(b * s, LANES), x1.reshape(b * s, d),
                        mod3, expert_w_gate[l].astype(BF16), expert_w_up[l].astype(BF16),
                        expert_w_down[l].astype(BF16), fin.reshape(1, d), s, tm=min(s, 512))
        x = out.reshape(b, s, d)
    return x
```

```python
import functools

import jax
import jax.numpy as jnp
import numpy as np
from jax import lax
from jax.experimental import pallas as pl
from jax.experimental.pallas import tpu as pltpu

F32 = jnp.float32
BF16 = jnp.bfloat16

HEAD_DIM = 64
LANES = 128
RMS_EPS = 1e-6
N_MOD = 6
N_GROUPS = 4
EXPERTS_PER_GROUP = 4
NEG_BIG = -1e30
LOG2E = 1.4426950408889634
VMEM_LIMIT = 56 * 1024 * 1024


def _dot(a, b):
    return jnp.dot(a, b, preferred_element_type=F32)


def _dot_nt(a, b):
    return lax.dot_general(a, b, (((1,), (1,)), ((), ())), preferred_element_type=F32)


def _split2(x):
    hi = x.astype(BF16)
    lo = (x - hi.astype(F32)).astype(BF16)
    return hi, lo


def _split3(x):
    hi = x.astype(BF16)
    r = x - hi.astype(F32)
    mid = r.astype(BF16)
    lo = (r - mid.astype(F32)).astype(BF16)
    return hi, mid, lo


def _log_sigmoid(z):
    return jnp.minimum(z, 0.0) - jnp.log1p(jnp.exp(-jnp.abs(z)))


def _rms_scale(x):
    return x * lax.rsqrt(jnp.mean(x * x, axis=-1, keepdims=True) + RMS_EPS)


def _mod_kernel(c_ref, w_ref, b_ref, o_ref):
    c = c_ref[...]
    ca = c * jax.nn.sigmoid(c)
    c_hi, c_mid, c_lo = _split3(ca)
    w_hi, w_mid, w_lo = _split3(w_ref[...])
    acc = _dot(c_hi, w_hi) + _dot(c_hi, w_mid) + _dot(c_mid, w_hi)
    acc += _dot(c_mid, w_mid) + _dot(c_hi, w_lo) + _dot(c_lo, w_hi)
    o_ref[...] = acc + b_ref[...]


def _mod_call(c, ada_w, ada_b):
    b, d = c.shape
    n = ada_w.shape[1]
    bp = 16
    tn = n // N_MOD
    c_pad = jnp.pad(c, ((0, bp - b), (0, 0)))
    out = pl.pallas_call(
        _mod_kernel,
        out_shape=jax.ShapeDtypeStruct((bp, n), F32),
        grid=(n // tn,),
        in_specs=[
            pl.BlockSpec((bp, d), lambda j: (0, 0)),
            pl.BlockSpec((d, tn), lambda j: (0, j)),
            pl.BlockSpec((1, tn), lambda j: (0, j)),
        ],
        out_specs=pl.BlockSpec((bp, tn), lambda j: (0, j)),
        compiler_params=pltpu.CompilerParams(
            dimension_semantics=("arbitrary",), vmem_limit_bytes=VMEM_LIMIT),
        name="mod",
    )(c_pad, ada_w, ada_b.reshape(1, n))
    return out[:b]


def _in_kernel(n_heads, n_aug, chunk, x_ref, mod_ref, g_ref, w_ref, wg_ref, wf_ref, bf_ref,
               sel_ref, proj_ref, gates_ref, carry_ref):
    i = pl.program_id(1)
    tm = x_ref.shape[0]

    @pl.when(i == 0)
    def _():
        carry_ref[...] = jnp.zeros_like(carry_ref)

    x = x_ref[...]
    shift = mod_ref[0:1, :]
    scale = mod_ref[1:2, :]
    h = _rms_scale(x) * g_ref[...] * (1.0 + scale) + shift
    hb = h.astype(BF16)

    lane = lax.broadcasted_iota(jnp.int32, (tm, LANES), 1)
    lf = _log_sigmoid(_dot(hb, wf_ref[...]) + bf_ref[...])
    lf = jnp.where(lane < n_heads, lf, 0.0)
    row = lax.broadcasted_iota(jnp.int32, (tm, tm), 0)
    col = lax.broadcasted_iota(jnp.int32, (tm, tm), 1)
    tri = jnp.where(col <= row, 1.0, 0.0).astype(BF16)
    l_hi, l_mid, l_lo = _split3(lf)
    fcum = _dot(tri, l_hi) + _dot(tri, l_mid) + _dot(tri, l_lo) + carry_ref[0:1, :]
    carry_ref[...] = jnp.broadcast_to(fcum[tm - 1:tm, :], carry_ref.shape)
    fx = jnp.where(lane == n_heads, 1.0, fcum)
    f_hi, f_mid, f_lo = _split3(fx)
    hml = jnp.concatenate([f_hi, f_mid, f_lo], axis=1)

    n_proj = w_ref.shape[1]
    for c0 in range(0, n_proj, chunk):
        acc = _dot(hb, w_ref[:, c0:c0 + chunk])
        if c0 < n_aug:
            acc += _dot(hml, sel_ref[:, c0:c0 + chunk])
        proj_ref[:, c0:c0 + chunk] = acc.astype(BF16)
    n_gate = wg_ref.shape[1]
    for c0 in range(0, n_gate, chunk):
        gates_ref[:, c0:c0 + chunk] = _dot(hb, wg_ref[:, c0:c0 + chunk])


def _in_call(x, mod3, norm_g, w_all, w_gates, w_f, b_f, sel, n_heads, tm):
    b, s, d = x.shape
    n_proj = w_all.shape[1]
    n_gate = w_gates.shape[1]
    n_aug = sel.shape[1]
    const = lambda shape: pl.BlockSpec(shape, lambda bi, i: (0,) * len(shape),
                                       pipeline_mode=pl.Buffered(1))
    return pl.pallas_call(
        functools.partial(_in_kernel, n_heads, n_aug, 512),
        out_shape=(jax.ShapeDtypeStruct((b, s, n_proj), BF16),
                   jax.ShapeDtypeStruct((b, s, n_gate), F32)),
        grid=(b, s // tm),
        in_specs=[
            pl.BlockSpec((None, tm, d), lambda bi, i: (bi, i, 0)),
            pl.BlockSpec((None, N_MOD, d), lambda bi, i: (bi, 0, 0)),
            const((1, d)),
            const((d, n_proj)),
            const((d, n_gate)),
            const((d, LANES)),
            const((1, LANES)),
            const((3 * LANES, n_aug)),
        ],
        out_specs=(pl.BlockSpec((None, tm, n_proj), lambda bi, i: (bi, i, 0)),
                   pl.BlockSpec((None, tm, n_gate), lambda bi, i: (bi, i, 0))),
        scratch_shapes=[pltpu.VMEM((8, LANES), F32)],
        compiler_params=pltpu.CompilerParams(
            dimension_semantics=("arbitrary", "arbitrary"), vmem_limit_bytes=VMEM_LIMIT),
        name="in_proj",
    )(x, mod3, norm_g, w_all, w_gates, w_f, b_f, sel)


def _fox_kernel(tk, q_ref, k_ref, v_ref, o_ref):
    i = pl.program_id(2)
    tq = q_ref.shape[0]
    n_diag = tq // tk
    row = lax.broadcasted_iota(jnp.int32, (tq, tk), 0)
    col = lax.broadcasted_iota(jnp.int32, (tq, tk), 1)
    lane = lax.broadcasted_iota(jnp.int32, (tq, LANES), 1)

    def head_step(hh, j, state, masked):
        m, l, acc = state
        q = q_ref[:, hh * LANES:(hh + 1) * LANES]
        k0 = pl.multiple_of(j * tk, tk)
        k = k_ref[pl.ds(k0, tk), hh * LANES:(hh + 1) * LANES]
        s = _dot_nt(q, k)
        if masked:
            s = jnp.where(col + (j - i * n_diag) * tk <= row, s, NEG_BIG)
        m_new = jnp.maximum(m, jnp.max(s, axis=1, keepdims=True))
        alpha = jnp.exp(m - m_new)
        p = jnp.exp(s - m_new)
        l = alpha * l + jnp.sum(p, axis=1, keepdims=True)
        acc = alpha * acc + _dot(p.astype(BF16), v_ref[pl.ds(k0, tk), :])
        return m_new, l, acc

    def pair_step(j, states, masked):
        return tuple(head_step(hh, j, st, masked) for hh, st in enumerate(states))

    init = (jnp.full((tq, 1), NEG_BIG, F32), jnp.zeros((tq, 1), F32),
            jnp.zeros((tq, LANES), F32))
    states = lax.fori_loop(0, i * n_diag, functools.partial(pair_step, masked=False),
                           (init, init))
    for dj in range(n_diag):
        states = pair_step(i * n_diag + dj, states, True)
    outs = [acc / l for _, l, acc in states]
    o_ref[...] = jnp.where(lane < HEAD_DIM, outs[0], outs[1]).astype(o_ref.dtype)


def _fox_call(proj, n_heads, q_off, k_off, v_off, tq, tk):
    b, s, _ = proj.shape
    n_pairs = n_heads // 2
    pw = 2 * LANES
    return pl.pallas_call(
        functools.partial(_fox_kernel, tk),
        out_shape=jax.ShapeDtypeStruct((b, s, n_heads * HEAD_DIM), BF16),
        grid=(b, n_pairs, s // tq),
        in_specs=[
            pl.BlockSpec((None, tq, pw), lambda bi, p, i: (bi, i, q_off // pw + p)),
            pl.BlockSpec((None, s, pw), lambda bi, p, i: (bi, 0, k_off // pw + p)),
            pl.BlockSpec((None, s, LANES), lambda bi, p, i: (bi, 0, v_off // LANES + p)),
        ],
        out_specs=pl.BlockSpec((None, tq, LANES), lambda bi, p, i: (bi, i, p)),
        compiler_params=pltpu.CompilerParams(
            dimension_semantics=("arbitrary", "arbitrary", "arbitrary"),
            vmem_limit_bytes=VMEM_LIMIT),
        name="fox_attn",
    )(proj, proj, proj)


def _sb_kernel(tk, tr, q_ref, k_ref, v_ref, o_ref):
    i = pl.program_id(2)
    tq = q_ref.shape[0]
    n_chunks = tq // tr
    n_diag = tr // tk
    row = lax.broadcasted_iota(jnp.int32, (tr, tk), 0)
    col = lax.broadcasted_iota(jnp.int32, (tr, tk), 1)
    lane = lax.broadcasted_iota(jnp.int32, (tq, LANES), 1)
    ur = lax.broadcasted_iota(jnp.int32, (2 * tk, tk), 0)
    uc = lax.broadcasted_iota(jnp.int32, (2 * tk, tk), 1)
    upper2 = jnp.where((ur & (tk - 1)) > uc, 1.0, 0.0).astype(BF16)
    chains = [(hh, r) for r in range(n_chunks) for hh in range(2)]

    def chain_step(hh, r, j, state, dj=None):
        masked = dj is not None
        c, acc = state
        q = q_ref[r * tr:(r + 1) * tr, hh * LANES:(hh + 1) * LANES]
        k0 = pl.multiple_of(j * tk, tk)
        k = k_ref[pl.ds(k0, tk), hh * LANES:(hh + 1) * LANES]
        z = _dot_nt(q, k)
        t = jnp.exp2(jnp.abs(z) * (-LOG2E))
        ls = jnp.minimum(z, 0.0) - jnp.log(1.0 + t)
        lom = ls - z
        if masked:
            strict = col + dj * tk < row
            lom = jnp.where(strict, lom, 0.0)
        hi = pltpu.bitcast(pltpu.bitcast(lom, jnp.uint32) & jnp.uint32(0xFFFF0000), F32)
        parts = jnp.concatenate([hi.astype(BF16), (lom - hi).astype(BF16)], axis=1)
        within = _dot(parts, upper2)
        a = jnp.exp2((ls + within + c) * LOG2E)
        if masked:
            a = jnp.where(strict, a, 0.0)
        acc = acc + _dot(a.astype(BF16), v_ref[pl.ds(k0, tk), :])
        c = c + within[:, 0:1] + lom[:, 0:1]
        return c, acc

    base = i * (tq // tk)
    states = []
    for hh, r in chains:
        st = (jnp.zeros((tr, 1), F32), jnp.zeros((tr, LANES), F32))
        for dj in reversed(range(n_diag)):
            st = chain_step(hh, r, base + r * n_diag + dj, st, dj)
        for j in reversed(range(r * n_diag)):
            st = chain_step(hh, r, base + j, st)
        states.append(st)

    def body(t, sts):
        j = base - 1 - t
        return tuple(chain_step(hh, r, j, st) for (hh, r), st in zip(chains, sts))

    states = lax.fori_loop(0, base, body, tuple(states))
    acc = {ch: st[1] for ch, st in zip(chains, states)}
    o0 = jnp.concatenate([acc[(0, r)] for r in range(n_chunks)], axis=0)
    o1 = jnp.concatenate([acc[(1, r)] for r in range(n_chunks)], axis=0)
    o_ref[...] = jnp.where(lane < HEAD_DIM, o0, o1).astype(o_ref.dtype)


def _sb_call(proj, n_heads, q_off, k_off, v_off, tq, tk, tr):
    b, s, _ = proj.shape
    n_pairs = n_heads // 2
    pw = 2 * LANES
    return pl.pallas_call(
        functools.partial(_sb_kernel, tk, tr),
        out_shape=jax.ShapeDtypeStruct((b, s, n_heads * HEAD_DIM), BF16),
        grid=(b, n_pairs, s // tq),
        in_specs=[
            pl.BlockSpec((None, tq, pw), lambda bi, p, i: (bi, i, q_off // pw + p)),
            pl.BlockSpec((None, s, pw), lambda bi, p, i: (bi, 0, k_off // pw + p)),
            pl.BlockSpec((None, s, LANES), lambda bi, p, i: (bi, 0, v_off // LANES + p)),
        ],
        out_specs=pl.BlockSpec((None, tq, LANES), lambda bi, p, i: (bi, i, p)),
        compiler_params=pltpu.CompilerParams(
            dimension_semantics=("arbitrary", "arbitrary", "arbitrary"),
            vmem_limit_bytes=VMEM_LIMIT),
        name="sb_attn",
    )(proj, proj, proj)


def _route(logits):
    tm = logits.shape[0]
    lane = lax.broadcasted_iota(jnp.int32, (tm, LANES), 1)
    big = jnp.int32(LANES)
    g_mask = lane < N_GROUPS
    lg = jnp.where(g_mask, logits, -jnp.inf)
    g_max = jnp.max(lg, axis=1, keepdims=True)
    g_exp = jnp.exp(lg - g_max)
    g_prob = g_exp / jnp.sum(g_exp, axis=1, keepdims=True)
    g_top = jnp.max(g_prob, axis=1, keepdims=True)
    g_idx = jnp.min(jnp.where(g_mask & (g_prob == g_top), lane, big), axis=1, keepdims=True)
    lo = N_GROUPS + EXPERTS_PER_GROUP * g_idx
    e_mask = (lane >= lo) & (lane < lo + EXPERTS_PER_GROUP)
    le = jnp.where(e_mask, logits, -jnp.inf)
    e_max = jnp.max(le, axis=1, keepdims=True)
    e_exp = jnp.exp(le - e_max)
    e_prob = e_exp / jnp.sum(e_exp, axis=1, keepdims=True)
    p1 = jnp.max(e_prob, axis=1, keepdims=True)
    i1 = jnp.min(jnp.where(e_mask & (e_prob == p1), lane, big), axis=1, keepdims=True)
    rest = e_mask & (lane != i1)
    p2 = jnp.max(jnp.where(rest, e_prob, -1.0), axis=1, keepdims=True)
    i2 = jnp.min(jnp.where(rest & (e_prob == p2), lane, big), axis=1, keepdims=True)
    tot = p1 + p2
    within = jnp.where(lane == i1, p1 / tot, 0.0) + jnp.where(lane == i2, p2 / tot, 0.0)
    return within * g_top


def _post_kernel(of_ref, os_ref, gates_ref, x_ref, mod_ref, wbf_ref, wbs_ref, wo_ref, g2_ref,
                 wr_hi_ref, wr_lo_ref, rb_ref, x1_ref, h2_ref, comb_ref):
    d = x_ref.shape[1]
    bf = _dot(of_ref[...], wbf_ref[...])
    bs = _dot(os_ref[...], wbs_ref[...])
    merged = jax.nn.sigmoid(gates_ref[:, :d]) * bf + jax.nn.sigmoid(gates_ref[:, d:]) * bs
    y = _dot(merged.astype(BF16), wo_ref[...])
    x1 = x_ref[...] + mod_ref[2:3, :] * y
    x1_ref[...] = x1
    h2 = _rms_scale(x1) * g2_ref[...] * (1.0 + mod_ref[4:5, :]) + mod_ref[3:4, :]
    h_hi, h_lo = _split2(h2)
    h2_ref[...] = h_hi
    w_hi = wr_hi_ref[...]
    logits = _dot(h_hi, w_hi) + _dot(h_hi, wr_lo_ref[...]) + _dot(h_lo, w_hi) + rb_ref[...]
    comb_ref[...] = _route(logits)


def _post_call(o_fox, o_sb, gates, x, mod3, w_bf, w_bs, w_o, norm_g, wr_hi, wr_lo, rb, tm):
    b, s, d = x.shape
    wdt = o_fox.shape[2]
    const = lambda shape: pl.BlockSpec(shape, lambda bi, i: (0,) * len(shape),
                                       pipeline_mode=pl.Buffered(1))
    tile = lambda w: pl.BlockSpec((None, tm, w), lambda bi, i: (bi, i, 0))
    return pl.pallas_call(
        _post_kernel,
        out_shape=(jax.ShapeDtypeStruct((b, s, d), F32),
                   jax.ShapeDtypeStruct((b, s, d), BF16),
                   jax.ShapeDtypeStruct((b, s, LANES), F32)),
        grid=(b, s // tm),
        in_specs=[
            tile(wdt), tile(wdt), tile(2 * d), tile(d),
            pl.BlockSpec((None, N_MOD, d), lambda bi, i: (bi, 0, 0)),
            const((wdt, d)), const((wdt, d)), const((d, d)), const((1, d)),
            const((d, LANES)), const((d, LANES)), const((1, LANES)),
        ],
        out_specs=(tile(d), tile(d), tile(LANES)),
        compiler_params=pltpu.CompilerParams(
            dimension_semantics=("arbitrary", "arbitrary"), vmem_limit_bytes=VMEM_LIMIT),
        name="post",
    )(o_fox, o_sb, gates, x, mod3, w_bf, w_bs, w_o, norm_g, wr_hi, wr_lo, rb)


def _moe_kernel(h_ref, comb_ref, x1_ref, mod_ref, wg_ref, wu_ref, wd_ref, fg_ref, o_ref, acc_ref):
    e = pl.program_id(1)
    tm = h_ref.shape[0]

    @pl.when(e == 0)
    def _():
        acc_ref[...] = jnp.zeros_like(acc_ref)

    h = h_ref[...]
    g = _dot(h, wg_ref[...])
    u = _dot(h, wu_ref[...])
    act = (g * jax.nn.sigmoid(g) * u).astype(BF16)
    dn = _dot(act, wd_ref[...])
    lane = lax.broadcasted_iota(jnp.int32, (tm, LANES), 1)
    w = jnp.sum(jnp.where(lane == N_GROUPS + e, comb_ref[...], 0.0), axis=1, keepdims=True)
    acc_ref[...] += w * dn

    @pl.when(e == pl.num_programs(1) - 1)
    def _():
        x2 = x1_ref[...] + mod_ref[5:6, :] * acc_ref[...]
        o_ref[...] = _rms_scale(x2) * fg_ref[...]


def _moe_call(h2, comb, x1, mod3, w_gate, w_up, w_down, final_g, seq, tm):
    n, d = h2.shape
    n_exp, _, f = w_gate.shape
    per_seq = seq // tm
    return pl.pallas_call(
        _moe_kernel,
        out_shape=jax.ShapeDtypeStruct((n, d), F32),
        grid=(n // tm, n_exp),
        in_specs=[
            pl.BlockSpec((tm, d), lambda i, e: (i, 0)),
            pl.BlockSpec((tm, LANES), lambda i, e: (i, 0)),
            pl.BlockSpec((tm, d), lambda i, e: (i, 0)),
            pl.BlockSpec((None, N_MOD, d), lambda i, e: (i // per_seq, 0, 0)),
            pl.BlockSpec((None, d, f), lambda i, e: (e, 0, 0)),
            pl.BlockSpec((None, d, f), lambda i, e: (e, 0, 0)),
            pl.BlockSpec((None, f, d), lambda i, e: (e, 0, 0)),
            pl.BlockSpec((1, d), lambda i, e: (0, 0)),
        ],
        out_specs=pl.BlockSpec((tm, d), lambda i, e: (i, 0)),
        scratch_shapes=[pltpu.VMEM((tm, d), F32)],
        compiler_params=pltpu.CompilerParams(
            dimension_semantics=("arbitrary", "arbitrary"), vmem_limit_bytes=VMEM_LIMIT),
        name="moe",
    )(h2, comb, x1, mod3, w_gate, w_up, w_down, final_g)


def _pad_lanes(a):
    return jnp.pad(a.astype(F32), ((0, 0), (0, LANES - a.shape[1])))


def _pad_heads(w, n_heads, scale=1.0):
    d = w.shape[0]
    w = (w * scale).reshape(d, n_heads, HEAD_DIM)
    w = jnp.concatenate([w, jnp.zeros_like(w)], axis=2)
    return w.reshape(d, n_heads * LANES)


def _bias_selectors(n_heads):
    rows = 3 * LANES
    sel_q = np.zeros((rows, n_heads * LANES), np.float32)
    sel_k = np.zeros((rows, n_heads * LANES), np.float32)
    ones_row = n_heads
    for h in range(n_heads):
        base = h * LANES + HEAD_DIM
        for j in range(3):
            sel_q[j * LANES + h, base + j] = 1.0
            sel_q[ones_row, base + 3 + j] = 1.0
            sel_k[ones_row, base + j] = 1.0
            sel_k[j * LANES + h, base + 3 + j] = -1.0
    return sel_q, sel_k


def kernel(x, c, ada_w, ada_b, norm1_g, w_in, b_forget, w_branch_fox, w_branch_sb, w_out,
           norm2_g, router_group_w, router_group_b, router_expert_w, router_expert_b,
           expert_w_gate, expert_w_up, expert_w_down, final_g):
    b, s, d = x.shape
    depth = ada_w.shape[0]
    hf = b_forget.shape[1]
    fw = hf * HEAD_DIM
    sw = w_branch_sb.shape[1]
    hs = sw // HEAD_DIM
    scale = HEAD_DIM ** -0.5
    n_exp = expert_w_gate.shape[1]

    assert depth == 1, "the final RMSNorm is fused into the single layer's MoE kernel"
    for l in range(depth):
        mod = _mod_call(c, ada_w[l], ada_b[l])
        mod3 = mod.reshape(b, N_MOD, d)

        wl = w_in[l]
        o = 0
        wq_a = wl[:, o:o + fw]; o += fw
        wk_a = wl[:, o:o + fw]; o += fw
        wv_a = wl[:, o:o + fw]; o += fw
        wf = wl[:, o:o + hf]; o += hf
        wq_b = wl[:, o:o + sw]; o += sw
        wk_b = wl[:, o:o + sw]; o += sw
        wv_b = wl[:, o:o + sw]; o += sw
        w_gates = wl[:, o:].astype(BF16)
        w_all = jnp.concatenate([
            _pad_heads(wq_a, hf, scale), _pad_heads(wk_a, hf),
            _pad_heads(wq_b, hs, scale), _pad_heads(wk_b, hs), wv_a, wv_b], axis=1).astype(BF16)
        sel_q, sel_k = _bias_selectors(hf)
        sel = jnp.concatenate([sel_q, sel_k], axis=1).astype(BF16)
        w_f = _pad_lanes(wf).astype(BF16)
        b_f = _pad_lanes(b_forget[l].reshape(1, hf))

        q_a_off = 0
        k_a_off = hf * LANES
        q_b_off = 2 * hf * LANES
        k_b_off = q_b_off + hs * LANES
        v_a_off = k_b_off + hs * LANES
        v_b_off = v_a_off + fw

        proj, gates = _in_call(x, mod3, norm1_g[l].reshape(1, d), w_all, w_gates, w_f, b_f, sel,
                               hf, tm=min(s, 512))
        tq = min(s, 512)
        o_fox = _fox_call(proj, hf, q_a_off, k_a_off, v_a_off, tq, min(tq, 512))
        o_sb = _sb_call(proj, hs, q_b_off, k_b_off, v_b_off, min(s, 512), min(s, 256), min(s, 512))

        wr = _pad_lanes(jnp.concatenate([router_group_w[l], router_expert_w[l]], axis=1))
        wr_hi = wr.astype(BF16)
        wr_lo = (wr - wr_hi.astype(F32)).astype(BF16)
        rb = _pad_lanes(jnp.concatenate([router_group_b[l], router_expert_b[l]]).reshape(1, -1))

        x1, h2, comb = _post_call(
            o_fox, o_sb, gates, x, mod3, w_branch_fox[l].astype(BF16), w_branch_sb[l].astype(BF16),
            w_out[l].astype(BF16), norm2_g[l].reshape(1, d), wr_hi, wr_lo, rb, tm=min(s, 512))

        fin = final_g
        out = _moe_call(h2.reshape(b * s, d), comb.reshape(b * s, LANES), x1.reshape(b * s, d),
                        mod3, expert_w_gate[l].astype(BF16), expert_w_up[l].astype(BF16),
                        expert_w_down[l].astype(BF16), fin.reshape(1, d), s, tm=min(s, 512))
        x = out.reshape(b, s, d)
    return x
```

```python
import functools

import jax
import jax.numpy as jnp
import numpy as np
from jax import lax
from jax.experimental import pallas as pl
from jax.experimental.pallas import tpu as pltpu

F32 = jnp.float32
BF16 = jnp.bfloat16

HEAD_DIM = 64
LANES = 128
RMS_EPS = 1e-6
N_MOD = 6
N_GROUPS = 4
EXPERTS_PER_GROUP = 4
N_PAIRS = 6
MOE_TILE = 256
PERMUTE_ROWS = 2048
NEG_BIG = -1e30
LOG2E = 1.4426950408889634
VMEM_LIMIT = 56 * 1024 * 1024


def _dot(a, b):
    return jnp.dot(a, b, preferred_element_type=F32)


def _dot_nt(a, b):
    return lax.dot_general(a, b, (((1,), (1,)), ((), ())), preferred_element_type=F32)


def _split2(x):
    hi = x.astype(BF16)
    lo = (x - hi.astype(F32)).astype(BF16)
    return hi, lo


def _split3(x):
    hi = x.astype(BF16)
    r = x - hi.astype(F32)
    mid = r.astype(BF16)
    lo = (r - mid.astype(F32)).astype(BF16)
    return hi, mid, lo


def _log_sigmoid(z):
    return jnp.minimum(z, 0.0) - jnp.log1p(jnp.exp(-jnp.abs(z)))


def _rms_scale(x):
    return x * lax.rsqrt(jnp.mean(x * x, axis=-1, keepdims=True) + RMS_EPS)


def _mod_kernel(c_ref, w_ref, b_ref, o_ref):
    c = c_ref[...]
    ca = c * jax.nn.sigmoid(c)
    c_hi, c_mid, c_lo = _split3(ca)
    w_hi, w_mid, w_lo = _split3(w_ref[...])
    acc = _dot(c_hi, w_hi) + _dot(c_hi, w_mid) + _dot(c_mid, w_hi)
    acc += _dot(c_mid, w_mid) + _dot(c_hi, w_lo) + _dot(c_lo, w_hi)
    o_ref[...] = acc + b_ref[...]


def _mod_call(c, ada_w, ada_b):
    b, d = c.shape
    n = ada_w.shape[1]
    bp = 16
    tn = n // N_MOD
    c_pad = jnp.pad(c, ((0, bp - b), (0, 0)))
    out = pl.pallas_call(
        _mod_kernel,
        out_shape=jax.ShapeDtypeStruct((bp, n), F32),
        grid=(n // tn,),
        in_specs=[
            pl.BlockSpec((bp, d), lambda j: (0, 0)),
            pl.BlockSpec((d, tn), lambda j: (0, j)),
            pl.BlockSpec((1, tn), lambda j: (0, j)),
        ],
        out_specs=pl.BlockSpec((bp, tn), lambda j: (0, j)),
        compiler_params=pltpu.CompilerParams(
            dimension_semantics=("arbitrary",), vmem_limit_bytes=VMEM_LIMIT),
        name="mod",
    )(c_pad, ada_w, ada_b.reshape(1, n))
    return out[:b]


def _in_kernel(n_heads, n_aug, chunk, x_ref, mod_ref, g_ref, w_ref, wg_ref, wf_ref, bf_ref,
               sel_ref, proj_ref, gates_ref, carry_ref):
    i = pl.program_id(1)
    tm = x_ref.shape[0]

    @pl.when(i == 0)
    def _():
        carry_ref[...] = jnp.zeros_like(carry_ref)

    x = x_ref[...]
    shift = mod_ref[0:1, :]
    scale = mod_ref[1:2, :]
    h = _rms_scale(x) * g_ref[...] * (1.0 + scale) + shift
    hb = h.astype(BF16)

    lane = lax.broadcasted_iota(jnp.int32, (tm, LANES), 1)
    lf = _log_sigmoid(_dot(hb, wf_ref[...]) + bf_ref[...])
    lf = jnp.where(lane < n_heads, lf, 0.0)
    row = lax.broadcasted_iota(jnp.int32, (tm, tm), 0)
    col = lax.broadcasted_iota(jnp.int32, (tm, tm), 1)
    tri = jnp.where(col <= row, 1.0, 0.0).astype(BF16)
    l_hi, l_mid, l_lo = _split3(lf)
    fcum = _dot(tri, l_hi) + _dot(tri, l_mid) + _dot(tri, l_lo) + carry_ref[0:1, :]
    carry_ref[...] = jnp.broadcast_to(fcum[tm - 1:tm, :], carry_ref.shape)
    fx = jnp.where(lane == n_heads, 1.0, fcum)
    f_hi, f_mid, f_lo = _split3(fx)
    hml = jnp.concatenate([f_hi, f_mid, f_lo], axis=1)

    n_proj = w_ref.shape[1]
    for c0 in range(0, n_proj, chunk):
        acc = _dot(hb, w_ref[:, c0:c0 + chunk])
        if c0 < n_aug:
            acc += _dot(hml, sel_ref[:, c0:c0 + chunk])
        proj_ref[:, c0:c0 + chunk] = acc.astype(BF16)
    n_gate = wg_ref.shape[1]
    for c0 in range(0, n_gate, chunk):
        gates_ref[:, c0:c0 + chunk] = _dot(hb, wg_ref[:, c0:c0 + chunk])


def _in_call(x, mod3, norm_g, w_all, w_gates, w_f, b_f, sel, n_heads, tm):
    b, s, d = x.shape
    n_proj = w_all.shape[1]
    n_gate = w_gates.shape[1]
    n_aug = sel.shape[1]
    const = lambda shape: pl.BlockSpec(shape, lambda bi, i: (0,) * len(shape),
                                       pipeline_mode=pl.Buffered(1))
    return pl.pallas_call(
        functools.partial(_in_kernel, n_heads, n_aug, 512),
        out_shape=(jax.ShapeDtypeStruct((b, s, n_proj), BF16),
                   jax.ShapeDtypeStruct((b, s, n_gate), F32)),
        grid=(b, s // tm),
        in_specs=[
            pl.BlockSpec((None, tm, d), lambda bi, i: (bi, i, 0)),
            pl.BlockSpec((None, N_MOD, d), lambda bi, i: (bi, 0, 0)),
            const((1, d)),
            const((d, n_proj)),
            const((d, n_gate)),
            const((d, LANES)),
            const((1, LANES)),
            const((3 * LANES, n_aug)),
        ],
        out_specs=(pl.BlockSpec((None, tm, n_proj), lambda bi, i: (bi, i, 0)),
                   pl.BlockSpec((None, tm, n_gate), lambda bi, i: (bi, i, 0))),
        scratch_shapes=[pltpu.VMEM((8, LANES), F32)],
        compiler_params=pltpu.CompilerParams(
            dimension_semantics=("arbitrary", "arbitrary"), vmem_limit_bytes=VMEM_LIMIT),
        name="in_proj",
    )(x, mod3, norm_g, w_all, w_gates, w_f, b_f, sel)


def _fox_kernel(tk, q_ref, k_ref, v_ref, o_ref):
    i = pl.program_id(2)
    tq = q_ref.shape[0]
    n_diag = tq // tk
    row = lax.broadcasted_iota(jnp.int32, (tq, tk), 0)
    col = lax.broadcasted_iota(jnp.int32, (tq, tk), 1)
    lane = lax.broadcasted_iota(jnp.int32, (tq, LANES), 1)

    def head_step(hh, j, state, masked):
        m, l, acc = state
        q = q_ref[:, hh * LANES:(hh + 1) * LANES]
        k0 = pl.multiple_of(j * tk, tk)
        k = k_ref[pl.ds(k0, tk), hh * LANES:(hh + 1) * LANES]
        s = _dot_nt(q, k)
        if masked:
            s = jnp.where(col + (j - i * n_diag) * tk <= row, s, NEG_BIG)
        m_new = jnp.maximum(m, jnp.max(s, axis=1, keepdims=True))
        alpha = jnp.exp(m - m_new)
        p = jnp.exp(s - m_new)
        l = alpha * l + jnp.sum(p, axis=1, keepdims=True)
        acc = alpha * acc + _dot(p.astype(BF16), v_ref[pl.ds(k0, tk), :])
        return m_new, l, acc

    def pair_step(j, states, masked):
        return tuple(head_step(hh, j, st, masked) for hh, st in enumerate(states))

    init = (jnp.full((tq, 1), NEG_BIG, F32), jnp.zeros((tq, 1), F32),
            jnp.zeros((tq, LANES), F32))
    states = lax.fori_loop(0, i * n_diag, functools.partial(pair_step, masked=False),
                           (init, init))
    for dj in range(n_diag):
        states = pair_step(i * n_diag + dj, states, True)
    outs = [acc / l for _, l, acc in states]
    o_ref[...] = jnp.where(lane < HEAD_DIM, outs[0], outs[1]).astype(o_ref.dtype)


def _fox_call(proj, n_heads, q_off, k_off, v_off, tq, tk):
    b, s, _ = proj.shape
    n_pairs = n_heads // 2
    pw = 2 * LANES
    return pl.pallas_call(
        functools.partial(_fox_kernel, tk),
        out_shape=jax.ShapeDtypeStruct((b, s, n_heads * HEAD_DIM), BF16),
        grid=(b, n_pairs, s // tq),
        in_specs=[
            pl.BlockSpec((None, tq, pw), lambda bi, p, i: (bi, i, q_off // pw + p)),
            pl.BlockSpec((None, s, pw), lambda bi, p, i: (bi, 0, k_off // pw + p)),
            pl.BlockSpec((None, s, LANES), lambda bi, p, i: (bi, 0, v_off // LANES + p)),
        ],
        out_specs=pl.BlockSpec((None, tq, LANES), lambda bi, p, i: (bi, i, p)),
        compiler_params=pltpu.CompilerParams(
            dimension_semantics=("arbitrary", "arbitrary", "arbitrary"),
            vmem_limit_bytes=VMEM_LIMIT),
        name="fox_attn",
    )(proj, proj, proj)


def _sb_kernel(tk, tr, q_ref, k_ref, v_ref, o_ref):
    i = pl.program_id(2)
    tq = q_ref.shape[0]
    n_chunks = tq // tr
    n_diag = tr // tk
    row = lax.broadcasted_iota(jnp.int32, (tr, tk), 0)
    col = lax.broadcasted_iota(jnp.int32, (tr, tk), 1)
    lane = lax.broadcasted_iota(jnp.int32, (tq, LANES), 1)
    ur = lax.broadcasted_iota(jnp.int32, (2 * tk, tk), 0)
    uc = lax.broadcasted_iota(jnp.int32, (2 * tk, tk), 1)
    upper2 = jnp.where((ur & (tk - 1)) > uc, 1.0, 0.0).astype(BF16)
    chains = [(hh, r) for r in range(n_chunks) for hh in range(2)]

    def chain_step(hh, r, j, state, dj=None):
        masked = dj is not None
        c, acc = state
        q = q_ref[r * tr:(r + 1) * tr, hh * LANES:(hh + 1) * LANES]
        k0 = pl.multiple_of(j * tk, tk)
        k = k_ref[pl.ds(k0, tk), hh * LANES:(hh + 1) * LANES]
        z = _dot_nt(q, k)
        t = jnp.exp2(jnp.abs(z) * (-LOG2E))
        ls = jnp.minimum(z, 0.0) - jnp.log(1.0 + t)
        lom = ls - z
        if masked:
            strict = col + dj * tk < row
            lom = jnp.where(strict, lom, 0.0)
        hi = pltpu.bitcast(pltpu.bitcast(lom, jnp.uint32) & jnp.uint32(0xFFFF0000), F32)
        parts = jnp.concatenate([hi.astype(BF16), (lom - hi).astype(BF16)], axis=1)
        within = _dot(parts, upper2)
        a = jnp.exp2((ls + within + c) * LOG2E)
        if masked:
            a = jnp.where(strict, a, 0.0)
        acc = acc + _dot(a.astype(BF16), v_ref[pl.ds(k0, tk), :])
        c = c + within[:, 0:1] + lom[:, 0:1]
        return c, acc

    base = i * (tq // tk)
    states = []
    for hh, r in chains:
        st = (jnp.zeros((tr, 1), F32), jnp.zeros((tr, LANES), F32))
        for dj in reversed(range(n_diag)):
            st = chain_step(hh, r, base + r * n_diag + dj, st, dj)
        for j in reversed(range(r * n_diag)):
            st = chain_step(hh, r, base + j, st)
        states.append(st)

    def body(t, sts):
        j = base - 1 - t
        return tuple(chain_step(hh, r, j, st) for (hh, r), st in zip(chains, sts))

    states = lax.fori_loop(0, base, body, tuple(states))
    acc = {ch: st[1] for ch, st in zip(chains, states)}
    o0 = jnp.concatenate([acc[(0, r)] for r in range(n_chunks)], axis=0)
    o1 = jnp.concatenate([acc[(1, r)] for r in range(n_chunks)], axis=0)
    o_ref[...] = jnp.where(lane < HEAD_DIM, o0, o1).astype(o_ref.dtype)


def _sb_call(proj, n_heads, q_off, k_off, v_off, tq, tk, tr):
    b, s, _ = proj.shape
    n_pairs = n_heads // 2
    pw = 2 * LANES
    return pl.pallas_call(
        functools.partial(_sb_kernel, tk, tr),
        out_shape=jax.ShapeDtypeStruct((b, s, n_heads * HEAD_DIM), BF16),
        grid=(b, n_pairs, s // tq),
        in_specs=[
            pl.BlockSpec((None, tq, pw), lambda bi, p, i: (bi, i, q_off // pw + p)),
            pl.BlockSpec((None, s, pw), lambda bi, p, i: (bi, 0, k_off // pw + p)),
            pl.BlockSpec((None, s, LANES), lambda bi, p, i: (bi, 0, v_off // LANES + p)),
        ],
        out_specs=pl.BlockSpec((None, tq, LANES), lambda bi, p, i: (bi, i, p)),
        compiler_params=pltpu.CompilerParams(
            dimension_semantics=("arbitrary", "arbitrary", "arbitrary"),
            vmem_limit_bytes=VMEM_LIMIT),
        name="sb_attn",
    )(proj, proj, proj)


def _route(logits):
    tm = logits.shape[0]
    lane = lax.broadcasted_iota(jnp.int32, (tm, LANES), 1)
    big = jnp.int32(LANES)
    g_mask = lane < N_GROUPS
    lg = jnp.where(g_mask, logits, -jnp.inf)
    g_max = jnp.max(lg, axis=1, keepdims=True)
    g_exp = jnp.exp(lg - g_max)
    g_prob = g_exp / jnp.sum(g_exp, axis=1, keepdims=True)
    g_top = jnp.max(g_prob, axis=1, keepdims=True)
    g_idx = jnp.min(jnp.where(g_mask & (g_prob == g_top), lane, big), axis=1, keepdims=True)
    lo = N_GROUPS + EXPERTS_PER_GROUP * g_idx
    e_mask = (lane >= lo) & (lane < lo + EXPERTS_PER_GROUP)
    le = jnp.where(e_mask, logits, -jnp.inf)
    e_max = jnp.max(le, axis=1, keepdims=True)
    e_exp = jnp.exp(le - e_max)
    e_prob = e_exp / jnp.sum(e_exp, axis=1, keepdims=True)
    p1 = jnp.max(e_prob, axis=1, keepdims=True)
    i1 = jnp.min(jnp.where(e_mask & (e_prob == p1), lane, big), axis=1, keepdims=True)
    rest = e_mask & (lane != i1)
    p2 = jnp.max(jnp.where(rest, e_prob, -1.0), axis=1, keepdims=True)
    i2 = jnp.min(jnp.where(rest & (e_prob == p2), lane, big), axis=1, keepdims=True)
    tot = p1 + p2
    w1 = p1 / tot * g_top
    w2 = p2 / tot * g_top
    first_lower = i1 < i2
    e_lo = jnp.minimum(i1, i2) - lo
    e_hi = jnp.maximum(i1, i2) - lo
    pair = jnp.where(e_lo == 0, 0, jnp.where(e_lo == 1, 3, 5)) + (e_hi - e_lo - 1)
    cls = g_idx * N_PAIRS + pair
    return jnp.where(first_lower, w1, w2), jnp.where(first_lower, w2, w1), cls


def _post_kernel(of_ref, os_ref, gates_ref, x_ref, mod_ref, wbf_ref, wbs_ref, wo_ref, g2_ref,
                 wr_hi_ref, wr_lo_ref, rb_ref, x1_ref, hm_ref, meta_ref, cnt_ref, carry_ref):
    first = (pl.program_id(0) == 0) & (pl.program_id(1) == 0)

    @pl.when(first)
    def _():
        carry_ref[...] = jnp.zeros_like(carry_ref)

    tm = x_ref.shape[0]
    d = x_ref.shape[1]
    bf = _dot(of_ref[...], wbf_ref[...])
    bs = _dot(os_ref[...], wbs_ref[...])
    merged = jax.nn.sigmoid(gates_ref[:, :d]) * bf + jax.nn.sigmoid(gates_ref[:, d:]) * bs
    y = _dot(merged.astype(BF16), wo_ref[...])
    x1 = x_ref[...] + mod_ref[2:3, :] * y
    x1_ref[...] = x1
    h2 = _rms_scale(x1) * g2_ref[...] * (1.0 + mod_ref[4:5, :]) + mod_ref[3:4, :]
    h_hi, h_lo = _split2(h2)
    w_hi = wr_hi_ref[...]
    logits = _dot(h_hi, w_hi) + _dot(h_hi, wr_lo_ref[...]) + _dot(h_lo, w_hi) + rb_ref[...]
    w_first, w_second, cls = _route(logits)

    lane = lax.broadcasted_iota(jnp.int32, (tm, LANES), 1)
    onehot = lane == cls
    oh = jnp.where(onehot, 1.0, 0.0)
    row = lax.broadcasted_iota(jnp.int32, (tm, tm), 0)
    col = lax.broadcasted_iota(jnp.int32, (tm, tm), 1)
    earlier = jnp.where(col < row, 1.0, 0.0).astype(BF16)
    before = _dot(earlier, oh.astype(BF16)) + carry_ref[0:1, :]
    rank = jnp.sum(jnp.where(onehot, before, 0.0), axis=1, keepdims=True)
    counts = carry_ref[0:1, :] + jnp.sum(oh, axis=0, keepdims=True)
    carry_ref[...] = jnp.broadcast_to(counts, carry_ref.shape)
    cnt_ref[...] = jnp.broadcast_to(counts, cnt_ref.shape)

    meta = (jnp.where(lane == 0, w_first, 0.0) + jnp.where(lane == 1, w_second, 0.0)
            + jnp.where(lane == 2, cls.astype(F32), 0.0) + jnp.where(lane == 3, rank, 0.0))
    meta_ref[...] = meta
    hm_ref[:, :d] = h2
    hm_ref[:, d:] = meta


def _post_call(o_fox, o_sb, gates, x, mod3, w_bf, w_bs, w_o, norm_g, wr_hi, wr_lo, rb, tm):
    b, s, d = x.shape
    wdt = o_fox.shape[2]
    const = lambda shape: pl.BlockSpec(shape, lambda bi, i: (0,) * len(shape),
                                       pipeline_mode=pl.Buffered(1))
    tile = lambda w: pl.BlockSpec((None, tm, w), lambda bi, i: (bi, i, 0))
    return pl.pallas_call(
        _post_kernel,
        out_shape=(jax.ShapeDtypeStruct((b, s, d), F32),
                   jax.ShapeDtypeStruct((b, s, d + LANES), F32),
                   jax.ShapeDtypeStruct((b, s, LANES), F32),
                   jax.ShapeDtypeStruct((8, LANES), F32)),
        grid=(b, s // tm),
        in_specs=[
            tile(wdt), tile(wdt), tile(2 * d), tile(d),
            pl.BlockSpec((None, N_MOD, d), lambda bi, i: (bi, 0, 0)),
            const((wdt, d)), const((wdt, d)), const((d, d)), const((1, d)),
            const((d, LANES)), const((d, LANES)), const((1, LANES)),
        ],
        out_specs=(tile(d), tile(d + LANES), tile(LANES),
                   pl.BlockSpec((8, LANES), lambda bi, i: (0, 0))),
        scratch_shapes=[pltpu.VMEM((8, LANES), F32)],
        compiler_params=pltpu.CompilerParams(
            dimension_semantics=("arbitrary", "arbitrary"), vmem_limit_bytes=VMEM_LIMIT),
        name="post",
    )(o_fox, o_sb, gates, x, mod3, w_bf, w_bs, w_o, norm_g, wr_hi, wr_lo, rb)


def _permute_kernel(scatter, rows, pos_ref, src_ref, *rest):
    dst_ref, sem = rest[-2], rest[-1]
    t = pl.program_id(0)
    base = t * rows

    def row_copy(i, p):
        if scatter:
            return pltpu.make_async_copy(src_ref.at[pl.ds(i, 1), :], dst_ref.at[pl.ds(p, 1), :], sem)
        return pltpu.make_async_copy(src_ref.at[pl.ds(p, 1), :], dst_ref.at[pl.ds(i, 1), :], sem)

    def issue(r, carry):
        i = base + r
        row_copy(i, pos_ref[i]).start()
        return carry

    lax.fori_loop(0, rows, issue, 0, unroll=8)

    def wait_rows():
        pltpu.make_async_copy(src_ref.at[pl.ds(0, rows), :], dst_ref.at[pl.ds(0, rows), :], sem).wait()

    @pl.when(t > 0)
    def _():
        wait_rows()

    @pl.when(t == pl.num_programs(0) - 1)
    def _():
        wait_rows()


def _permute_call(src, pos, n_dst, scatter, rows, name):
    n_tok = pos.shape[0]
    w = src.shape[1]
    any_spec = pl.BlockSpec(memory_space=pl.ANY)
    operands = [pos, src]
    in_specs = [any_spec]
    aliases = {}
    if scatter:
        operands.append(jnp.zeros((n_dst, w), src.dtype))
        in_specs.append(any_spec)
        aliases = {2: 0}
    return pl.pallas_call(
        functools.partial(_permute_kernel, scatter, rows),
        out_shape=jax.ShapeDtypeStruct((n_dst, w), src.dtype),
        grid_spec=pltpu.PrefetchScalarGridSpec(
            num_scalar_prefetch=1, grid=(n_tok // rows,),
            in_specs=in_specs, out_specs=any_spec,
            scratch_shapes=[pltpu.SemaphoreType.DMA(())]),
        input_output_aliases=aliases,
        compiler_params=pltpu.CompilerParams(
            dimension_semantics=("arbitrary",), has_side_effects=True),
        name=name,
    )(*operands)


def _moe_kernel(elo_ref, ehi_ref, tix_ref, nv_ref, hm_ref, wg0_ref, wu0_ref, wd0_ref,
                wg1_ref, wu1_ref, wd1_ref, y_ref):
    d = y_ref.shape[1]
    valid = pl.program_id(0) < nv_ref[0]

    @pl.when(jnp.logical_not(valid))
    def _():
        y_ref[...] = jnp.zeros_like(y_ref)

    @pl.when(valid)
    def _():
        h = hm_ref[:, :d].astype(BF16)

        def expert(wg_ref, wu_ref, wd_ref):
            g = _dot(h, wg_ref[...])
            u = _dot(h, wu_ref[...])
            act = (g * jax.nn.sigmoid(g) * u).astype(BF16)
            return _dot(act, wd_ref[...])

        y_ref[...] = (hm_ref[:, d:d + 1] * expert(wg0_ref, wu0_ref, wd0_ref)
                      + hm_ref[:, d + 1:d + 2] * expert(wg1_ref, wu1_ref, wd1_ref))


def _moe_call(hm_sorted, tile_elo, tile_ehi, tile_ix, n_valid, w_gate, w_up, w_down, tm):
    n_pad, wdt = hm_sorted.shape
    _, d, f = w_gate.shape
    lo = lambda t, elo, ehi, tix, nv: (elo[t], 0, 0)
    hi = lambda t, elo, ehi, tix, nv: (ehi[t], 0, 0)
    row = lambda t, elo, ehi, tix, nv: (tix[t], 0)
    return pl.pallas_call(
        _moe_kernel,
        out_shape=jax.ShapeDtypeStruct((n_pad, d), F32),
        grid_spec=pltpu.PrefetchScalarGridSpec(
            num_scalar_prefetch=4, grid=(n_pad // tm,),
            in_specs=[
                pl.BlockSpec((tm, wdt), row),
                pl.BlockSpec((None, d, f), lo), pl.BlockSpec((None, d, f), lo),
                pl.BlockSpec((None, f, d), lo),
                pl.BlockSpec((None, d, f), hi), pl.BlockSpec((None, d, f), hi),
                pl.BlockSpec((None, f, d), hi),
            ],
            out_specs=pl.BlockSpec((tm, d), lambda t, elo, ehi, tix, nv: (t, 0))),
        compiler_params=pltpu.CompilerParams(
            dimension_semantics=("arbitrary",), vmem_limit_bytes=VMEM_LIMIT),
        name="moe",
    )(tile_elo, tile_ehi, tile_ix, n_valid, hm_sorted, w_gate, w_up, w_down, w_gate, w_up, w_down)


def _final_kernel(x1_ref, y_ref, mod_ref, fg_ref, o_ref):
    x2 = x1_ref[...] + mod_ref[5:6, :] * y_ref[...]
    o_ref[...] = _rms_scale(x2) * fg_ref[...]


def _final_call(x1, y, mod3, final_g, tm):
    b, s, d = x1.shape
    tile = pl.BlockSpec((None, tm, d), lambda bi, i: (bi, i, 0))
    return pl.pallas_call(
        _final_kernel,
        out_shape=jax.ShapeDtypeStruct((b, s, d), F32),
        grid=(b, s // tm),
        in_specs=[tile, tile, pl.BlockSpec((None, N_MOD, d), lambda bi, i: (bi, 0, 0)),
                  pl.BlockSpec((1, d), lambda bi, i: (0, 0))],
        out_specs=tile,
        compiler_params=pltpu.CompilerParams(
            dimension_semantics=("arbitrary", "arbitrary"), vmem_limit_bytes=VMEM_LIMIT),
        name="final",
    )(x1, y, mod3, final_g)


def _pad_lanes(a):
    return jnp.pad(a.astype(F32), ((0, 0), (0, LANES - a.shape[1])))


def _pad_heads(w, n_heads, scale=1.0):
    d = w.shape[0]
    w = (w * scale).reshape(d, n_heads, HEAD_DIM)
    w = jnp.concatenate([w, jnp.zeros_like(w)], axis=2)
    return w.reshape(d, n_heads * LANES)


def _bias_selectors(n_heads):
    rows = 3 * LANES
    sel_q = np.zeros((rows, n_heads * LANES), np.float32)
    sel_k = np.zeros((rows, n_heads * LANES), np.float32)
    ones_row = n_heads
    for h in range(n_heads):
        base = h * LANES + HEAD_DIM
        for j in range(3):
            sel_q[j * LANES + h, base + j] = 1.0
            sel_q[ones_row, base + 3 + j] = 1.0
            sel_k[ones_row, base + j] = 1.0
            sel_k[j * LANES + h, base + 3 + j] = -1.0
    return sel_q, sel_k


def kernel(x, c, ada_w, ada_b, norm1_g, w_in, b_forget, w_branch_fox, w_branch_sb, w_out,
           norm2_g, router_group_w, router_group_b, router_expert_w, router_expert_b,
           expert_w_gate, expert_w_up, expert_w_down, final_g):
    b, s, d = x.shape
    depth = ada_w.shape[0]
    hf = b_forget.shape[1]
    fw = hf * HEAD_DIM
    sw = w_branch_sb.shape[1]
    hs = sw // HEAD_DIM
    scale = HEAD_DIM ** -0.5
    n_exp = expert_w_gate.shape[1]

    assert depth == 1, "the final RMSNorm is fused into the single layer's MoE kernel"
    for l in range(depth):
        mod = _mod_call(c, ada_w[l], ada_b[l])
        mod3 = mod.reshape(b, N_MOD, d)

        wl = w_in[l]
        o = 0
        wq_a = wl[:, o:o + fw]; o += fw
        wk_a = wl[:, o:o + fw]; o += fw
        wv_a = wl[:, o:o + fw]; o += fw
        wf = wl[:, o:o + hf]; o += hf
        wq_b = wl[:, o:o + sw]; o += sw
        wk_b = wl[:, o:o + sw]; o += sw
        wv_b = wl[:, o:o + sw]; o += sw
        w_gates = wl[:, o:].astype(BF16)
        w_all = jnp.concatenate([
            _pad_heads(wq_a, hf, scale), _pad_heads(wk_a, hf),
            _pad_heads(wq_b, hs, scale), _pad_heads(wk_b, hs), wv_a, wv_b], axis=1).astype(BF16)
        sel_q, sel_k = _bias_selectors(hf)
        sel = jnp.concatenate([sel_q, sel_k], axis=1).astype(BF16)
        w_f = _pad_lanes(wf).astype(BF16)
        b_f = _pad_lanes(b_forget[l].reshape(1, hf))

        q_a_off = 0
        k_a_off = hf * LANES
        q_b_off = 2 * hf * LANES
        k_b_off = q_b_off + hs * LANES
        v_a_off = k_b_off + hs * LANES
        v_b_off = v_a_off + fw

        proj, gates = _in_call(x, mod3, norm1_g[l].reshape(1, d), w_all, w_gates, w_f, b_f, sel,
                               hf, tm=min(s, 512))
        tq = min(s, 512)
        o_fox = _fox_call(proj, hf, q_a_off, k_a_off, v_a_off, tq, min(tq, 512))
        o_sb = _sb_call(proj, hs, q_b_off, k_b_off, v_b_off, min(s, 512), min(s, 256), min(s, 512))

        wr = _pad_lanes(jnp.concatenate([router_group_w[l], router_expert_w[l]], axis=1))
        wr_hi = wr.astype(BF16)
        wr_lo = (wr - wr_hi.astype(F32)).astype(BF16)
        rb = _pad_lanes(jnp.concatenate([router_group_b[l], router_expert_b[l]]).reshape(1, -1))

        x1, hm, meta, counts = _post_call(
            o_fox, o_sb, gates, x, mod3, w_branch_fox[l].astype(BF16), w_branch_sb[l].astype(BF16),
            w_out[l].astype(BF16), norm2_g[l].reshape(1, d), wr_hi, wr_lo, rb, tm=min(s, 512))

        n = b * s
        tm_e = MOE_TILE
        n_cls = N_GROUPS * N_PAIRS
        n_pad = n + n_cls * tm_e
        cls = meta[..., 2].reshape(n).astype(jnp.int32)
        rank = meta[..., 3].reshape(n).astype(jnp.int32)
        cnt = counts[0, :n_cls].astype(jnp.int32)
        padded = (cnt + tm_e - 1) // tm_e * tm_e
        ends = jnp.cumsum(padded)
        starts = ends - padded
        cls_ids = jnp.arange(n_cls, dtype=jnp.int32)
        pos = jnp.sum(jnp.where(cls[:, None] == cls_ids[None, :], starts[None, :], 0), axis=1) + rank
        n_valid = ends[-1] // tm_e
        tile_ix = jnp.minimum(jnp.arange(n_pad // tm_e, dtype=jnp.int32), n_valid - 1)
        tile_cls = jnp.sum((tile_ix * tm_e)[:, None] >= ends[None, :], axis=1)
        pair_lo = jnp.asarray([0, 0, 0, 1, 1, 2], jnp.int32)
        pair_hi = jnp.asarray([1, 2, 3, 2, 3, 3], jnp.int32)
        tile_grp = tile_cls // N_PAIRS * EXPERTS_PER_GROUP
        tile_elo = tile_grp + pair_lo[tile_cls % N_PAIRS]
        tile_ehi = tile_grp + pair_hi[tile_cls % N_PAIRS]

        hm_sorted = _permute_call(hm.reshape(n, d + LANES), pos, n_pad, True, PERMUTE_ROWS,
                                  "dispatch")
        y_sorted = _moe_call(hm_sorted, tile_elo, tile_ehi, tile_ix, n_valid.reshape(1),
                             expert_w_gate[l].astype(BF16), expert_w_up[l].astype(BF16),
                             expert_w_down[l].astype(BF16), tm_e)
        y = _permute_call(y_sorted, pos, n, False, PERMUTE_ROWS, "combine")
        x = _final_call(x1, y.reshape(b, s, d), mod3, final_g.reshape(1, d), tm=min(s, 512))
    return x
```

```python
import functools

import jax
import jax.numpy as jnp
import numpy as np
from jax import lax
from jax.experimental import pallas as pl
from jax.experimental.pallas import tpu as pltpu

F32 = jnp.float32
BF16 = jnp.bfloat16

HEAD_DIM = 64
LANES = 128
RMS_EPS = 1e-6
N_MOD = 6
N_GROUPS = 4
EXPERTS_PER_GROUP = 4
N_PAIRS = 6
MOE_TILE = 256
PERMUTE_ROWS = 1024
SLAB_ROWS = 16
NEG_BIG = -1e30
LOG2E = 1.4426950408889634
VMEM_LIMIT = 56 * 1024 * 1024


def _dot(a, b):
    return jnp.dot(a, b, preferred_element_type=F32)


def _dot_nt(a, b):
    return lax.dot_general(a, b, (((1,), (1,)), ((), ())), preferred_element_type=F32)


def _split2(x):
    hi = x.astype(BF16)
    lo = (x - hi.astype(F32)).astype(BF16)
    return hi, lo


def _split3(x):
    hi = x.astype(BF16)
    r = x - hi.astype(F32)
    mid = r.astype(BF16)
    lo = (r - mid.astype(F32)).astype(BF16)
    return hi, mid, lo


def _log_sigmoid(z):
    return jnp.minimum(z, 0.0) - jnp.log1p(jnp.exp(-jnp.abs(z)))


def _rms_scale(x):
    return x * lax.rsqrt(jnp.mean(x * x, axis=-1, keepdims=True) + RMS_EPS)


def _mod_kernel(c_ref, w_ref, b_ref, o_ref):
    c = c_ref[...]
    ca = c * jax.nn.sigmoid(c)
    c_hi, c_mid, c_lo = _split3(ca)
    w_hi, w_mid, w_lo = _split3(w_ref[...])
    acc = _dot(c_hi, w_hi) + _dot(c_hi, w_mid) + _dot(c_mid, w_hi)
    acc += _dot(c_mid, w_mid) + _dot(c_hi, w_lo) + _dot(c_lo, w_hi)
    o_ref[...] = acc + b_ref[...]


def _mod_call(c, ada_w, ada_b):
    b, d = c.shape
    n = ada_w.shape[1]
    bp = 16
    tn = n // N_MOD
    c_pad = jnp.pad(c, ((0, bp - b), (0, 0)))
    out = pl.pallas_call(
        _mod_kernel,
        out_shape=jax.ShapeDtypeStruct((bp, n), F32),
        grid=(n // tn,),
        in_specs=[
            pl.BlockSpec((bp, d), lambda j: (0, 0)),
            pl.BlockSpec((d, tn), lambda j: (0, j)),
            pl.BlockSpec((1, tn), lambda j: (0, j)),
        ],
        out_specs=pl.BlockSpec((bp, tn), lambda j: (0, j)),
        compiler_params=pltpu.CompilerParams(
            dimension_semantics=("arbitrary",), vmem_limit_bytes=VMEM_LIMIT),
        name="mod",
    )(c_pad, ada_w, ada_b.reshape(1, n))
    return out[:b]


def _in_kernel(n_heads, n_aug, chunk, x_ref, mod_ref, g_ref, w_ref, wg_ref, wf_ref, bf_ref,
               sel_ref, proj_ref, gates_ref, carry_ref):
    i = pl.program_id(1)
    tm = x_ref.shape[0]

    @pl.when(i == 0)
    def _():
        carry_ref[...] = jnp.zeros_like(carry_ref)

    x = x_ref[...]
    shift = mod_ref[0:1, :]
    scale = mod_ref[1:2, :]
    h = _rms_scale(x) * g_ref[...] * (1.0 + scale) + shift
    hb = h.astype(BF16)

    lane = lax.broadcasted_iota(jnp.int32, (tm, LANES), 1)
    lf = _log_sigmoid(_dot(hb, wf_ref[...]) + bf_ref[...])
    lf = jnp.where(lane < n_heads, lf, 0.0)
    row = lax.broadcasted_iota(jnp.int32, (tm, tm), 0)
    col = lax.broadcasted_iota(jnp.int32, (tm, tm), 1)
    tri = jnp.where(col <= row, 1.0, 0.0).astype(BF16)
    l_hi, l_mid, l_lo = _split3(lf)
    fcum = _dot(tri, l_hi) + _dot(tri, l_mid) + _dot(tri, l_lo) + carry_ref[0:1, :]
    carry_ref[...] = jnp.broadcast_to(fcum[tm - 1:tm, :], carry_ref.shape)
    fx = jnp.where(lane == n_heads, 1.0, fcum)
    f_hi, f_mid, f_lo = _split3(fx)
    hml = jnp.concatenate([f_hi, f_mid, f_lo], axis=1)

    n_proj = w_ref.shape[1]
    for c0 in range(0, n_proj, chunk):
        acc = _dot(hb, w_ref[:, c0:c0 + chunk])
        if c0 < n_aug:
            acc += _dot(hml, sel_ref[:, c0:c0 + chunk])
        proj_ref[:, c0:c0 + chunk] = acc.astype(BF16)
    n_gate = wg_ref.shape[1]
    for c0 in range(0, n_gate, chunk):
        gates_ref[:, c0:c0 + chunk] = _dot(hb, wg_ref[:, c0:c0 + chunk])


def _in_call(x, mod3, norm_g, w_all, w_gates, w_f, b_f, sel, n_heads, tm):
    b, s, d = x.shape
    n_proj = w_all.shape[1]
    n_gate = w_gates.shape[1]
    n_aug = sel.shape[1]
    const = lambda shape: pl.BlockSpec(shape, lambda bi, i: (0,) * len(shape),
                                       pipeline_mode=pl.Buffered(1))
    return pl.pallas_call(
        functools.partial(_in_kernel, n_heads, n_aug, 512),
        out_shape=(jax.ShapeDtypeStruct((b, s, n_proj), BF16),
                   jax.ShapeDtypeStruct((b, s, n_gate), F32)),
        grid=(b, s // tm),
        in_specs=[
            pl.BlockSpec((None, tm, d), lambda bi, i: (bi, i, 0)),
            pl.BlockSpec((None, N_MOD, d), lambda bi, i: (bi, 0, 0)),
            const((1, d)),
            const((d, n_proj)),
            const((d, n_gate)),
            const((d, LANES)),
            const((1, LANES)),
            const((3 * LANES, n_aug)),
        ],
        out_specs=(pl.BlockSpec((None, tm, n_proj), lambda bi, i: (bi, i, 0)),
                   pl.BlockSpec((None, tm, n_gate), lambda bi, i: (bi, i, 0))),
        scratch_shapes=[pltpu.VMEM((8, LANES), F32)],
        compiler_params=pltpu.CompilerParams(
            dimension_semantics=("arbitrary", "arbitrary"), vmem_limit_bytes=VMEM_LIMIT),
        name="in_proj",
    )(x, mod3, norm_g, w_all, w_gates, w_f, b_f, sel)


def _fox_kernel(tk, q_ref, k_ref, v_ref, o_ref):
    i = pl.program_id(2)
    tq = q_ref.shape[0]
    n_diag = tq // tk
    row = lax.broadcasted_iota(jnp.int32, (tq, tk), 0)
    col = lax.broadcasted_iota(jnp.int32, (tq, tk), 1)
    lane = lax.broadcasted_iota(jnp.int32, (tq, LANES), 1)

    def head_step(hh, j, state, masked):
        m, l, acc = state
        q = q_ref[:, hh * LANES:(hh + 1) * LANES]
        k0 = pl.multiple_of(j * tk, tk)
        k = k_ref[pl.ds(k0, tk), hh * LANES:(hh + 1) * LANES]
        s = _dot_nt(q, k)
        if masked:
            s = jnp.where(col + (j - i * n_diag) * tk <= row, s, NEG_BIG)
        m_new = jnp.maximum(m, jnp.max(s, axis=1, keepdims=True))
        alpha = jnp.exp(m - m_new)
        p = jnp.exp(s - m_new)
        l = alpha * l + jnp.sum(p, axis=1, keepdims=True)
        acc = alpha * acc + _dot(p.astype(BF16), v_ref[pl.ds(k0, tk), :])
        return m_new, l, acc

    def pair_step(j, states, masked):
        return tuple(head_step(hh, j, st, masked) for hh, st in enumerate(states))

    init = (jnp.full((tq, 1), NEG_BIG, F32), jnp.zeros((tq, 1), F32),
            jnp.zeros((tq, LANES), F32))
    states = lax.fori_loop(0, i * n_diag, functools.partial(pair_step, masked=False),
                           (init, init))
    for dj in range(n_diag):
        states = pair_step(i * n_diag + dj, states, True)
    outs = [acc / l for _, l, acc in states]
    o_ref[...] = jnp.where(lane < HEAD_DIM, outs[0], outs[1]).astype(o_ref.dtype)


def _fox_call(proj, n_heads, q_off, k_off, v_off, tq, tk):
    b, s, _ = proj.shape
    n_pairs = n_heads // 2
    pw = 2 * LANES
    return pl.pallas_call(
        functools.partial(_fox_kernel, tk),
        out_shape=jax.ShapeDtypeStruct((b, s, n_heads * HEAD_DIM), BF16),
        grid=(b, n_pairs, s // tq),
        in_specs=[
            pl.BlockSpec((None, tq, pw), lambda bi, p, i: (bi, i, q_off // pw + p)),
            pl.BlockSpec((None, s, pw), lambda bi, p, i: (bi, 0, k_off // pw + p)),
            pl.BlockSpec((None, s, LANES), lambda bi, p, i: (bi, 0, v_off // LANES + p)),
        ],
        out_specs=pl.BlockSpec((None, tq, LANES), lambda bi, p, i: (bi, i, p)),
        compiler_params=pltpu.CompilerParams(
            dimension_semantics=("arbitrary", "arbitrary", "arbitrary"),
            vmem_limit_bytes=VMEM_LIMIT),
        name="fox_attn",
    )(proj, proj, proj)


def _sb_kernel(tk, tr, q_ref, k_ref, v_ref, o_ref):
    i = pl.program_id(2)
    tq = q_ref.shape[0]
    n_chunks = tq // tr
    n_diag = tr // tk
    row = lax.broadcasted_iota(jnp.int32, (tr, tk), 0)
    col = lax.broadcasted_iota(jnp.int32, (tr, tk), 1)
    lane = lax.broadcasted_iota(jnp.int32, (tq, LANES), 1)
    ur = lax.broadcasted_iota(jnp.int32, (2 * tk, tk), 0)
    uc = lax.broadcasted_iota(jnp.int32, (2 * tk, tk), 1)
    upper2 = jnp.where((ur & (tk - 1)) > uc, 1.0, 0.0).astype(BF16)
    chains = [(hh, r) for r in range(n_chunks) for hh in range(2)]

    def chain_step(hh, r, j, state, dj=None):
        masked = dj is not None
        c, acc = state
        q = q_ref[r * tr:(r + 1) * tr, hh * LANES:(hh + 1) * LANES]
        k0 = pl.multiple_of(j * tk, tk)
        k = k_ref[pl.ds(k0, tk), hh * LANES:(hh + 1) * LANES]
        z = _dot_nt(q, k)
        t = jnp.exp2(jnp.abs(z) * (-LOG2E))
        ls = jnp.minimum(z, 0.0) - jnp.log(1.0 + t)
        lom = ls - z
        if masked:
            strict = col + dj * tk < row
            lom = jnp.where(strict, lom, 0.0)
        hi = pltpu.bitcast(pltpu.bitcast(lom, jnp.uint32) & jnp.uint32(0xFFFF0000), F32)
        parts = jnp.concatenate([hi.astype(BF16), (lom - hi).astype(BF16)], axis=1)
        within = _dot(parts, upper2)
        a = jnp.exp2((ls + within + c) * LOG2E)
        if masked:
            a = jnp.where(strict, a, 0.0)
        acc = acc + _dot(a.astype(BF16), v_ref[pl.ds(k0, tk), :])
        c = c + within[:, 0:1] + lom[:, 0:1]
        return c, acc

    base = i * (tq // tk)
    states = []
    for hh, r in chains:
        st = (jnp.zeros((tr, 1), F32), jnp.zeros((tr, LANES), F32))
        for dj in reversed(range(n_diag)):
            st = chain_step(hh, r, base + r * n_diag + dj, st, dj)
        for j in reversed(range(r * n_diag)):
            st = chain_step(hh, r, base + j, st)
        states.append(st)

    def body(t, sts):
        j = base - 1 - t
        return tuple(chain_step(hh, r, j, st) for (hh, r), st in zip(chains, sts))

    states = lax.fori_loop(0, base, body, tuple(states))
    acc = {ch: st[1] for ch, st in zip(chains, states)}
    o0 = jnp.concatenate([acc[(0, r)] for r in range(n_chunks)], axis=0)
    o1 = jnp.concatenate([acc[(1, r)] for r in range(n_chunks)], axis=0)
    o_ref[...] = jnp.where(lane < HEAD_DIM, o0, o1).astype(o_ref.dtype)


def _sb_call(proj, n_heads, q_off, k_off, v_off, tq, tk, tr):
    b, s, _ = proj.shape
    n_pairs = n_heads // 2
    pw = 2 * LANES
    return pl.pallas_call(
        functools.partial(_sb_kernel, tk, tr),
        out_shape=jax.ShapeDtypeStruct((b, s, n_heads * HEAD_DIM), BF16),
        grid=(b, n_pairs, s // tq),
        in_specs=[
            pl.BlockSpec((None, tq, pw), lambda bi, p, i: (bi, i, q_off // pw + p)),
            pl.BlockSpec((None, s, pw), lambda bi, p, i: (bi, 0, k_off // pw + p)),
            pl.BlockSpec((None, s, LANES), lambda bi, p, i: (bi, 0, v_off // LANES + p)),
        ],
        out_specs=pl.BlockSpec((None, tq, LANES), lambda bi, p, i: (bi, i, p)),
        compiler_params=pltpu.CompilerParams(
            dimension_semantics=("arbitrary", "arbitrary", "arbitrary"),
            vmem_limit_bytes=VMEM_LIMIT),
        name="sb_attn",
    )(proj, proj, proj)


def _route(logits):
    tm = logits.shape[0]
    lane = lax.broadcasted_iota(jnp.int32, (tm, LANES), 1)
    big = jnp.int32(LANES)
    g_mask = lane < N_GROUPS
    lg = jnp.where(g_mask, logits, -jnp.inf)
    g_max = jnp.max(lg, axis=1, keepdims=True)
    g_exp = jnp.exp(lg - g_max)
    g_prob = g_exp / jnp.sum(g_exp, axis=1, keepdims=True)
    g_top = jnp.max(g_prob, axis=1, keepdims=True)
    g_idx = jnp.min(jnp.where(g_mask & (g_prob == g_top), lane, big), axis=1, keepdims=True)
    lo = N_GROUPS + EXPERTS_PER_GROUP * g_idx
    e_mask = (lane >= lo) & (lane < lo + EXPERTS_PER_GROUP)
    le = jnp.where(e_mask, logits, -jnp.inf)
    e_max = jnp.max(le, axis=1, keepdims=True)
    e_exp = jnp.exp(le - e_max)
    e_prob = e_exp / jnp.sum(e_exp, axis=1, keepdims=True)
    p1 = jnp.max(e_prob, axis=1, keepdims=True)
    i1 = jnp.min(jnp.where(e_mask & (e_prob == p1), lane, big), axis=1, keepdims=True)
    rest = e_mask & (lane != i1)
    p2 = jnp.max(jnp.where(rest, e_prob, -1.0), axis=1, keepdims=True)
    i2 = jnp.min(jnp.where(rest & (e_prob == p2), lane, big), axis=1, keepdims=True)
    tot = p1 + p2
    w1 = p1 / tot * g_top
    w2 = p2 / tot * g_top
    first_lower = i1 < i2
    e_lo = jnp.minimum(i1, i2) - lo
    e_hi = jnp.maximum(i1, i2) - lo
    pair = jnp.where(e_lo == 0, 0, jnp.where(e_lo == 1, 3, 5)) + (e_hi - e_lo - 1)
    cls = g_idx * N_PAIRS + pair
    return jnp.where(first_lower, w1, w2), jnp.where(first_lower, w2, w1), cls


def _post_kernel(of_ref, os_ref, gates_ref, x_ref, mod_ref, wbf_ref, wbs_ref, wo_ref, g2_ref,
                 wr_hi_ref, wr_lo_ref, rb_ref, x1_ref, slab_ref, meta_ref, cnt_ref, carry_ref):
    first = (pl.program_id(0) == 0) & (pl.program_id(1) == 0)

    @pl.when(first)
    def _():
        carry_ref[...] = jnp.zeros_like(carry_ref)

    tm = x_ref.shape[0]
    d = x_ref.shape[1]
    bf = _dot(of_ref[...], wbf_ref[...])
    bs = _dot(os_ref[...], wbs_ref[...])
    merged = jax.nn.sigmoid(gates_ref[:, :d]) * bf + jax.nn.sigmoid(gates_ref[:, d:]) * bs
    y = _dot(merged.astype(BF16), wo_ref[...])
    x1 = x_ref[...] + mod_ref[2:3, :] * y
    x1_ref[...] = x1
    h2 = _rms_scale(x1) * g2_ref[...] * (1.0 + mod_ref[4:5, :]) + mod_ref[3:4, :]
    h_hi, h_lo = _split2(h2)
    w_hi = wr_hi_ref[...]
    logits = _dot(h_hi, w_hi) + _dot(h_hi, wr_lo_ref[...]) + _dot(h_lo, w_hi) + rb_ref[...]
    w_first, w_second, cls = _route(logits)

    lane = lax.broadcasted_iota(jnp.int32, (tm, LANES), 1)
    onehot = lane == cls
    oh = jnp.where(onehot, 1.0, 0.0)
    row = lax.broadcasted_iota(jnp.int32, (tm, tm), 0)
    col = lax.broadcasted_iota(jnp.int32, (tm, tm), 1)
    earlier = jnp.where(col < row, 1.0, 0.0).astype(BF16)
    before = _dot(earlier, oh.astype(BF16)) + carry_ref[0:1, :]
    rank = jnp.sum(jnp.where(onehot, before, 0.0), axis=1, keepdims=True)
    counts = carry_ref[0:1, :] + jnp.sum(oh, axis=0, keepdims=True)
    carry_ref[...] = jnp.broadcast_to(counts, carry_ref.shape)
    cnt_ref[...] = jnp.broadcast_to(counts, cnt_ref.shape)

    meta = (jnp.where(lane == 0, w_first, 0.0) + jnp.where(lane == 1, w_second, 0.0)
            + jnp.where(lane == 2, cls.astype(F32), 0.0) + jnp.where(lane == 3, rank, 0.0))
    meta_ref[...] = meta
    h_rows = d // LANES
    for s in range(h_rows):
        slab_ref[:, s, :] = h2[:, s * LANES:(s + 1) * LANES]
    slab_ref[:, h_rows, :] = meta
    for s in range(h_rows + 1, SLAB_ROWS):
        slab_ref[:, s, :] = jnp.zeros((tm, LANES), F32)


def _post_call(o_fox, o_sb, gates, x, mod3, w_bf, w_bs, w_o, norm_g, wr_hi, wr_lo, rb, tm):
    b, s, d = x.shape
    wdt = o_fox.shape[2]
    const = lambda shape: pl.BlockSpec(shape, lambda bi, i: (0,) * len(shape),
                                       pipeline_mode=pl.Buffered(1))
    tile = lambda w: pl.BlockSpec((None, tm, w), lambda bi, i: (bi, i, 0))
    return pl.pallas_call(
        _post_kernel,
        out_shape=(jax.ShapeDtypeStruct((b, s, d), F32),
                   jax.ShapeDtypeStruct((b, s, SLAB_ROWS, LANES), F32),
                   jax.ShapeDtypeStruct((b, s, LANES), F32),
                   jax.ShapeDtypeStruct((8, LANES), F32)),
        grid=(b, s // tm),
        in_specs=[
            tile(wdt), tile(wdt), tile(2 * d), tile(d),
            pl.BlockSpec((None, N_MOD, d), lambda bi, i: (bi, 0, 0)),
            const((wdt, d)), const((wdt, d)), const((d, d)), const((1, d)),
            const((d, LANES)), const((d, LANES)), const((1, LANES)),
        ],
        out_specs=(tile(d),
                   pl.BlockSpec((None, tm, SLAB_ROWS, LANES), lambda bi, i: (bi, i, 0, 0)),
                   tile(LANES),
                   pl.BlockSpec((8, LANES), lambda bi, i: (0, 0))),
        scratch_shapes=[pltpu.VMEM((8, LANES), F32)],
        compiler_params=pltpu.CompilerParams(
            dimension_semantics=("arbitrary", "arbitrary"), vmem_limit_bytes=VMEM_LIMIT),
        name="post",
    )(o_fox, o_sb, gates, x, mod3, w_bf, w_bs, w_o, norm_g, wr_hi, wr_lo, rb)


def _dispatch_kernel(pos_ref, src_ref, init_ref, dst_ref, sem):
    del init_ref
    rows = src_ref.shape[0]
    base = pl.program_id(0) * rows

    def issue(r, carry):
        pltpu.make_async_copy(src_ref.at[r], dst_ref.at[pos_ref[base + r]], sem).start()
        return carry

    lax.fori_loop(0, rows, issue, 0, unroll=8)
    pltpu.make_async_copy(src_ref, dst_ref.at[pl.ds(0, rows)], sem).wait()


def _dispatch_call(slabs, pos, n_pad, rows):
    n = slabs.shape[0]
    any_spec = pl.BlockSpec(memory_space=pl.ANY)
    return pl.pallas_call(
        _dispatch_kernel,
        out_shape=jax.ShapeDtypeStruct((n_pad,) + slabs.shape[1:], slabs.dtype),
        grid_spec=pltpu.PrefetchScalarGridSpec(
            num_scalar_prefetch=1, grid=(n // rows,),
            in_specs=[pl.BlockSpec((rows,) + slabs.shape[1:], lambda t, pos: (t, 0, 0)), any_spec],
            out_specs=any_spec,
            scratch_shapes=[pltpu.SemaphoreType.DMA(())]),
        input_output_aliases={2: 0},
        compiler_params=pltpu.CompilerParams(
            dimension_semantics=("arbitrary",), has_side_effects=True,
            vmem_limit_bytes=VMEM_LIMIT),
        name="dispatch",
    )(pos, slabs, jnp.zeros((n_pad,) + slabs.shape[1:], slabs.dtype))


def _moe_kernel(elo_ref, ehi_ref, tix_ref, nv_ref, slab_ref, wg0_ref, wu0_ref, wd0_ref,
                wg1_ref, wu1_ref, wd1_ref, y_ref):
    y_rows = y_ref.shape[1]
    valid = pl.program_id(0) < nv_ref[0]

    @pl.when(jnp.logical_not(valid))
    def _():
        y_ref[...] = jnp.zeros_like(y_ref)

    @pl.when(valid)
    def _():
        h = jnp.concatenate([slab_ref[:, s, :] for s in range(y_rows)], axis=1).astype(BF16)
        meta = slab_ref[:, y_rows, :]

        def expert(wg_ref, wu_ref, wd_ref):
            g = _dot(h, wg_ref[...])
            u = _dot(h, wu_ref[...])
            act = (g * jax.nn.sigmoid(g) * u).astype(BF16)
            return _dot(act, wd_ref[...])

        y = (meta[:, 0:1] * expert(wg0_ref, wu0_ref, wd0_ref)
             + meta[:, 1:2] * expert(wg1_ref, wu1_ref, wd1_ref))
        for s in range(y_rows):
            y_ref[:, s, :] = y[:, s * LANES:(s + 1) * LANES]


def _moe_call(slabs, tile_elo, tile_ehi, tile_ix, n_valid, w_gate, w_up, w_down, tm):
    n_pad = slabs.shape[0]
    _, d, f = w_gate.shape
    lo = lambda t, elo, ehi, tix, nv: (elo[t], 0, 0)
    hi = lambda t, elo, ehi, tix, nv: (ehi[t], 0, 0)
    return pl.pallas_call(
        _moe_kernel,
        out_shape=jax.ShapeDtypeStruct((n_pad, d // LANES, LANES), F32),
        grid_spec=pltpu.PrefetchScalarGridSpec(
            num_scalar_prefetch=4, grid=(n_pad // tm,),
            in_specs=[
                pl.BlockSpec((tm,) + slabs.shape[1:], lambda t, elo, ehi, tix, nv: (tix[t], 0, 0)),
                pl.BlockSpec((None, d, f), lo), pl.BlockSpec((None, d, f), lo),
                pl.BlockSpec((None, f, d), lo),
                pl.BlockSpec((None, d, f), hi), pl.BlockSpec((None, d, f), hi),
                pl.BlockSpec((None, f, d), hi),
            ],
            out_specs=pl.BlockSpec((tm, d // LANES, LANES),
                                   lambda t, elo, ehi, tix, nv: (t, 0, 0))),
        compiler_params=pltpu.CompilerParams(
            dimension_semantics=("arbitrary",), vmem_limit_bytes=VMEM_LIMIT),
        name="moe",
    )(tile_elo, tile_ehi, tile_ix, n_valid, slabs, w_gate, w_up, w_down, w_gate, w_up, w_down)


def _combine_kernel(pos_ref, ys_ref, x1_ref, mod_ref, fg_ref, o_ref, ybuf, sem):
    rows = x1_ref.shape[0]
    y_rows = ybuf.shape[2]
    t = pl.program_id(0)

    def issue(step, slot):
        base = step * rows

        def body(r, carry):
            pltpu.make_async_copy(ys_ref.at[pos_ref[base + r]], ybuf.at[slot, r],
                                  sem.at[slot]).start()
            return carry

        lax.fori_loop(0, rows, body, 0, unroll=8)

    @pl.when(t == 0)
    def _():
        issue(0, 0)

    @pl.when(t + 1 < pl.num_programs(0))
    def _():
        issue(t + 1, (t + 1) % 2)

    slot = t % 2
    pltpu.make_async_copy(ys_ref.at[pl.ds(0, rows)], ybuf.at[slot], sem.at[slot]).wait()
    y = jnp.concatenate([ybuf[slot, :, s, :] for s in range(y_rows)], axis=1)
    x2 = x1_ref[...] + mod_ref[5:6, :] * y
    o_ref[...] = _rms_scale(x2) * fg_ref[...]


def _combine_call(y_sorted, pos, x1, mod3, final_g, seq, rows):
    n, d = x1.shape
    y_rows = y_sorted.shape[1]
    per_seq = seq // rows
    return pl.pallas_call(
        _combine_kernel,
        out_shape=jax.ShapeDtypeStruct((n, d), F32),
        grid_spec=pltpu.PrefetchScalarGridSpec(
            num_scalar_prefetch=1, grid=(n // rows,),
            in_specs=[
                pl.BlockSpec(memory_space=pl.ANY),
                pl.BlockSpec((rows, d), lambda t, pos: (t, 0)),
                pl.BlockSpec((None, N_MOD, d), lambda t, pos: (t // per_seq, 0, 0)),
                pl.BlockSpec((1, d), lambda t, pos: (0, 0)),
            ],
            out_specs=pl.BlockSpec((rows, d), lambda t, pos: (t, 0)),
            scratch_shapes=[pltpu.VMEM((2, rows, y_rows, LANES), F32),
                            pltpu.SemaphoreType.DMA((2,))]),
        compiler_params=pltpu.CompilerParams(
            dimension_semantics=("arbitrary",), vmem_limit_bytes=VMEM_LIMIT),
        name="combine",
    )(pos, y_sorted, x1, mod3, final_g)


def _pad_lanes(a):
    return jnp.pad(a.astype(F32), ((0, 0), (0, LANES - a.shape[1])))


def _pad_heads(w, n_heads, scale=1.0):
    d = w.shape[0]
    w = (w * scale).reshape(d, n_heads, HEAD_DIM)
    w = jnp.concatenate([w, jnp.zeros_like(w)], axis=2)
    return w.reshape(d, n_heads * LANES)


def _bias_selectors(n_heads):
    rows = 3 * LANES
    sel_q = np.zeros((rows, n_heads * LANES), np.float32)
    sel_k = np.zeros((rows, n_heads * LANES), np.float32)
    ones_row = n_heads
    for h in range(n_heads):
        base = h * LANES + HEAD_DIM
        for j in range(3):
            sel_q[j * LANES + h, base + j] = 1.0
            sel_q[ones_row, base + 3 + j] = 1.0
            sel_k[ones_row, base + j] = 1.0
            sel_k[j * LANES + h, base + 3 + j] = -1.0
    return sel_q, sel_k


def kernel(x, c, ada_w, ada_b, norm1_g, w_in, b_forget, w_branch_fox, w_branch_sb, w_out,
           norm2_g, router_group_w, router_group_b, router_expert_w, router_expert_b,
           expert_w_gate, expert_w_up, expert_w_down, final_g):
    b, s, d = x.shape
    depth = ada_w.shape[0]
    hf = b_forget.shape[1]
    fw = hf * HEAD_DIM
    sw = w_branch_sb.shape[1]
    hs = sw // HEAD_DIM
    scale = HEAD_DIM ** -0.5
    n_exp = expert_w_gate.shape[1]

    assert depth == 1, "the final RMSNorm is fused into the single layer's MoE kernel"
    for l in range(depth):
        mod = _mod_call(c, ada_w[l], ada_b[l])
        mod3 = mod.reshape(b, N_MOD, d)

        wl = w_in[l]
        o = 0
        wq_a = wl[:, o:o + fw]; o += fw
        wk_a = wl[:, o:o + fw]; o += fw
        wv_a = wl[:, o:o + fw]; o += fw
        wf = wl[:, o:o + hf]; o += hf
        wq_b = wl[:, o:o + sw]; o += sw
        wk_b = wl[:, o:o + sw]; o += sw
        wv_b = wl[:, o:o + sw]; o += sw
        w_gates = wl[:, o:].astype(BF16)
        w_all = jnp.concatenate([
            _pad_heads(wq_a, hf, scale), _pad_heads(wk_a, hf),
            _pad_heads(wq_b, hs, scale), _pad_heads(wk_b, hs), wv_a, wv_b], axis=1).astype(BF16)
        sel_q, sel_k = _bias_selectors(hf)
        sel = jnp.concatenate([sel_q, sel_k], axis=1).astype(BF16)
        w_f = _pad_lanes(wf).astype(BF16)
        b_f = _pad_lanes(b_forget[l].reshape(1, hf))

        q_a_off = 0
        k_a_off = hf * LANES
        q_b_off = 2 * hf * LANES
        k_b_off = q_b_off + hs * LANES
        v_a_off = k_b_off + hs * LANES
        v_b_off = v_a_off + fw

        proj, gates = _in_call(x, mod3, norm1_g[l].reshape(1, d), w_all, w_gates, w_f, b_f, sel,
                               hf, tm=min(s, 512))
        tq = min(s, 512)
        o_fox = _fox_call(proj, hf, q_a_off, k_a_off, v_a_off, tq, min(tq, 512))
        o_sb = _sb_call(proj, hs, q_b_off, k_b_off, v_b_off, min(s, 512), min(s, 256), min(s, 512))

        wr = _pad_lanes(jnp.concatenate([router_group_w[l], router_expert_w[l]], axis=1))
        wr_hi = wr.astype(BF16)
        wr_lo = (wr - wr_hi.astype(F32)).astype(BF16)
        rb = _pad_lanes(jnp.concatenate([router_group_b[l], router_expert_b[l]]).reshape(1, -1))

        x1, slabs, meta, counts = _post_call(
            o_fox, o_sb, gates, x, mod3, w_branch_fox[l].astype(BF16), w_branch_sb[l].astype(BF16),
            w_out[l].astype(BF16), norm2_g[l].reshape(1, d), wr_hi, wr_lo, rb, tm=min(s, 512))

        n = b * s
        tm_e = MOE_TILE
        n_cls = N_GROUPS * N_PAIRS
        n_pad = n + n_cls * tm_e
        cls = meta[..., 2].reshape(n).astype(jnp.int32)
        rank = meta[..., 3].reshape(n).astype(jnp.int32)
        cnt = counts[0, :n_cls].astype(jnp.int32)
        padded = (cnt + tm_e - 1) // tm_e * tm_e
        ends = jnp.cumsum(padded)
        starts = ends - padded
        cls_ids = jnp.arange(n_cls, dtype=jnp.int32)
        pos = jnp.sum(jnp.where(cls[:, None] == cls_ids[None, :], starts[None, :], 0), axis=1) + rank
        n_valid = ends[-1] // tm_e
        tile_ix = jnp.minimum(jnp.arange(n_pad // tm_e, dtype=jnp.int32), n_valid - 1)
        tile_cls = jnp.sum((tile_ix * tm_e)[:, None] >= ends[None, :], axis=1)
        pair_lo = jnp.asarray([0, 0, 0, 1, 1, 2], jnp.int32)
        pair_hi = jnp.asarray([1, 2, 3, 2, 3, 3], jnp.int32)
        tile_grp = tile_cls // N_PAIRS * EXPERTS_PER_GROUP
        tile_elo = tile_grp + pair_lo[tile_cls % N_PAIRS]
        tile_ehi = tile_grp + pair_hi[tile_cls % N_PAIRS]

        rows = min(s, PERMUTE_ROWS)
        sorted_slabs = _dispatch_call(slabs.reshape(n, SLAB_ROWS, LANES), pos, n_pad, rows)
        y_sorted = _moe_call(sorted_slabs, tile_elo, tile_ehi, tile_ix, n_valid.reshape(1),
                             expert_w_gate[l].astype(BF16), expert_w_up[l].astype(BF16),
                             expert_w_down[l].astype(BF16), tm_e)
        out = _combine_call(y_sorted, pos, x1.reshape(n, d), mod3, final_g.reshape(1, d), s, rows)
        x = out.reshape(b, s, d)
    return x
```

```python
import functools

import jax
import jax.numpy as jnp
import numpy as np
from jax import lax
from jax.experimental import pallas as pl
from jax.experimental.pallas import tpu as pltpu

F32 = jnp.float32
BF16 = jnp.bfloat16

HEAD_DIM = 64
LANES = 128
RMS_EPS = 1e-6
N_MOD = 6
N_GROUPS = 4
EXPERTS_PER_GROUP = 4
N_PAIRS = 6
MOE_TILE = 256
PERMUTE_ROWS = 1024
SLAB_ROWS = 16
NEG_BIG = -1e30
LOG2E = 1.4426950408889634
SB_ZERO_BELOW = -104.0
VMEM_LIMIT = 56 * 1024 * 1024


def _dot(a, b):
    return jnp.dot(a, b, preferred_element_type=F32)


def _dot_nt(a, b):
    return lax.dot_general(a, b, (((1,), (1,)), ((), ())), preferred_element_type=F32)


def _split2(x):
    hi = x.astype(BF16)
    lo = (x - hi.astype(F32)).astype(BF16)
    return hi, lo


def _split3(x):
    hi = x.astype(BF16)
    r = x - hi.astype(F32)
    mid = r.astype(BF16)
    lo = (r - mid.astype(F32)).astype(BF16)
    return hi, mid, lo


def _log_sigmoid(z):
    return jnp.minimum(z, 0.0) - jnp.log1p(jnp.exp(-jnp.abs(z)))


def _rms_scale(x):
    return x * lax.rsqrt(jnp.mean(x * x, axis=-1, keepdims=True) + RMS_EPS)


def _mod_kernel(c_ref, w_ref, b_ref, o_ref):
    c = c_ref[...]
    ca = c * jax.nn.sigmoid(c)
    c_hi, c_mid, c_lo = _split3(ca)
    w_hi, w_mid, w_lo = _split3(w_ref[...])
    acc = _dot(c_hi, w_hi) + _dot(c_hi, w_mid) + _dot(c_mid, w_hi)
    acc += _dot(c_mid, w_mid) + _dot(c_hi, w_lo) + _dot(c_lo, w_hi)
    o_ref[...] = acc + b_ref[...]


def _mod_call(c, ada_w, ada_b):
    b, d = c.shape
    n = ada_w.shape[1]
    bp = 16
    tn = n // N_MOD
    c_pad = jnp.pad(c, ((0, bp - b), (0, 0)))
    out = pl.pallas_call(
        _mod_kernel,
        out_shape=jax.ShapeDtypeStruct((bp, n), F32),
        grid=(n // tn,),
        in_specs=[
            pl.BlockSpec((bp, d), lambda j: (0, 0)),
            pl.BlockSpec((d, tn), lambda j: (0, j)),
            pl.BlockSpec((1, tn), lambda j: (0, j)),
        ],
        out_specs=pl.BlockSpec((bp, tn), lambda j: (0, j)),
        compiler_params=pltpu.CompilerParams(
            dimension_semantics=("arbitrary",), vmem_limit_bytes=VMEM_LIMIT),
        name="mod",
    )(c_pad, ada_w, ada_b.reshape(1, n))
    return out[:b]


def _in_kernel(n_heads, n_aug, chunk, x_ref, mod_ref, g_ref, w_ref, wg_ref, wf_ref, bf_ref,
               sel_ref, proj_ref, gates_ref, carry_ref):
    i = pl.program_id(1)
    tm = x_ref.shape[0]

    @pl.when(i == 0)
    def _():
        carry_ref[...] = jnp.zeros_like(carry_ref)

    x = x_ref[...]
    shift = mod_ref[0:1, :]
    scale = mod_ref[1:2, :]
    h = _rms_scale(x) * g_ref[...] * (1.0 + scale) + shift
    hb = h.astype(BF16)

    lane = lax.broadcasted_iota(jnp.int32, (tm, LANES), 1)
    lf = _log_sigmoid(_dot(hb, wf_ref[...]) + bf_ref[...])
    lf = jnp.where(lane < n_heads, lf, 0.0)
    row = lax.broadcasted_iota(jnp.int32, (tm, tm), 0)
    col = lax.broadcasted_iota(jnp.int32, (tm, tm), 1)
    tri = jnp.where(col <= row, 1.0, 0.0).astype(BF16)
    l_hi, l_mid, l_lo = _split3(lf)
    fcum = _dot(tri, l_hi) + _dot(tri, l_mid) + _dot(tri, l_lo) + carry_ref[0:1, :]
    carry_ref[...] = jnp.broadcast_to(fcum[tm - 1:tm, :], carry_ref.shape)
    fx = jnp.where(lane == n_heads, 1.0, fcum)
    f_hi, f_mid, f_lo = _split3(fx)
    hml = jnp.concatenate([f_hi, f_mid, f_lo], axis=1)

    n_proj = w_ref.shape[1]
    for c0 in range(0, n_proj, chunk):
        acc = _dot(hb, w_ref[:, c0:c0 + chunk])
        if c0 < n_aug:
            acc += _dot(hml, sel_ref[:, c0:c0 + chunk])
        proj_ref[:, c0:c0 + chunk] = acc.astype(BF16)
    n_gate = wg_ref.shape[1]
    for c0 in range(0, n_gate, chunk):
        gates_ref[:, c0:c0 + chunk] = _dot(hb, wg_ref[:, c0:c0 + chunk])


def _in_call(x, mod3, norm_g, w_all, w_gates, w_f, b_f, sel, n_heads, tm):
    b, s, d = x.shape
    n_proj = w_all.shape[1]
    n_gate = w_gates.shape[1]
    n_aug = sel.shape[1]
    const = lambda shape: pl.BlockSpec(shape, lambda bi, i: (0,) * len(shape),
                                       pipeline_mode=pl.Buffered(1))
    return pl.pallas_call(
        functools.partial(_in_kernel, n_heads, n_aug, 512),
        out_shape=(jax.ShapeDtypeStruct((b, s, n_proj), BF16),
                   jax.ShapeDtypeStruct((b, s, n_gate), F32)),
        grid=(b, s // tm),
        in_specs=[
            pl.BlockSpec((None, tm, d), lambda bi, i: (bi, i, 0)),
            pl.BlockSpec((None, N_MOD, d), lambda bi, i: (bi, 0, 0)),
            const((1, d)),
            const((d, n_proj)),
            const((d, n_gate)),
            const((d, LANES)),
            const((1, LANES)),
            const((3 * LANES, n_aug)),
        ],
        out_specs=(pl.BlockSpec((None, tm, n_proj), lambda bi, i: (bi, i, 0)),
                   pl.BlockSpec((None, tm, n_gate), lambda bi, i: (bi, i, 0))),
        scratch_shapes=[pltpu.VMEM((8, LANES), F32)],
        compiler_params=pltpu.CompilerParams(
            dimension_semantics=("arbitrary", "arbitrary"), vmem_limit_bytes=VMEM_LIMIT),
        name="in_proj",
    )(x, mod3, norm_g, w_all, w_gates, w_f, b_f, sel)


def _fox_kernel(tk, q_ref, k_ref, v_ref, o_ref):
    i = pl.program_id(2)
    tq = q_ref.shape[0]
    n_diag = tq // tk
    row = lax.broadcasted_iota(jnp.int32, (tq, tk), 0)
    col = lax.broadcasted_iota(jnp.int32, (tq, tk), 1)
    lane = lax.broadcasted_iota(jnp.int32, (tq, LANES), 1)

    def head_step(hh, j, state, masked):
        m, l, acc = state
        q = q_ref[:, hh * LANES:(hh + 1) * LANES]
        k0 = pl.multiple_of(j * tk, tk)
        k = k_ref[pl.ds(k0, tk), hh * LANES:(hh + 1) * LANES]
        s = _dot_nt(q, k)
        if masked:
            s = jnp.where(col + (j - i * n_diag) * tk <= row, s, NEG_BIG)
        m_new = jnp.maximum(m, jnp.max(s, axis=1, keepdims=True))
        alpha = jnp.exp(m - m_new)
        p = jnp.exp(s - m_new)
        l = alpha * l + jnp.sum(p, axis=1, keepdims=True)
        acc = alpha * acc + _dot(p.astype(BF16), v_ref[pl.ds(k0, tk), :])
        return m_new, l, acc

    def pair_step(j, states, masked):
        return tuple(head_step(hh, j, st, masked) for hh, st in enumerate(states))

    init = (jnp.full((tq, 1), NEG_BIG, F32), jnp.zeros((tq, 1), F32),
            jnp.zeros((tq, LANES), F32))
    states = lax.fori_loop(0, i * n_diag, functools.partial(pair_step, masked=False),
                           (init, init))
    for dj in range(n_diag):
        states = pair_step(i * n_diag + dj, states, True)
    outs = [acc / l for _, l, acc in states]
    o_ref[...] = jnp.where(lane < HEAD_DIM, outs[0], outs[1]).astype(o_ref.dtype)


def _fox_call(proj, n_heads, q_off, k_off, v_off, tq, tk):
    b, s, _ = proj.shape
    n_pairs = n_heads // 2
    pw = 2 * LANES
    return pl.pallas_call(
        functools.partial(_fox_kernel, tk),
        out_shape=jax.ShapeDtypeStruct((b, s, n_heads * HEAD_DIM), BF16),
        grid=(b, n_pairs, s // tq),
        in_specs=[
            pl.BlockSpec((None, tq, pw), lambda bi, p, i: (bi, i, q_off // pw + p)),
            pl.BlockSpec((None, s, pw), lambda bi, p, i: (bi, 0, k_off // pw + p)),
            pl.BlockSpec((None, s, LANES), lambda bi, p, i: (bi, 0, v_off // LANES + p)),
        ],
        out_specs=pl.BlockSpec((None, tq, LANES), lambda bi, p, i: (bi, i, p)),
        compiler_params=pltpu.CompilerParams(
            dimension_semantics=("arbitrary", "arbitrary", "arbitrary"),
            vmem_limit_bytes=VMEM_LIMIT),
        name="fox_attn",
    )(proj, proj, proj)


def _sb_kernel(tk, tr, q_ref, k_ref, v_ref, o_ref):
    i = pl.program_id(2)
    tq = q_ref.shape[0]
    n_chunks = tq // tr
    n_diag = tr // tk
    row = lax.broadcasted_iota(jnp.int32, (tr, tk), 0)
    col = lax.broadcasted_iota(jnp.int32, (tr, tk), 1)
    lane = lax.broadcasted_iota(jnp.int32, (tq, LANES), 1)
    ur = lax.broadcasted_iota(jnp.int32, (2 * tk, tk), 0)
    uc = lax.broadcasted_iota(jnp.int32, (2 * tk, tk), 1)
    upper2 = jnp.where((ur & (tk - 1)) > uc, 1.0, 0.0).astype(BF16)
    chains = [(hh, r) for r in range(n_chunks) for hh in range(2)]

    def chain_step(hh, r, j, state, dj=None):
        masked = dj is not None
        c, acc = state
        q = q_ref[r * tr:(r + 1) * tr, hh * LANES:(hh + 1) * LANES]
        k0 = pl.multiple_of(j * tk, tk)
        k = k_ref[pl.ds(k0, tk), hh * LANES:(hh + 1) * LANES]
        z = _dot_nt(q, k)
        t = jnp.exp2(jnp.abs(z) * (-LOG2E))
        ls = jnp.minimum(z, 0.0) - jnp.log(1.0 + t)
        lom = ls - z
        if masked:
            strict = col + dj * tk < row
            lom = jnp.where(strict, lom, 0.0)
        hi = pltpu.bitcast(pltpu.bitcast(lom, jnp.uint32) & jnp.uint32(0xFFFF0000), F32)
        parts = jnp.concatenate([hi.astype(BF16), (lom - hi).astype(BF16)], axis=1)
        within = _dot(parts, upper2)
        a = jnp.exp2((ls + within + c) * LOG2E)
        if masked:
            a = jnp.where(strict, a, 0.0)
        acc = acc + _dot(a.astype(BF16), v_ref[pl.ds(k0, tk), :])
        c = c + within[:, 0:1] + lom[:, 0:1]
        return c, acc

    base = i * (tq // tk)
    states = []
    for hh, r in chains:
        st = (jnp.zeros((tr, 1), F32), jnp.zeros((tr, LANES), F32))
        for dj in reversed(range(n_diag)):
            st = chain_step(hh, r, base + r * n_diag + dj, st, dj)
        for j in reversed(range(r * n_diag)):
            st = chain_step(hh, r, base + j, st)
        states.append(st)

    def c_max(sts):
        return functools.reduce(jnp.maximum, [jnp.max(st[0]) for st in sts])

    def cond(carry):
        t, cm, _ = carry
        return (t < base) & (cm >= SB_ZERO_BELOW)

    def body(carry):
        t, _, sts = carry
        j = base - 1 - t
        sts = tuple(chain_step(hh, r, j, st) for (hh, r), st in zip(chains, sts))
        return t + 1, c_max(sts), sts

    states = tuple(states)
    _, _, states = lax.while_loop(cond, body, (jnp.int32(0), c_max(states), states))
    acc = {ch: st[1] for ch, st in zip(chains, states)}
    o0 = jnp.concatenate([acc[(0, r)] for r in range(n_chunks)], axis=0)
    o1 = jnp.concatenate([acc[(1, r)] for r in range(n_chunks)], axis=0)
    o_ref[...] = jnp.where(lane < HEAD_DIM, o0, o1).astype(o_ref.dtype)


def _sb_call(proj, n_heads, q_off, k_off, v_off, tq, tk, tr):
    b, s, _ = proj.shape
    n_pairs = n_heads // 2
    pw = 2 * LANES
    return pl.pallas_call(
        functools.partial(_sb_kernel, tk, tr),
        out_shape=jax.ShapeDtypeStruct((b, s, n_heads * HEAD_DIM), BF16),
        grid=(b, n_pairs, s // tq),
        in_specs=[
            pl.BlockSpec((None, tq, pw), lambda bi, p, i: (bi, i, q_off // pw + p)),
            pl.BlockSpec((None, s, pw), lambda bi, p, i: (bi, 0, k_off // pw + p)),
            pl.BlockSpec((None, s, LANES), lambda bi, p, i: (bi, 0, v_off // LANES + p)),
        ],
        out_specs=pl.BlockSpec((None, tq, LANES), lambda bi, p, i: (bi, i, p)),
        compiler_params=pltpu.CompilerParams(
            dimension_semantics=("arbitrary", "arbitrary", "arbitrary"),
            vmem_limit_bytes=VMEM_LIMIT),
        name="sb_attn",
    )(proj, proj, proj)


def _route(logits):
    tm = logits.shape[0]
    lane = lax.broadcasted_iota(jnp.int32, (tm, LANES), 1)
    big = jnp.int32(LANES)
    g_mask = lane < N_GROUPS
    lg = jnp.where(g_mask, logits, -jnp.inf)
    g_max = jnp.max(lg, axis=1, keepdims=True)
    g_exp = jnp.exp(lg - g_max)
    g_prob = g_exp / jnp.sum(g_exp, axis=1, keepdims=True)
    g_top = jnp.max(g_prob, axis=1, keepdims=True)
    g_idx = jnp.min(jnp.where(g_mask & (g_prob == g_top), lane, big), axis=1, keepdims=True)
    lo = N_GROUPS + EXPERTS_PER_GROUP * g_idx
    e_mask = (lane >= lo) & (lane < lo + EXPERTS_PER_GROUP)
    le = jnp.where(e_mask, logits, -jnp.inf)
    e_max = jnp.max(le, axis=1, keepdims=True)
    e_exp = jnp.exp(le - e_max)
    e_prob = e_exp / jnp.sum(e_exp, axis=1, keepdims=True)
    p1 = jnp.max(e_prob, axis=1, keepdims=True)
    i1 = jnp.min(jnp.where(e_mask & (e_prob == p1), lane, big), axis=1, keepdims=True)
    rest = e_mask & (lane != i1)
    p2 = jnp.max(jnp.where(rest, e_prob, -1.0), axis=1, keepdims=True)
    i2 = jnp.min(jnp.where(rest & (e_prob == p2), lane, big), axis=1, keepdims=True)
    tot = p1 + p2
    w1 = p1 / tot * g_top
    w2 = p2 / tot * g_top
    first_lower = i1 < i2
    e_lo = jnp.minimum(i1, i2) - lo
    e_hi = jnp.maximum(i1, i2) - lo
    pair = jnp.where(e_lo == 0, 0, jnp.where(e_lo == 1, 3, 5)) + (e_hi - e_lo - 1)
    cls = g_idx * N_PAIRS + pair
    return jnp.where(first_lower, w1, w2), jnp.where(first_lower, w2, w1), cls


def _post_kernel(of_ref, os_ref, gates_ref, x_ref, mod_ref, wbf_ref, wbs_ref, wo_ref, g2_ref,
                 wr_hi_ref, wr_lo_ref, rb_ref, x1_ref, slab_ref, meta_ref, cnt_ref, carry_ref):
    first = (pl.program_id(0) == 0) & (pl.program_id(1) == 0)

    @pl.when(first)
    def _():
        carry_ref[...] = jnp.zeros_like(carry_ref)

    tm = x_ref.shape[0]
    d = x_ref.shape[1]
    bf = _dot(of_ref[...], wbf_ref[...])
    bs = _dot(os_ref[...], wbs_ref[...])
    merged = jax.nn.sigmoid(gates_ref[:, :d]) * bf + jax.nn.sigmoid(gates_ref[:, d:]) * bs
    y = _dot(merged.astype(BF16), wo_ref[...])
    x1 = x_ref[...] + mod_ref[2:3, :] * y
    x1_ref[...] = x1
    h2 = _rms_scale(x1) * g2_ref[...] * (1.0 + mod_ref[4:5, :]) + mod_ref[3:4, :]
    h_hi, h_lo = _split2(h2)
    w_hi = wr_hi_ref[...]
    logits = _dot(h_hi, w_hi) + _dot(h_hi, wr_lo_ref[...]) + _dot(h_lo, w_hi) + rb_ref[...]
    w_first, w_second, cls = _route(logits)

    lane = lax.broadcasted_iota(jnp.int32, (tm, LANES), 1)
    onehot = lane == cls
    oh = jnp.where(onehot, 1.0, 0.0)
    row = lax.broadcasted_iota(jnp.int32, (tm, tm), 0)
    col = lax.broadcasted_iota(jnp.int32, (tm, tm), 1)
    earlier = jnp.where(col < row, 1.0, 0.0).astype(BF16)
    before = _dot(earlier, oh.astype(BF16)) + carry_ref[0:1, :]
    rank = jnp.sum(jnp.where(onehot, before, 0.0), axis=1, keepdims=True)
    counts = carry_ref[0:1, :] + jnp.sum(oh, axis=0, keepdims=True)
    carry_ref[...] = jnp.broadcast_to(counts, carry_ref.shape)
    cnt_ref[...] = jnp.broadcast_to(counts, cnt_ref.shape)

    meta = (jnp.where(lane == 0, w_first, 0.0) + jnp.where(lane == 1, w_second, 0.0)
            + jnp.where(lane == 2, cls.astype(F32), 0.0) + jnp.where(lane == 3, rank, 0.0))
    meta_ref[...] = meta
    h_rows = d // LANES
    for s in range(h_rows):
        slab_ref[:, s, :] = h2[:, s * LANES:(s + 1) * LANES]
    slab_ref[:, h_rows, :] = meta
    for s in range(h_rows + 1, SLAB_ROWS):
        slab_ref[:, s, :] = jnp.zeros((tm, LANES), F32)


def _post_call(o_fox, o_sb, gates, x, mod3, w_bf, w_bs, w_o, norm_g, wr_hi, wr_lo, rb, tm):
    b, s, d = x.shape
    wdt = o_fox.shape[2]
    const = lambda shape: pl.BlockSpec(shape, lambda bi, i: (0,) * len(shape),
                                       pipeline_mode=pl.Buffered(1))
    tile = lambda w: pl.BlockSpec((None, tm, w), lambda bi, i: (bi, i, 0))
    return pl.pallas_call(
        _post_kernel,
        out_shape=(jax.ShapeDtypeStruct((b, s, d), F32),
                   jax.ShapeDtypeStruct((b, s, SLAB_ROWS, LANES), F32),
                   jax.ShapeDtypeStruct((b, s, LANES), F32),
                   jax.ShapeDtypeStruct((8, LANES), F32)),
        grid=(b, s // tm),
        in_specs=[
            tile(wdt), tile(wdt), tile(2 * d), tile(d),
            pl.BlockSpec((None, N_MOD, d), lambda bi, i: (bi, 0, 0)),
            const((wdt, d)), const((wdt, d)), const((d, d)), const((1, d)),
            const((d, LANES)), const((d, LANES)), const((1, LANES)),
        ],
        out_specs=(tile(d),
                   pl.BlockSpec((None, tm, SLAB_ROWS, LANES), lambda bi, i: (bi, i, 0, 0)),
                   tile(LANES),
                   pl.BlockSpec((8, LANES), lambda bi, i: (0, 0))),
        scratch_shapes=[pltpu.VMEM((8, LANES), F32)],
        compiler_params=pltpu.CompilerParams(
            dimension_semantics=("arbitrary", "arbitrary"), vmem_limit_bytes=VMEM_LIMIT),
        name="post",
    )(o_fox, o_sb, gates, x, mod3, w_bf, w_bs, w_o, norm_g, wr_hi, wr_lo, rb)


def _dispatch_kernel(pos_ref, src_ref, init_ref, dst_ref, sem):
    del init_ref
    rows = src_ref.shape[0]
    base = pl.program_id(0) * rows

    def issue(r, carry):
        pltpu.make_async_copy(src_ref.at[r], dst_ref.at[pos_ref[base + r]], sem).start()
        return carry

    lax.fori_loop(0, rows, issue, 0, unroll=8)
    pltpu.make_async_copy(src_ref, dst_ref.at[pl.ds(0, rows)], sem).wait()


def _dispatch_call(slabs, pos, n_pad, rows):
    n = slabs.shape[0]
    any_spec = pl.BlockSpec(memory_space=pl.ANY)
    return pl.pallas_call(
        _dispatch_kernel,
        out_shape=jax.ShapeDtypeStruct((n_pad,) + slabs.shape[1:], slabs.dtype),
        grid_spec=pltpu.PrefetchScalarGridSpec(
            num_scalar_prefetch=1, grid=(n // rows,),
            in_specs=[pl.BlockSpec((rows,) + slabs.shape[1:], lambda t, pos: (t, 0, 0)), any_spec],
            out_specs=any_spec,
            scratch_shapes=[pltpu.SemaphoreType.DMA(())]),
        input_output_aliases={2: 0},
        compiler_params=pltpu.CompilerParams(
            dimension_semantics=("arbitrary",), has_side_effects=True,
            vmem_limit_bytes=VMEM_LIMIT),
        name="dispatch",
    )(pos, slabs, jnp.zeros((n_pad,) + slabs.shape[1:], slabs.dtype))


def _moe_kernel(elo_ref, ehi_ref, tix_ref, nv_ref, slab_ref, wg0_ref, wu0_ref, wd0_ref,
                wg1_ref, wu1_ref, wd1_ref, y_ref):
    y_rows = y_ref.shape[1]
    valid = pl.program_id(0) < nv_ref[0]

    @pl.when(jnp.logical_not(valid))
    def _():
        y_ref[...] = jnp.zeros_like(y_ref)

    @pl.when(valid)
    def _():
        h = jnp.concatenate([slab_ref[:, s, :] for s in range(y_rows)], axis=1).astype(BF16)
        meta = slab_ref[:, y_rows, :]

        def expert(wg_ref, wu_ref, wd_ref):
            g = _dot(h, wg_ref[...])
            u = _dot(h, wu_ref[...])
            act = (g * jax.nn.sigmoid(g) * u).astype(BF16)
            return _dot(act, wd_ref[...])

        y = (meta[:, 0:1] * expert(wg0_ref, wu0_ref, wd0_ref)
             + meta[:, 1:2] * expert(wg1_ref, wu1_ref, wd1_ref))
        for s in range(y_rows):
            y_ref[:, s, :] = y[:, s * LANES:(s + 1) * LANES]


def _moe_call(slabs, tile_elo, tile_ehi, tile_ix, n_valid, w_gate, w_up, w_down, tm):
    n_pad = slabs.shape[0]
    _, d, f = w_gate.shape
    lo = lambda t, elo, ehi, tix, nv: (elo[t], 0, 0)
    hi = lambda t, elo, ehi, tix, nv: (ehi[t], 0, 0)
    return pl.pallas_call(
        _moe_kernel,
        out_shape=jax.ShapeDtypeStruct((n_pad, d // LANES, LANES), F32),
        grid_spec=pltpu.PrefetchScalarGridSpec(
            num_scalar_prefetch=4, grid=(n_pad // tm,),
            in_specs=[
                pl.BlockSpec((tm,) + slabs.shape[1:], lambda t, elo, ehi, tix, nv: (tix[t], 0, 0)),
                pl.BlockSpec((None, d, f), lo), pl.BlockSpec((None, d, f), lo),
                pl.BlockSpec((None, f, d), lo),
                pl.BlockSpec((None, d, f), hi), pl.BlockSpec((None, d, f), hi),
                pl.BlockSpec((None, f, d), hi),
            ],
            out_specs=pl.BlockSpec((tm, d // LANES, LANES),
                                   lambda t, elo, ehi, tix, nv: (t, 0, 0))),
        compiler_params=pltpu.CompilerParams(
            dimension_semantics=("arbitrary",), vmem_limit_bytes=VMEM_LIMIT),
        name="moe",
    )(tile_elo, tile_ehi, tile_ix, n_valid, slabs, w_gate, w_up, w_down, w_gate, w_up, w_down)


def _combine_kernel(pos_ref, ys_ref, x1_ref, mod_ref, fg_ref, o_ref, ybuf, sem):
    rows = x1_ref.shape[0]
    y_rows = ybuf.shape[2]
    t = pl.program_id(0)

    def issue(step, slot):
        base = step * rows

        def body(r, carry):
            pltpu.make_async_copy(ys_ref.at[pos_ref[base + r]], ybuf.at[slot, r],
                                  sem.at[slot]).start()
            return carry

        lax.fori_loop(0, rows, body, 0, unroll=8)

    @pl.when(t == 0)
    def _():
        issue(0, 0)

    @pl.when(t + 1 < pl.num_programs(0))
    def _():
        issue(t + 1, (t + 1) % 2)

    slot = t % 2
    pltpu.make_async_copy(ys_ref.at[pl.ds(0, rows)], ybuf.at[slot], sem.at[slot]).wait()
    y = jnp.concatenate([ybuf[slot, :, s, :] for s in range(y_rows)], axis=1)
    x2 = x1_ref[...] + mod_ref[5:6, :] * y
    o_ref[...] = _rms_scale(x2) * fg_ref[...]


def _combine_call(y_sorted, pos, x1, mod3, final_g, seq, rows):
    n, d = x1.shape
    y_rows = y_sorted.shape[1]
    per_seq = seq // rows
    return pl.pallas_call(
        _combine_kernel,
        out_shape=jax.ShapeDtypeStruct((n, d), F32),
        grid_spec=pltpu.PrefetchScalarGridSpec(
            num_scalar_prefetch=1, grid=(n // rows,),
            in_specs=[
                pl.BlockSpec(memory_space=pl.ANY),
                pl.BlockSpec((rows, d), lambda t, pos: (t, 0)),
                pl.BlockSpec((None, N_MOD, d), lambda t, pos: (t // per_seq, 0, 0)),
                pl.BlockSpec((1, d), lambda t, pos: (0, 0)),
            ],
            out_specs=pl.BlockSpec((rows, d), lambda t, pos: (t, 0)),
            scratch_shapes=[pltpu.VMEM((2, rows, y_rows, LANES), F32),
                            pltpu.SemaphoreType.DMA((2,))]),
        compiler_params=pltpu.CompilerParams(
            dimension_semantics=("arbitrary",), vmem_limit_bytes=VMEM_LIMIT),
        name="combine",
    )(pos, y_sorted, x1, mod3, final_g)


def _pad_lanes(a):
    return jnp.pad(a.astype(F32), ((0, 0), (0, LANES - a.shape[1])))


def _pad_heads(w, n_heads, scale=1.0):
    d = w.shape[0]
    w = (w * scale).reshape(d, n_heads, HEAD_DIM)
    w = jnp.concatenate([w, jnp.zeros_like(w)], axis=2)
    return w.reshape(d, n_heads * LANES)


def _bias_selectors(n_heads):
    rows = 3 * LANES
    sel_q = np.zeros((rows, n_heads * LANES), np.float32)
    sel_k = np.zeros((rows, n_heads * LANES), np.float32)
    ones_row = n_heads
    for h in range(n_heads):
        base = h * LANES + HEAD_DIM
        for j in range(3):
            sel_q[j * LANES + h, base + j] = 1.0
            sel_q[ones_row, base + 3 + j] = 1.0
            sel_k[ones_row, base + j] = 1.0
            sel_k[j * LANES + h, base + 3 + j] = -1.0
    return sel_q, sel_k


def kernel(x, c, ada_w, ada_b, norm1_g, w_in, b_forget, w_branch_fox, w_branch_sb, w_out,
           norm2_g, router_group_w, router_group_b, router_expert_w, router_expert_b,
           expert_w_gate, expert_w_up, expert_w_down, final_g):
    b, s, d = x.shape
    depth = ada_w.shape[0]
    hf = b_forget.shape[1]
    fw = hf * HEAD_DIM
    sw = w_branch_sb.shape[1]
    hs = sw // HEAD_DIM
    scale = HEAD_DIM ** -0.5
    n_exp = expert_w_gate.shape[1]

    assert depth == 1, "the final RMSNorm is fused into the single layer's MoE kernel"
    for l in range(depth):
        mod = _mod_call(c, ada_w[l], ada_b[l])
        mod3 = mod.reshape(b, N_MOD, d)

        wl = w_in[l]
        o = 0
        wq_a = wl[:, o:o + fw]; o += fw
        wk_a = wl[:, o:o + fw]; o += fw
        wv_a = wl[:, o:o + fw]; o += fw
        wf = wl[:, o:o + hf]; o += hf
        wq_b = wl[:, o:o + sw]; o += sw
        wk_b = wl[:, o:o + sw]; o += sw
        wv_b = wl[:, o:o + sw]; o += sw
        w_gates = wl[:, o:].astype(BF16)
        w_all = jnp.concatenate([
            _pad_heads(wq_a, hf, scale), _pad_heads(wk_a, hf),
            _pad_heads(wq_b, hs, scale), _pad_heads(wk_b, hs), wv_a, wv_b], axis=1).astype(BF16)
        sel_q, sel_k = _bias_selectors(hf)
        sel = jnp.concatenate([sel_q, sel_k], axis=1).astype(BF16)
        w_f = _pad_lanes(wf).astype(BF16)
        b_f = _pad_lanes(b_forget[l].reshape(1, hf))

        q_a_off = 0
        k_a_off = hf * LANES
        q_b_off = 2 * hf * LANES
        k_b_off = q_b_off + hs * LANES
        v_a_off = k_b_off + hs * LANES
        v_b_off = v_a_off + fw

        proj, gates = _in_call(x, mod3, norm1_g[l].reshape(1, d), w_all, w_gates, w_f, b_f, sel,
                               hf, tm=min(s, 512))
        tq = min(s, 512)
        o_fox = _fox_call(proj, hf, q_a_off, k_a_off, v_a_off, tq, min(tq, 512))
        o_sb = _sb_call(proj, hs, q_b_off, k_b_off, v_b_off, min(s, 512), min(s, 256), min(s, 512))

        wr = _pad_lanes(jnp.concatenate([router_group_w[l], router_expert_w[l]], axis=1))
        wr_hi = wr.astype(BF16)
        wr_lo = (wr - wr_hi.astype(F32)).astype(BF16)
        rb = _pad_lanes(jnp.concatenate([router_group_b[l], router_expert_b[l]]).reshape(1, -1))

        x1, slabs, meta, counts = _post_call(
            o_fox, o_sb, gates, x, mod3, w_branch_fox[l].astype(BF16), w_branch_sb[l].astype(BF16),
            w_out[l].astype(BF16), norm2_g[l].reshape(1, d), wr_hi, wr_lo, rb, tm=min(s, 512))

        n = b * s
        tm_e = MOE_TILE
        n_cls = N_GROUPS * N_PAIRS
        n_pad = n + n_cls * tm_e
        cls = meta[..., 2].reshape(n).astype(jnp.int32)
        rank = meta[..., 3].reshape(n).astype(jnp.int32)
        cnt = counts[0, :n_cls].astype(jnp.int32)
        padded = (cnt + tm_e - 1) // tm_e * tm_e
        ends = jnp.cumsum(padded)
        starts = ends - padded
        cls_ids = jnp.arange(n_cls, dtype=jnp.int32)
        pos = jnp.sum(jnp.where(cls[:, None] == cls_ids[None, :], starts[None, :], 0), axis=1) + rank
        n_valid = ends[-1] // tm_e
        tile_ix = jnp.minimum(jnp.arange(n_pad // tm_e, dtype=jnp.int32), n_valid - 1)
        tile_cls = jnp.sum((tile_ix * tm_e)[:, None] >= ends[None, :], axis=1)
        pair_lo = jnp.asarray([0, 0, 0, 1, 1, 2], jnp.int32)
        pair_hi = jnp.asarray([1, 2, 3, 2, 3, 3], jnp.int32)
        tile_grp = tile_cls // N_PAIRS * EXPERTS_PER_GROUP
        tile_elo = tile_grp + pair_lo[tile_cls % N_PAIRS]
        tile_ehi = tile_grp + pair_hi[tile_cls % N_PAIRS]

        rows = min(s, PERMUTE_ROWS)
        sorted_slabs = _dispatch_call(slabs.reshape(n, SLAB_ROWS, LANES), pos, n_pad, rows)
        y_sorted = _moe_call(sorted_slabs, tile_elo, tile_ehi, tile_ix, n_valid.reshape(1),
                             expert_w_gate[l].astype(BF16), expert_w_up[l].astype(BF16),
                             expert_w_down[l].astype(BF16), tm_e)
        out = _combine_call(y_sorted, pos, x1.reshape(n, d), mod3, final_g.reshape(1, d), s, rows)
        x = out.reshape(b, s, d)
    return x
```

```python
import functools

import jax
import jax.numpy as jnp
import numpy as np
from jax import lax
from jax.experimental import pallas as pl
from jax.experimental.pallas import tpu as pltpu

F32 = jnp.float32
BF16 = jnp.bfloat16

HEAD_DIM = 64
LANES = 128
RMS_EPS = 1e-6
N_MOD = 6
N_GROUPS = 4
EXPERTS_PER_GROUP = 4
N_PAIRS = 6
MOE_TILE = 256
PERMUTE_ROWS = 1024
SLAB_ROWS = 16
NEG_BIG = -1e30
LOG2E = 1.4426950408889634
SB_ZERO_BELOW = -104.0
FOX_ZERO_BELOW = -110.0
NORM_SLACK = 1.02
VMEM_LIMIT = 56 * 1024 * 1024


def _dot(a, b):
    return jnp.dot(a, b, preferred_element_type=F32)


def _dot_nt(a, b):
    return lax.dot_general(a, b, (((1,), (1,)), ((), ())), preferred_element_type=F32)


def _split2(x):
    hi = x.astype(BF16)
    lo = (x - hi.astype(F32)).astype(BF16)
    return hi, lo


def _split3(x):
    hi = x.astype(BF16)
    r = x - hi.astype(F32)
    mid = r.astype(BF16)
    lo = (r - mid.astype(F32)).astype(BF16)
    return hi, mid, lo


def _log_sigmoid(z):
    return jnp.minimum(z, 0.0) - jnp.log1p(jnp.exp(-jnp.abs(z)))


def _rms_scale(x):
    return x * lax.rsqrt(jnp.mean(x * x, axis=-1, keepdims=True) + RMS_EPS)


def _mod_kernel(c_ref, w_ref, b_ref, o_ref):
    c = c_ref[...]
    ca = c * jax.nn.sigmoid(c)
    c_hi, c_mid, c_lo = _split3(ca)
    w_hi, w_mid, w_lo = _split3(w_ref[...])
    acc = _dot(c_hi, w_hi) + _dot(c_hi, w_mid) + _dot(c_mid, w_hi)
    acc += _dot(c_mid, w_mid) + _dot(c_hi, w_lo) + _dot(c_lo, w_hi)
    o_ref[...] = acc + b_ref[...]


def _mod_call(c, ada_w, ada_b):
    b, d = c.shape
    n = ada_w.shape[1]
    bp = 16
    tn = n // N_MOD
    c_pad = jnp.pad(c, ((0, bp - b), (0, 0)))
    out = pl.pallas_call(
        _mod_kernel,
        out_shape=jax.ShapeDtypeStruct((bp, n), F32),
        grid=(n // tn,),
        in_specs=[
            pl.BlockSpec((bp, d), lambda j: (0, 0)),
            pl.BlockSpec((d, tn), lambda j: (0, j)),
            pl.BlockSpec((1, tn), lambda j: (0, j)),
        ],
        out_specs=pl.BlockSpec((bp, tn), lambda j: (0, j)),
        compiler_params=pltpu.CompilerParams(
            dimension_semantics=("arbitrary",), vmem_limit_bytes=VMEM_LIMIT),
        name="mod",
    )(c_pad, ada_w, ada_b.reshape(1, n))
    return out[:b]


def _in_kernel(n_heads, n_aug, chunk, x_ref, mod_ref, g_ref, w_ref, wg_ref, wf_ref, bf_ref,
               sel_ref, proj_ref, gates_ref, stats_ref, carry_ref):
    i = pl.program_id(1)
    tm = x_ref.shape[0]

    @pl.when(i == 0)
    def _():
        carry_ref[...] = jnp.zeros_like(carry_ref)

    x = x_ref[...]
    shift = mod_ref[0:1, :]
    scale = mod_ref[1:2, :]
    h = _rms_scale(x) * g_ref[...] * (1.0 + scale) + shift
    hb = h.astype(BF16)

    lane = lax.broadcasted_iota(jnp.int32, (tm, LANES), 1)
    lf = _log_sigmoid(_dot(hb, wf_ref[...]) + bf_ref[...])
    lf = jnp.where(lane < n_heads, lf, 0.0)
    row = lax.broadcasted_iota(jnp.int32, (tm, tm), 0)
    col = lax.broadcasted_iota(jnp.int32, (tm, tm), 1)
    tri = jnp.where(col <= row, 1.0, 0.0).astype(BF16)
    l_hi, l_mid, l_lo = _split3(lf)
    fcum = _dot(tri, l_hi) + _dot(tri, l_mid) + _dot(tri, l_lo) + carry_ref[0:1, :]
    carry_ref[...] = jnp.broadcast_to(fcum[tm - 1:tm, :], carry_ref.shape)
    fx = jnp.where(lane == n_heads, 1.0, fcum)
    f_hi, f_mid, f_lo = _split3(fx)
    hml = jnp.concatenate([f_hi, f_mid, f_lo], axis=1)

    srow = lax.broadcasted_iota(jnp.int32, (8, LANES), 0)
    slane = lax.broadcasted_iota(jnp.int32, (8, LANES), 1)
    stats = jnp.where(srow == 1, jnp.broadcast_to(fcum[tm - 1:tm, :], (8, LANES)), 0.0)
    k_off = n_heads * LANES

    n_proj = w_ref.shape[1]
    for c0 in range(0, n_proj, chunk):
        acc = _dot(hb, w_ref[:, c0:c0 + chunk])
        if c0 < n_aug:
            acc += _dot(hml, sel_ref[:, c0:c0 + chunk])
        proj_ref[:, c0:c0 + chunk] = acc.astype(BF16)
        if k_off <= c0 < 2 * k_off:
            for hl in range(chunk // LANES):
                kh = acc[:, hl * LANES:(hl + 1) * LANES]
                sq = jnp.sum(jnp.where(lane < HEAD_DIM, kh * kh, 0.0), axis=1, keepdims=True)
                head = (c0 - k_off) // LANES + hl
                top = jnp.max(sq, axis=0, keepdims=True)
                stats = jnp.where((srow == 0) & (slane == head), top, stats)
    stats_ref[...] = stats
    n_gate = wg_ref.shape[1]
    for c0 in range(0, n_gate, chunk):
        gates_ref[:, c0:c0 + chunk] = _dot(hb, wg_ref[:, c0:c0 + chunk])


def _in_call(x, mod3, norm_g, w_all, w_gates, w_f, b_f, sel, n_heads, tm):
    b, s, d = x.shape
    n_proj = w_all.shape[1]
    n_gate = w_gates.shape[1]
    n_aug = sel.shape[1]
    const = lambda shape: pl.BlockSpec(shape, lambda bi, i: (0,) * len(shape),
                                       pipeline_mode=pl.Buffered(1))
    return pl.pallas_call(
        functools.partial(_in_kernel, n_heads, n_aug, 512),
        out_shape=(jax.ShapeDtypeStruct((b, s, n_proj), BF16),
                   jax.ShapeDtypeStruct((b, s, n_gate), F32),
                   jax.ShapeDtypeStruct((b, s // tm, 8, LANES), F32)),
        grid=(b, s // tm),
        in_specs=[
            pl.BlockSpec((None, tm, d), lambda bi, i: (bi, i, 0)),
            pl.BlockSpec((None, N_MOD, d), lambda bi, i: (bi, 0, 0)),
            const((1, d)),
            const((d, n_proj)),
            const((d, n_gate)),
            const((d, LANES)),
            const((1, LANES)),
            const((3 * LANES, n_aug)),
        ],
        out_specs=(pl.BlockSpec((None, tm, n_proj), lambda bi, i: (bi, i, 0)),
                   pl.BlockSpec((None, tm, n_gate), lambda bi, i: (bi, i, 0)),
                   pl.BlockSpec((None, None, 8, LANES), lambda bi, i: (bi, i, 0, 0))),
        scratch_shapes=[pltpu.VMEM((8, LANES), F32)],
        compiler_params=pltpu.CompilerParams(
            dimension_semantics=("arbitrary", "arbitrary"), vmem_limit_bytes=VMEM_LIMIT),
        name="in_proj",
    )(x, mod3, norm_g, w_all, w_gates, w_f, b_f, sel)


def _fox_kernel(tk, n_heads, kn_ref, fe_ref, q_ref, k_ref, v_ref, o_ref):
    bi = pl.program_id(0)
    pair = pl.program_id(1)
    i = pl.program_id(2)
    tq = q_ref.shape[0]
    n_diag = tq // tk
    n_kt = k_ref.shape[0] // tk
    row = lax.broadcasted_iota(jnp.int32, (tq, tk), 0)
    col = lax.broadcasted_iota(jnp.int32, (tq, tk), 1)
    lane = lax.broadcasted_iota(jnp.int32, (tq, LANES), 1)

    q_norm, q_f = [], []
    for hh in range(2):
        qf = q_ref[:, hh * LANES:(hh + 1) * LANES].astype(F32)
        q_norm.append(jnp.sqrt(jnp.sum(jnp.where(lane < HEAD_DIM, qf * qf, 0.0),
                                       axis=1, keepdims=True)))
        q_f.append(jnp.sum(jnp.where((lane >= HEAD_DIM) & (lane < HEAD_DIM + 3), qf, 0.0),
                           axis=1, keepdims=True))

    def may_contribute(j, states):
        worst = []
        for hh, (m, _, _) in enumerate(states):
            idx = (bi * n_kt + j) * n_heads + 2 * pair + hh
            bound = q_norm[hh] * (kn_ref[idx] * NORM_SLACK) + (q_f[hh] - fe_ref[idx]) - m
            worst.append(jnp.max(bound))
        return jnp.maximum(worst[0], worst[1]) >= FOX_ZERO_BELOW

    def head_step(hh, j, state, masked):
        m, l, acc = state
        q = q_ref[:, hh * LANES:(hh + 1) * LANES]
        k0 = pl.multiple_of(j * tk, tk)
        k = k_ref[pl.ds(k0, tk), hh * LANES:(hh + 1) * LANES]
        s = _dot_nt(q, k)
        if masked:
            s = jnp.where(col + (j - i * n_diag) * tk <= row, s, NEG_BIG)
        m_new = jnp.maximum(m, jnp.max(s, axis=1, keepdims=True))
        alpha = jnp.exp(m - m_new)
        p = jnp.exp(s - m_new)
        l = alpha * l + jnp.sum(p, axis=1, keepdims=True)
        acc = alpha * acc + _dot(p.astype(BF16), v_ref[pl.ds(k0, tk), :])
        return m_new, l, acc

    def pair_step(j, states, masked):
        return tuple(head_step(hh, j, st, masked) for hh, st in enumerate(states))

    init = (jnp.full((tq, 1), NEG_BIG, F32), jnp.zeros((tq, 1), F32),
            jnp.zeros((tq, LANES), F32))
    states = (init, init)
    for dj in range(n_diag):
        states = pair_step(i * n_diag + dj, states, True)

    n_full = i * n_diag

    def cond(carry):
        t, go, _ = carry
        return (t < n_full) & go

    def body(carry):
        t, _, sts = carry
        sts = pair_step(n_full - 1 - t, sts, False)
        return t + 1, may_contribute(jnp.maximum(n_full - 2 - t, 0), sts), sts

    go0 = may_contribute(jnp.maximum(n_full - 1, 0), states)
    _, _, states = lax.while_loop(cond, body, (jnp.int32(0), go0, states))
    outs = [acc / l for _, l, acc in states]
    o_ref[...] = jnp.where(lane < HEAD_DIM, outs[0], outs[1]).astype(o_ref.dtype)


def _fox_call(proj, stats, n_heads, q_off, k_off, v_off, tq, tk):
    b, s, _ = proj.shape
    assert stats.shape[1] == s // tk
    n_pairs = n_heads // 2
    pw = 2 * LANES
    k_norm = lax.cummax(jnp.sqrt(stats[:, :, 0, :n_heads]), axis=1).reshape(-1)
    f_end = stats[:, :, 1, :n_heads].reshape(-1)
    return pl.pallas_call(
        functools.partial(_fox_kernel, tk, n_heads),
        out_shape=jax.ShapeDtypeStruct((b, s, n_heads * HEAD_DIM), BF16),
        grid_spec=pltpu.PrefetchScalarGridSpec(
            num_scalar_prefetch=2, grid=(b, n_pairs, s // tq),
            in_specs=[
                pl.BlockSpec((None, tq, pw), lambda bi, p, i, kn, fe: (bi, i, q_off // pw + p)),
                pl.BlockSpec((None, s, pw), lambda bi, p, i, kn, fe: (bi, 0, k_off // pw + p)),
                pl.BlockSpec((None, s, LANES),
                             lambda bi, p, i, kn, fe: (bi, 0, v_off // LANES + p)),
            ],
            out_specs=pl.BlockSpec((None, tq, LANES), lambda bi, p, i, kn, fe: (bi, i, p))),
        compiler_params=pltpu.CompilerParams(
            dimension_semantics=("arbitrary", "arbitrary", "arbitrary"),
            vmem_limit_bytes=VMEM_LIMIT),
        name="fox_attn",
    )(k_norm, f_end, proj, proj, proj)


def _sb_kernel(tk, tr, q_ref, k_ref, v_ref, o_ref):
    i = pl.program_id(2)
    tq = q_ref.shape[0]
    n_chunks = tq // tr
    n_diag = tr // tk
    row = lax.broadcasted_iota(jnp.int32, (tr, tk), 0)
    col = lax.broadcasted_iota(jnp.int32, (tr, tk), 1)
    lane = lax.broadcasted_iota(jnp.int32, (tq, LANES), 1)
    ur = lax.broadcasted_iota(jnp.int32, (2 * tk, tk), 0)
    uc = lax.broadcasted_iota(jnp.int32, (2 * tk, tk), 1)
    upper2 = jnp.where((ur & (tk - 1)) > uc, 1.0, 0.0).astype(BF16)
    chains = [(hh, r) for r in range(n_chunks) for hh in range(2)]

    def chain_step(hh, r, j, state, dj=None):
        masked = dj is not None
        c, acc = state
        q = q_ref[r * tr:(r + 1) * tr, hh * LANES:(hh + 1) * LANES]
        k0 = pl.multiple_of(j * tk, tk)
        k = k_ref[pl.ds(k0, tk), hh * LANES:(hh + 1) * LANES]
        z = _dot_nt(q, k)
        t = jnp.exp2(jnp.abs(z) * (-LOG2E))
        ls = jnp.minimum(z, 0.0) - jnp.log(1.0 + t)
        lom = ls - z
        if masked:
            strict = col + dj * tk < row
            lom = jnp.where(strict, lom, 0.0)
        hi = pltpu.bitcast(pltpu.bitcast(lom, jnp.uint32) & jnp.uint32(0xFFFF0000), F32)
        parts = jnp.concatenate([hi.astype(BF16), (lom - hi).astype(BF16)], axis=1)
        within = _dot(parts, upper2)
        a = jnp.exp2((ls + within + c) * LOG2E)
        if masked:
            a = jnp.where(strict, a, 0.0)
        acc = acc + _dot(a.astype(BF16), v_ref[pl.ds(k0, tk), :])
        c = c + within[:, 0:1] + lom[:, 0:1]
        return c, acc

    base = i * (tq // tk)
    states = []
    for hh, r in chains:
        st = (jnp.zeros((tr, 1), F32), jnp.zeros((tr, LANES), F32))
        for dj in reversed(range(n_diag)):
            st = chain_step(hh, r, base + r * n_diag + dj, st, dj)
        for j in reversed(range(r * n_diag)):
            st = chain_step(hh, r, base + j, st)
        states.append(st)

    def c_max(sts):
        return functools.reduce(jnp.maximum, [jnp.max(st[0]) for st in sts])

    def cond(carry):
        t, cm, _ = carry
        return (t < base) & (cm >= SB_ZERO_BELOW)

    def body(carry):
        t, _, sts = carry
        j = base - 1 - t
        sts = tuple(chain_step(hh, r, j, st) for (hh, r), st in zip(chains, sts))
        return t + 1, c_max(sts), sts

    states = tuple(states)
    _, _, states = lax.while_loop(cond, body, (jnp.int32(0), c_max(states), states))
    acc = {ch: st[1] for ch, st in zip(chains, states)}
    o0 = jnp.concatenate([acc[(0, r)] for r in range(n_chunks)], axis=0)
    o1 = jnp.concatenate([acc[(1, r)] for r in range(n_chunks)], axis=0)
    o_ref[...] = jnp.where(lane < HEAD_DIM, o0, o1).astype(o_ref.dtype)


def _sb_call(proj, n_heads, q_off, k_off, v_off, tq, tk, tr):
    b, s, _ = proj.shape
    n_pairs = n_heads // 2
    pw = 2 * LANES
    return pl.pallas_call(
        functools.partial(_sb_kernel, tk, tr),
        out_shape=jax.ShapeDtypeStruct((b, s, n_heads * HEAD_DIM), BF16),
        grid=(b, n_pairs, s // tq),
        in_specs=[
            pl.BlockSpec((None, tq, pw), lambda bi, p, i: (bi, i, q_off // pw + p)),
            pl.BlockSpec((None, s, pw), lambda bi, p, i: (bi, 0, k_off // pw + p)),
            pl.BlockSpec((None, s, LANES), lambda bi, p, i: (bi, 0, v_off // LANES + p)),
        ],
        out_specs=pl.BlockSpec((None, tq, LANES), lambda bi, p, i: (bi, i, p)),
        compiler_params=pltpu.CompilerParams(
            dimension_semantics=("arbitrary", "arbitrary", "arbitrary"),
            vmem_limit_bytes=VMEM_LIMIT),
        name="sb_attn",
    )(proj, proj, proj)


def _route(logits):
    tm = logits.shape[0]
    lane = lax.broadcasted_iota(jnp.int32, (tm, LANES), 1)
    big = jnp.int32(LANES)
    g_mask = lane < N_GROUPS
    lg = jnp.where(g_mask, logits, -jnp.inf)
    g_max = jnp.max(lg, axis=1, keepdims=True)
    g_exp = jnp.exp(lg - g_max)
    g_prob = g_exp / jnp.sum(g_exp, axis=1, keepdims=True)
    g_top = jnp.max(g_prob, axis=1, keepdims=True)
    g_idx = jnp.min(jnp.where(g_mask & (g_prob == g_top), lane, big), axis=1, keepdims=True)
    lo = N_GROUPS + EXPERTS_PER_GROUP * g_idx
    e_mask = (lane >= lo) & (lane < lo + EXPERTS_PER_GROUP)
    le = jnp.where(e_mask, logits, -jnp.inf)
    e_max = jnp.max(le, axis=1, keepdims=True)
    e_exp = jnp.exp(le - e_max)
    e_prob = e_exp / jnp.sum(e_exp, axis=1, keepdims=True)
    p1 = jnp.max(e_prob, axis=1, keepdims=True)
    i1 = jnp.min(jnp.where(e_mask & (e_prob == p1), lane, big), axis=1, keepdims=True)
    rest = e_mask & (lane != i1)
    p2 = jnp.max(jnp.where(rest, e_prob, -1.0), axis=1, keepdims=True)
    i2 = jnp.min(jnp.where(rest & (e_prob == p2), lane, big), axis=1, keepdims=True)
    tot = p1 + p2
    w1 = p1 / tot * g_top
    w2 = p2 / tot * g_top
    first_lower = i1 < i2
    e_lo = jnp.minimum(i1, i2) - lo
    e_hi = jnp.maximum(i1, i2) - lo
    pair = jnp.where(e_lo == 0, 0, jnp.where(e_lo == 1, 3, 5)) + (e_hi - e_lo - 1)
    cls = g_idx * N_PAIRS + pair
    return jnp.where(first_lower, w1, w2), jnp.where(first_lower, w2, w1), cls


def _post_kernel(of_ref, os_ref, gates_ref, x_ref, mod_ref, wbf_ref, wbs_ref, wo_ref, g2_ref,
                 wr_hi_ref, wr_lo_ref, rb_ref, x1_ref, slab_ref, meta_ref, cnt_ref, carry_ref):
    first = (pl.program_id(0) == 0) & (pl.program_id(1) == 0)

    @pl.when(first)
    def _():
        carry_ref[...] = jnp.zeros_like(carry_ref)

    tm = x_ref.shape[0]
    d = x_ref.shape[1]
    bf = _dot(of_ref[...], wbf_ref[...])
    bs = _dot(os_ref[...], wbs_ref[...])
    merged = jax.nn.sigmoid(gates_ref[:, :d]) * bf + jax.nn.sigmoid(gates_ref[:, d:]) * bs
    y = _dot(merged.astype(BF16), wo_ref[...])
    x1 = x_ref[...] + mod_ref[2:3, :] * y
    x1_ref[...] = x1
    h2 = _rms_scale(x1) * g2_ref[...] * (1.0 + mod_ref[4:5, :]) + mod_ref[3:4, :]
    h_hi, h_lo = _split2(h2)
    w_hi = wr_hi_ref[...]
    logits = _dot(h_hi, w_hi) + _dot(h_hi, wr_lo_ref[...]) + _dot(h_lo, w_hi) + rb_ref[...]
    w_first, w_second, cls = _route(logits)

    lane = lax.broadcasted_iota(jnp.int32, (tm, LANES), 1)
    onehot = lane == cls
    oh = jnp.where(onehot, 1.0, 0.0)
    row = lax.broadcasted_iota(jnp.int32, (tm, tm), 0)
    col = lax.broadcasted_iota(jnp.int32, (tm, tm), 1)
    earlier = jnp.where(col < row, 1.0, 0.0).astype(BF16)
    before = _dot(earlier, oh.astype(BF16)) + carry_ref[0:1, :]
    rank = jnp.sum(jnp.where(onehot, before, 0.0), axis=1, keepdims=True)
    counts = carry_ref[0:1, :] + jnp.sum(oh, axis=0, keepdims=True)
    carry_ref[...] = jnp.broadcast_to(counts, carry_ref.shape)
    cnt_ref[...] = jnp.broadcast_to(counts, cnt_ref.shape)

    meta = (jnp.where(lane == 0, w_first, 0.0) + jnp.where(lane == 1, w_second, 0.0)
            + jnp.where(lane == 2, cls.astype(F32), 0.0) + jnp.where(lane == 3, rank, 0.0))
    meta_ref[...] = meta
    h_rows = d // LANES
    for s in range(h_rows):
        slab_ref[:, s, :] = h2[:, s * LANES:(s + 1) * LANES]
    slab_ref[:, h_rows, :] = meta
    for s in range(h_rows + 1, SLAB_ROWS):
        slab_ref[:, s, :] = jnp.zeros((tm, LANES), F32)


def _post_call(o_fox, o_sb, gates, x, mod3, w_bf, w_bs, w_o, norm_g, wr_hi, wr_lo, rb, tm):
    b, s, d = x.shape
    wdt = o_fox.shape[2]
    const = lambda shape: pl.BlockSpec(shape, lambda bi, i: (0,) * len(shape),
                                       pipeline_mode=pl.Buffered(1))
    tile = lambda w: pl.BlockSpec((None, tm, w), lambda bi, i: (bi, i, 0))
    return pl.pallas_call(
        _post_kernel,
        out_shape=(jax.ShapeDtypeStruct((b, s, d), F32),
                   jax.ShapeDtypeStruct((b, s, SLAB_ROWS, LANES), F32),
                   jax.ShapeDtypeStruct((b, s, LANES), F32),
                   jax.ShapeDtypeStruct((8, LANES), F32)),
        grid=(b, s // tm),
        in_specs=[
            tile(wdt), tile(wdt), tile(2 * d), tile(d),
            pl.BlockSpec((None, N_MOD, d), lambda bi, i: (bi, 0, 0)),
            const((wdt, d)), const((wdt, d)), const((d, d)), const((1, d)),
            const((d, LANES)), const((d, LANES)), const((1, LANES)),
        ],
        out_specs=(tile(d),
                   pl.BlockSpec((None, tm, SLAB_ROWS, LANES), lambda bi, i: (bi, i, 0, 0)),
                   tile(LANES),
                   pl.BlockSpec((8, LANES), lambda bi, i: (0, 0))),
        scratch_shapes=[pltpu.VMEM((8, LANES), F32)],
        compiler_params=pltpu.CompilerParams(
            dimension_semantics=("arbitrary", "arbitrary"), vmem_limit_bytes=VMEM_LIMIT),
        name="post",
    )(o_fox, o_sb, gates, x, mod3, w_bf, w_bs, w_o, norm_g, wr_hi, wr_lo, rb)


def _dispatch_kernel(pos_ref, src_ref, init_ref, dst_ref, sem):
    del init_ref
    rows = src_ref.shape[0]
    base = pl.program_id(0) * rows

    def issue(r, carry):
        pltpu.make_async_copy(src_ref.at[r], dst_ref.at[pos_ref[base + r]], sem).start()
        return carry

    lax.fori_loop(0, rows, issue, 0, unroll=8)
    pltpu.make_async_copy(src_ref, dst_ref.at[pl.ds(0, rows)], sem).wait()


def _dispatch_call(slabs, pos, n_pad, rows):
    n = slabs.shape[0]
    any_spec = pl.BlockSpec(memory_space=pl.ANY)
    return pl.pallas_call(
        _dispatch_kernel,
        out_shape=jax.ShapeDtypeStruct((n_pad,) + slabs.shape[1:], slabs.dtype),
        grid_spec=pltpu.PrefetchScalarGridSpec(
            num_scalar_prefetch=1, grid=(n // rows,),
            in_specs=[pl.BlockSpec((rows,) + slabs.shape[1:], lambda t, pos: (t, 0, 0)), any_spec],
            out_specs=any_spec,
            scratch_shapes=[pltpu.SemaphoreType.DMA(())]),
        input_output_aliases={2: 0},
        compiler_params=pltpu.CompilerParams(
            dimension_semantics=("arbitrary",), has_side_effects=True,
            vmem_limit_bytes=VMEM_LIMIT),
        name="dispatch",
    )(pos, slabs, jnp.zeros((n_pad,) + slabs.shape[1:], slabs.dtype))


def _moe_kernel(elo_ref, ehi_ref, tix_ref, nv_ref, slab_ref, wg0_ref, wu0_ref, wd0_ref,
                wg1_ref, wu1_ref, wd1_ref, y_ref):
    y_rows = y_ref.shape[1]
    valid = pl.program_id(0) < nv_ref[0]

    @pl.when(jnp.logical_not(valid))
    def _():
        y_ref[...] = jnp.zeros_like(y_ref)

    @pl.when(valid)
    def _():
        h = jnp.concatenate([slab_ref[:, s, :] for s in range(y_rows)], axis=1).astype(BF16)
        meta = slab_ref[:, y_rows, :]

        def expert(wg_ref, wu_ref, wd_ref):
            g = _dot(h, wg_ref[...])
            u = _dot(h, wu_ref[...])
            act = (g * jax.nn.sigmoid(g) * u).astype(BF16)
            return _dot(act, wd_ref[...])

        y = (meta[:, 0:1] * expert(wg0_ref, wu0_ref, wd0_ref)
             + meta[:, 1:2] * expert(wg1_ref, wu1_ref, wd1_ref))
        for s in range(y_rows):
            y_ref[:, s, :] = y[:, s * LANES:(s + 1) * LANES]


def _moe_call(slabs, tile_elo, tile_ehi, tile_ix, n_valid, w_gate, w_up, w_down, tm):
    n_pad = slabs.shape[0]
    _, d, f = w_gate.shape
    lo = lambda t, elo, ehi, tix, nv: (elo[t], 0, 0)
    hi = lambda t, elo, ehi, tix, nv: (ehi[t], 0, 0)
    return pl.pallas_call(
        _moe_kernel,
        out_shape=jax.ShapeDtypeStruct((n_pad, d // LANES, LANES), F32),
        grid_spec=pltpu.PrefetchScalarGridSpec(
            num_scalar_prefetch=4, grid=(n_pad // tm,),
            in_specs=[
                pl.BlockSpec((tm,) + slabs.shape[1:], lambda t, elo, ehi, tix, nv: (tix[t], 0, 0)),
                pl.BlockSpec((None, d, f), lo), pl.BlockSpec((None, d, f), lo),
                pl.BlockSpec((None, f, d), lo),
                pl.BlockSpec((None, d, f), hi), pl.BlockSpec((None, d, f), hi),
                pl.BlockSpec((None, f, d), hi),
            ],
            out_specs=pl.BlockSpec((tm, d // LANES, LANES),
                                   lambda t, elo, ehi, tix, nv: (t, 0, 0))),
        compiler_params=pltpu.CompilerParams(
            dimension_semantics=("arbitrary",), vmem_limit_bytes=VMEM_LIMIT),
        name="moe",
    )(tile_elo, tile_ehi, tile_ix, n_valid, slabs, w_gate, w_up, w_down, w_gate, w_up, w_down)


def _combine_kernel(pos_ref, ys_ref, x1_ref, mod_ref, fg_ref, o_ref, ybuf, sem):
    rows = x1_ref.shape[0]
    y_rows = ybuf.shape[2]
    t = pl.program_id(0)

    def issue(step, slot):
        base = step * rows

        def body(r, carry):
            pltpu.make_async_copy(ys_ref.at[pos_ref[base + r]], ybuf.at[slot, r],
                                  sem.at[slot]).start()
            return carry

        lax.fori_loop(0, rows, body, 0, unroll=8)

    @pl.when(t == 0)
    def _():
        issue(0, 0)

    @pl.when(t + 1 < pl.num_programs(0))
    def _():
        issue(t + 1, (t + 1) % 2)

    slot = t % 2
    pltpu.make_async_copy(ys_ref.at[pl.ds(0, rows)], ybuf.at[slot], sem.at[slot]).wait()
    y = jnp.concatenate([ybuf[slot, :, s, :] for s in range(y_rows)], axis=1)
    x2 = x1_ref[...] + mod_ref[5:6, :] * y
    o_ref[...] = _rms_scale(x2) * fg_ref[...]


def _combine_call(y_sorted, pos, x1, mod3, final_g, seq, rows):
    n, d = x1.shape
    y_rows = y_sorted.shape[1]
    per_seq = seq // rows
    return pl.pallas_call(
        _combine_kernel,
        out_shape=jax.ShapeDtypeStruct((n, d), F32),
        grid_spec=pltpu.PrefetchScalarGridSpec(
            num_scalar_prefetch=1, grid=(n // rows,),
            in_specs=[
                pl.BlockSpec(memory_space=pl.ANY),
                pl.BlockSpec((rows, d), lambda t, pos: (t, 0)),
                pl.BlockSpec((None, N_MOD, d), lambda t, pos: (t // per_seq, 0, 0)),
                pl.BlockSpec((1, d), lambda t, pos: (0, 0)),
            ],
            out_specs=pl.BlockSpec((rows, d), lambda t, pos: (t, 0)),
            scratch_shapes=[pltpu.VMEM((2, rows, y_rows, LANES), F32),
                            pltpu.SemaphoreType.DMA((2,))]),
        compiler_params=pltpu.CompilerParams(
            dimension_semantics=("arbitrary",), vmem_limit_bytes=VMEM_LIMIT),
        name="combine",
    )(pos, y_sorted, x1, mod3, final_g)


def _pad_lanes(a):
    return jnp.pad(a.astype(F32), ((0, 0), (0, LANES - a.shape[1])))


def _pad_heads(w, n_heads, scale=1.0):
    d = w.shape[0]
    w = (w * scale).reshape(d, n_heads, HEAD_DIM)
    w = jnp.concatenate([w, jnp.zeros_like(w)], axis=2)
    return w.reshape(d, n_heads * LANES)


def _bias_selectors(n_heads):
    rows = 3 * LANES
    sel_q = np.zeros((rows, n_heads * LANES), np.float32)
    sel_k = np.zeros((rows, n_heads * LANES), np.float32)
    ones_row = n_heads
    for h in range(n_heads):
        base = h * LANES + HEAD_DIM
        for j in range(3):
            sel_q[j * LANES + h, base + j] = 1.0
            sel_q[ones_row, base + 3 + j] = 1.0
            sel_k[ones_row, base + j] = 1.0
            sel_k[j * LANES + h, base + 3 + j] = -1.0
    return sel_q, sel_k


def kernel(x, c, ada_w, ada_b, norm1_g, w_in, b_forget, w_branch_fox, w_branch_sb, w_out,
           norm2_g, router_group_w, router_group_b, router_expert_w, router_expert_b,
           expert_w_gate, expert_w_up, expert_w_down, final_g):
    b, s, d = x.shape
    depth = ada_w.shape[0]
    hf = b_forget.shape[1]
    fw = hf * HEAD_DIM
    sw = w_branch_sb.shape[1]
    hs = sw // HEAD_DIM
    scale = HEAD_DIM ** -0.5
    n_exp = expert_w_gate.shape[1]

    assert depth == 1, "the final RMSNorm is fused into the single layer's MoE kernel"
    for l in range(depth):
        mod = _mod_call(c, ada_w[l], ada_b[l])
        mod3 = mod.reshape(b, N_MOD, d)

        wl = w_in[l]
        o = 0
        wq_a = wl[:, o:o + fw]; o += fw
        wk_a = wl[:, o:o + fw]; o += fw
        wv_a = wl[:, o:o + fw]; o += fw
        wf = wl[:, o:o + hf]; o += hf
        wq_b = wl[:, o:o + sw]; o += sw
        wk_b = wl[:, o:o + sw]; o += sw
        wv_b = wl[:, o:o + sw]; o += sw
        w_gates = wl[:, o:].astype(BF16)
        w_all = jnp.concatenate([
            _pad_heads(wq_a, hf, scale), _pad_heads(wk_a, hf),
            _pad_heads(wq_b, hs, scale), _pad_heads(wk_b, hs), wv_a, wv_b], axis=1).astype(BF16)
        sel_q, sel_k = _bias_selectors(hf)
        sel = jnp.concatenate([sel_q, sel_k], axis=1).astype(BF16)
        w_f = _pad_lanes(wf).astype(BF16)
        b_f = _pad_lanes(b_forget[l].reshape(1, hf))

        q_a_off = 0
        k_a_off = hf * LANES
        q_b_off = 2 * hf * LANES
        k_b_off = q_b_off + hs * LANES
        v_a_off = k_b_off + hs * LANES
        v_b_off = v_a_off + fw

        tq = min(s, 512)
        proj, gates, kstats = _in_call(x, mod3, norm1_g[l].reshape(1, d), w_all, w_gates, w_f, b_f,
                                       sel, hf, tm=tq)
        o_fox = _fox_call(proj, kstats, hf, q_a_off, k_a_off, v_a_off, tq, tq)
        o_sb = _sb_call(proj, hs, q_b_off, k_b_off, v_b_off, min(s, 512), min(s, 256), min(s, 512))

        wr = _pad_lanes(jnp.concatenate([router_group_w[l], router_expert_w[l]], axis=1))
        wr_hi = wr.astype(BF16)
        wr_lo = (wr - wr_hi.astype(F32)).astype(BF16)
        rb = _pad_lanes(jnp.concatenate([router_group_b[l], router_expert_b[l]]).reshape(1, -1))

        x1, slabs, meta, counts = _post_call(
            o_fox, o_sb, gates, x, mod3, w_branch_fox[l].astype(BF16), w_branch_sb[l].astype(BF16),
            w_out[l].astype(BF16), norm2_g[l].reshape(1, d), wr_hi, wr_lo, rb, tm=min(s, 512))

        n = b * s
        tm_e = MOE_TILE
        n_cls = N_GROUPS * N_PAIRS
        n_pad = n + n_cls * tm_e
        cls = meta[..., 2].reshape(n).astype(jnp.int32)
        rank = meta[..., 3].reshape(n).astype(jnp.int32)
        cnt = counts[0, :n_cls].astype(jnp.int32)
        padded = (cnt + tm_e - 1) // tm_e * tm_e
        ends = jnp.cumsum(padded)
        starts = ends - padded
        cls_ids = jnp.arange(n_cls, dtype=jnp.int32)
        pos = jnp.sum(jnp.where(cls[:, None] == cls_ids[None, :], starts[None, :], 0), axis=1) + rank
        n_valid = ends[-1] // tm_e
        tile_ix = jnp.minimum(jnp.arange(n_pad // tm_e, dtype=jnp.int32), n_valid - 1)
        tile_cls = jnp.sum((tile_ix * tm_e)[:, None] >= ends[None, :], axis=1)
        pair_lo = jnp.asarray([0, 0, 0, 1, 1, 2], jnp.int32)
        pair_hi = jnp.asarray([1, 2, 3, 2, 3, 3], jnp.int32)
        tile_grp = tile_cls // N_PAIRS * EXPERTS_PER_GROUP
        tile_elo = tile_grp + pair_lo[tile_cls % N_PAIRS]
        tile_ehi = tile_grp + pair_hi[tile_cls % N_PAIRS]

        rows = min(s, PERMUTE_ROWS)
        sorted_slabs = _dispatch_call(slabs.reshape(n, SLAB_ROWS, LANES), pos, n_pad, rows)
        y_sorted = _moe_call(sorted_slabs, tile_elo, tile_ehi, tile_ix, n_valid.reshape(1),
                             expert_w_gate[l].astype(BF16), expert_w_up[l].astype(BF16),
                             expert_w_down[l].astype(BF16), tm_e)
        out = _combine_call(y_sorted, pos, x1.reshape(n, d), mod3, final_g.reshape(1, d), s, rows)
        x = out.reshape(b, s, d)
    return x
```

```python
import functools

import jax
import jax.numpy as jnp
import numpy as np
from jax import lax
from jax.experimental import pallas as pl
from jax.experimental.pallas import tpu as pltpu

F32 = jnp.float32
BF16 = jnp.bfloat16

HEAD_DIM = 64
LANES = 128
RMS_EPS = 1e-6
N_MOD = 6
N_GROUPS = 4
EXPERTS_PER_GROUP = 4
N_PAIRS = 6
N_SPLIT = 3
MOE_TILE = 256
PERMUTE_ROWS = 1024
NEG_BIG = -1e30
LOG2E = 1.4426950408889634
SB_ZERO_BELOW = -104.0
FOX_ZERO_BELOW = -110.0
NORM_SLACK = 1.02
VMEM_LIMIT = 56 * 1024 * 1024


def _dot(a, b):
    return jnp.dot(a, b, preferred_element_type=F32)


def _dot_nt(a, b):
    return lax.dot_general(a, b, (((1,), (1,)), ((), ())), preferred_element_type=F32)


def _split2(x):
    hi = x.astype(BF16)
    lo = (x - hi.astype(F32)).astype(BF16)
    return hi, lo


def _split3(x):
    hi = x.astype(BF16)
    r = x - hi.astype(F32)
    mid = r.astype(BF16)
    lo = (r - mid.astype(F32)).astype(BF16)
    return hi, mid, lo


def _log_sigmoid(z):
    return jnp.minimum(z, 0.0) - jnp.log1p(jnp.exp(-jnp.abs(z)))


def _rms_scale(x):
    return x * lax.rsqrt(jnp.mean(x * x, axis=-1, keepdims=True) + RMS_EPS)


def _mod_kernel(c_ref, w_ref, b_ref, o_ref):
    c = c_ref[...]
    ca = c * jax.nn.sigmoid(c)
    c_hi, c_mid, c_lo = _split3(ca)
    w_hi, w_mid, w_lo = _split3(w_ref[...])
    acc = _dot(c_hi, w_hi) + _dot(c_hi, w_mid) + _dot(c_mid, w_hi)
    acc += _dot(c_mid, w_mid) + _dot(c_hi, w_lo) + _dot(c_lo, w_hi)
    o_ref[...] = acc + b_ref[...]


def _mod_call(c, ada_w, ada_b):
    b, d = c.shape
    n = ada_w.shape[1]
    bp = 16
    tn = n // N_MOD
    c_pad = jnp.pad(c, ((0, bp - b), (0, 0)))
    out = pl.pallas_call(
        _mod_kernel,
        out_shape=jax.ShapeDtypeStruct((bp, n), F32),
        grid=(n // tn,),
        in_specs=[
            pl.BlockSpec((bp, d), lambda j: (0, 0)),
            pl.BlockSpec((d, tn), lambda j: (0, j)),
            pl.BlockSpec((1, tn), lambda j: (0, j)),
        ],
        out_specs=pl.BlockSpec((bp, tn), lambda j: (0, j)),
        compiler_params=pltpu.CompilerParams(
            dimension_semantics=("arbitrary",), vmem_limit_bytes=VMEM_LIMIT),
        name="mod",
    )(c_pad, ada_w, ada_b.reshape(1, n))
    return out[:b]


def _in_kernel(n_heads, chunk, x_ref, mod_ref, g_ref, w_ref, wg_ref, wf_ref, bf_ref,
               sel_ref, proj_ref, gates_ref, stats_ref, carry_ref):
    i = pl.program_id(1)
    tm = x_ref.shape[0]

    @pl.when(i == 0)
    def _():
        carry_ref[...] = jnp.zeros_like(carry_ref)

    x = x_ref[...]
    shift = mod_ref[0:1, :]
    scale = mod_ref[1:2, :]
    h = _rms_scale(x) * g_ref[...] * (1.0 + scale) + shift
    hb = h.astype(BF16)

    lane = lax.broadcasted_iota(jnp.int32, (tm, LANES), 1)
    lf = _log_sigmoid(_dot(hb, wf_ref[...]) + bf_ref[...])
    lf = jnp.where(lane < n_heads, lf, 0.0)
    row = lax.broadcasted_iota(jnp.int32, (tm, tm), 0)
    col = lax.broadcasted_iota(jnp.int32, (tm, tm), 1)
    tri = jnp.where(col <= row, 1.0, 0.0).astype(BF16)
    l_hi, l_mid, l_lo = _split3(lf)
    fcum = _dot(tri, l_hi) + _dot(tri, l_mid) + _dot(tri, l_lo) + carry_ref[0:1, :]
    carry_ref[...] = jnp.broadcast_to(fcum[tm - 1:tm, :], carry_ref.shape)
    fx = jnp.where(lane == n_heads, 1.0, fcum)
    f_hi, f_mid, f_lo = _split3(fx)
    hml = jnp.concatenate([f_hi, f_mid, f_lo], axis=1)

    srow = lax.broadcasted_iota(jnp.int32, (8, LANES), 0)
    slane = lax.broadcasted_iota(jnp.int32, (8, LANES), 1)
    stats = jnp.where(srow == 1, jnp.broadcast_to(fcum[tm - 1:tm, :], (8, LANES)), 0.0)
    k_off = n_heads * HEAD_DIM

    n_w = w_ref.shape[1]
    for c0 in range(0, n_w, chunk):
        acc = _dot(hb, w_ref[:, c0:c0 + chunk])
        proj_ref[:, c0:c0 + chunk] = acc.astype(BF16)
        if k_off <= c0 < 2 * k_off:
            for bl in range(chunk // LANES):
                kb = acc[:, bl * LANES:(bl + 1) * LANES]
                kb2 = kb * kb
                for half in range(2):
                    in_head = (lane >= half * HEAD_DIM) & (lane < (half + 1) * HEAD_DIM)
                    sq = jnp.sum(jnp.where(in_head, kb2, 0.0), axis=1, keepdims=True)
                    head = (c0 - k_off) // HEAD_DIM + 2 * bl + half
                    top = jnp.max(sq, axis=0, keepdims=True)
                    stats = jnp.where((srow == 0) & (slane == head), top, stats)
    stats_ref[...] = stats
    for c0 in range(0, sel_ref.shape[1], chunk):
        proj_ref[:, n_w + c0:n_w + c0 + chunk] = _dot(hml, sel_ref[:, c0:c0 + chunk]).astype(BF16)
    n_gate = wg_ref.shape[1]
    for c0 in range(0, n_gate, chunk):
        gates_ref[:, c0:c0 + chunk] = _dot(hb, wg_ref[:, c0:c0 + chunk])


def _in_call(x, mod3, norm_g, w_all, w_gates, w_f, b_f, sel, n_heads, tm):
    b, s, d = x.shape
    n_w = w_all.shape[1]
    n_ext = sel.shape[1]
    n_proj = n_w + n_ext
    n_gate = w_gates.shape[1]
    chunk = 512
    assert n_heads * HEAD_DIM % chunk == 0 and n_w % chunk == 0 and n_ext % chunk == 0
    const = lambda shape: pl.BlockSpec(shape, lambda bi, i: (0,) * len(shape),
                                       pipeline_mode=pl.Buffered(1))
    return pl.pallas_call(
        functools.partial(_in_kernel, n_heads, chunk),
        out_shape=(jax.ShapeDtypeStruct((b, s, n_proj), BF16),
                   jax.ShapeDtypeStruct((b, s, n_gate), F32),
                   jax.ShapeDtypeStruct((b, s // tm, 8, LANES), F32)),
        grid=(b, s // tm),
        in_specs=[
            pl.BlockSpec((None, tm, d), lambda bi, i: (bi, i, 0)),
            pl.BlockSpec((None, N_MOD, d), lambda bi, i: (bi, 0, 0)),
            const((1, d)),
            const((d, n_w)),
            const((d, n_gate)),
            const((d, LANES)),
            const((1, LANES)),
            const((3 * LANES, n_ext)),
        ],
        out_specs=(pl.BlockSpec((None, tm, n_proj), lambda bi, i: (bi, i, 0)),
                   pl.BlockSpec((None, tm, n_gate), lambda bi, i: (bi, i, 0)),
                   pl.BlockSpec((None, None, 8, LANES), lambda bi, i: (bi, i, 0, 0))),
        scratch_shapes=[pltpu.VMEM((8, LANES), F32)],
        compiler_params=pltpu.CompilerParams(
            dimension_semantics=("arbitrary", "arbitrary"), vmem_limit_bytes=VMEM_LIMIT),
        name="in_proj",
    )(x, mod3, norm_g, w_all, w_gates, w_f, b_f, sel)


def _head_lanes(lane, hh):
    return (lane >= hh * HEAD_DIM) & (lane < (hh + 1) * HEAD_DIM)


def _bias_lanes(lane, hh, lo=0, hi=2 * N_SPLIT):
    return (lane >= hh * 2 * N_SPLIT + lo) & (lane < hh * 2 * N_SPLIT + hi)


def _fox_kernel(tk, n_heads, kn_ref, fe_ref, q_ref, qe_ref, k_ref, ke_ref, v_ref, o_ref):
    bi = pl.program_id(0)
    pair = pl.program_id(1)
    i = pl.program_id(2)
    tq = q_ref.shape[0]
    n_diag = tq // tk
    n_kt = k_ref.shape[0] // tk
    row = lax.broadcasted_iota(jnp.int32, (tq, tk), 0)
    col = lax.broadcasted_iota(jnp.int32, (tq, tk), 1)
    lane = lax.broadcasted_iota(jnp.int32, (tq, LANES), 1)
    klane = lax.broadcasted_iota(jnp.int32, (tk, LANES), 1)
    q_all = jnp.concatenate([q_ref[...], qe_ref[...]], axis=1)

    qf = q_ref[...].astype(F32)
    qef = qe_ref[...].astype(F32)
    q_norm, q_f = [], []
    for hh in range(2):
        q_norm.append(jnp.sqrt(jnp.sum(jnp.where(_head_lanes(lane, hh), qf * qf, 0.0),
                                       axis=1, keepdims=True)))
        q_f.append(jnp.sum(jnp.where(_bias_lanes(lane, hh, 0, N_SPLIT), qef, 0.0),
                           axis=1, keepdims=True))

    def may_contribute(j, states):
        worst = []
        for hh, (m, _, _) in enumerate(states):
            idx = (bi * n_kt + j) * n_heads + 2 * pair + hh
            bound = q_norm[hh] * (kn_ref[idx] * NORM_SLACK) + (q_f[hh] - fe_ref[idx]) - m
            worst.append(jnp.max(bound))
        return jnp.maximum(worst[0], worst[1]) >= FOX_ZERO_BELOW

    def head_step(hh, j, state, masked):
        m, l, acc = state
        k0 = pl.multiple_of(j * tk, tk)
        k = jnp.where(_head_lanes(klane, hh), k_ref[pl.ds(k0, tk), :], 0)
        ke = jnp.where(_bias_lanes(klane, hh), ke_ref[pl.ds(k0, tk), :], 0)
        s = _dot_nt(q_all, jnp.concatenate([k, ke], axis=1))
        if masked:
            s = jnp.where(col + (j - i * n_diag) * tk <= row, s, NEG_BIG)
        m_new = jnp.maximum(m, jnp.max(s, axis=1, keepdims=True))
        alpha = jnp.exp(m - m_new)
        p = jnp.exp(s - m_new)
        l = alpha * l + jnp.sum(p, axis=1, keepdims=True)
        acc = alpha * acc + _dot(p.astype(BF16), v_ref[pl.ds(k0, tk), :])
        return m_new, l, acc

    def pair_step(j, states, masked):
        return tuple(head_step(hh, j, st, masked) for hh, st in enumerate(states))

    init = (jnp.full((tq, 1), NEG_BIG, F32), jnp.zeros((tq, 1), F32),
            jnp.zeros((tq, LANES), F32))
    states = (init, init)
    for dj in range(n_diag):
        states = pair_step(i * n_diag + dj, states, True)

    n_full = i * n_diag

    def cond(carry):
        t, go, _ = carry
        return (t < n_full) & go

    def body(carry):
        t, _, sts = carry
        sts = pair_step(n_full - 1 - t, sts, False)
        return t + 1, may_contribute(jnp.maximum(n_full - 2 - t, 0), sts), sts

    go0 = may_contribute(jnp.maximum(n_full - 1, 0), states)
    _, _, states = lax.while_loop(cond, body, (jnp.int32(0), go0, states))
    outs = [acc / l for _, l, acc in states]
    o_ref[...] = jnp.where(lane < HEAD_DIM, outs[0], outs[1]).astype(o_ref.dtype)


def _fox_call(proj, stats, n_heads, q_off, qe_off, k_off, ke_off, v_off, tq, tk):
    b, s, _ = proj.shape
    assert stats.shape[1] == s // tk
    n_pairs = n_heads // 2
    k_norm = lax.cummax(jnp.sqrt(stats[:, :, 0, :n_heads]), axis=1).reshape(-1)
    f_end = stats[:, :, 1, :n_heads].reshape(-1)
    q_tile = lambda off: pl.BlockSpec(
        (None, tq, LANES), lambda bi, p, i, kn, fe: (bi, i, off // LANES + p))
    kv_all = lambda off: pl.BlockSpec(
        (None, s, LANES), lambda bi, p, i, kn, fe: (bi, 0, off // LANES + p))
    return pl.pallas_call(
        functools.partial(_fox_kernel, tk, n_heads),
        out_shape=jax.ShapeDtypeStruct((b, s, n_heads * HEAD_DIM), BF16),
        grid_spec=pltpu.PrefetchScalarGridSpec(
            num_scalar_prefetch=2, grid=(b, n_pairs, s // tq),
            in_specs=[q_tile(q_off), q_tile(qe_off), kv_all(k_off), kv_all(ke_off),
                      kv_all(v_off)],
            out_specs=pl.BlockSpec((None, tq, LANES), lambda bi, p, i, kn, fe: (bi, i, p))),
        compiler_params=pltpu.CompilerParams(
            dimension_semantics=("arbitrary", "arbitrary", "arbitrary"),
            vmem_limit_bytes=VMEM_LIMIT),
        name="fox_attn",
    )(k_norm, f_end, proj, proj, proj, proj, proj)


def _sb_kernel(tk, tr, q_ref, k_ref, v_ref, o_ref):
    i = pl.program_id(2)
    tq = q_ref.shape[0]
    n_chunks = tq // tr
    n_diag = tr // tk
    row = lax.broadcasted_iota(jnp.int32, (tr, tk), 0)
    col = lax.broadcasted_iota(jnp.int32, (tr, tk), 1)
    lane = lax.broadcasted_iota(jnp.int32, (tq, LANES), 1)
    klane = lax.broadcasted_iota(jnp.int32, (tk, LANES), 1)
    ur = lax.broadcasted_iota(jnp.int32, (2 * tk, tk), 0)
    uc = lax.broadcasted_iota(jnp.int32, (2 * tk, tk), 1)
    upper2 = jnp.where((ur & (tk - 1)) > uc, 1.0, 0.0).astype(BF16)
    chains = [(hh, r) for r in range(n_chunks) for hh in range(2)]

    def chain_step(hh, r, j, state, dj=None):
        masked = dj is not None
        c, acc = state
        q = q_ref[r * tr:(r + 1) * tr, :]
        k0 = pl.multiple_of(j * tk, tk)
        k = jnp.where(_head_lanes(klane, hh), k_ref[pl.ds(k0, tk), :], 0)
        z = _dot_nt(q, k)
        t = jnp.exp2(jnp.abs(z) * (-LOG2E))
        ls = jnp.minimum(z, 0.0) - jnp.log(1.0 + t)
        lom = ls - z
        if masked:
            strict = col + dj * tk < row
            lom = jnp.where(strict, lom, 0.0)
        hi = pltpu.bitcast(pltpu.bitcast(lom, jnp.uint32) & jnp.uint32(0xFFFF0000), F32)
        parts = jnp.concatenate([hi.astype(BF16), (lom - hi).astype(BF16)], axis=1)
        within = _dot(parts, upper2)
        a = jnp.exp2((ls + within + c) * LOG2E)
        if masked:
            a = jnp.where(strict, a, 0.0)
        acc = acc + _dot(a.astype(BF16), v_ref[pl.ds(k0, tk), :])
        c = c + within[:, 0:1] + lom[:, 0:1]
        return c, acc

    base = i * (tq // tk)
    states = []
    for hh, r in chains:
        st = (jnp.zeros((tr, 1), F32), jnp.zeros((tr, LANES), F32))
        for dj in reversed(range(n_diag)):
            st = chain_step(hh, r, base + r * n_diag + dj, st, dj)
        for j in reversed(range(r * n_diag)):
            st = chain_step(hh, r, base + j, st)
        states.append(st)

    def c_max(sts):
        return functools.reduce(jnp.maximum, [jnp.max(st[0]) for st in sts])

    def cond(carry):
        t, cm, _ = carry
        return (t < base) & (cm >= SB_ZERO_BELOW)

    def body(carry):
        t, _, sts = carry
        j = base - 1 - t
        sts = tuple(chain_step(hh, r, j, st) for (hh, r), st in zip(chains, sts))
        return t + 1, c_max(sts), sts

    states = tuple(states)
    _, _, states = lax.while_loop(cond, body, (jnp.int32(0), c_max(states), states))
    acc = {ch: st[1] for ch, st in zip(chains, states)}
    o0 = jnp.concatenate([acc[(0, r)] for r in range(n_chunks)], axis=0)
    o1 = jnp.concatenate([acc[(1, r)] for r in range(n_chunks)], axis=0)
    o_ref[...] = jnp.where(lane < HEAD_DIM, o0, o1).astype(o_ref.dtype)


def _sb_call(proj, n_heads, q_off, k_off, v_off, tq, tk, tr):
    b, s, _ = proj.shape
    n_pairs = n_heads // 2
    return pl.pallas_call(
        functools.partial(_sb_kernel, tk, tr),
        out_shape=jax.ShapeDtypeStruct((b, s, n_heads * HEAD_DIM), BF16),
        grid=(b, n_pairs, s // tq),
        in_specs=[
            pl.BlockSpec((None, tq, LANES), lambda bi, p, i: (bi, i, q_off // LANES + p)),
            pl.BlockSpec((None, s, LANES), lambda bi, p, i: (bi, 0, k_off // LANES + p)),
            pl.BlockSpec((None, s, LANES), lambda bi, p, i: (bi, 0, v_off // LANES + p)),
        ],
        out_specs=pl.BlockSpec((None, tq, LANES), lambda bi, p, i: (bi, i, p)),
        compiler_params=pltpu.CompilerParams(
            dimension_semantics=("arbitrary", "arbitrary", "arbitrary"),
            vmem_limit_bytes=VMEM_LIMIT),
        name="sb_attn",
    )(proj, proj, proj)


def _route(logits):
    tm = logits.shape[0]
    lane = lax.broadcasted_iota(jnp.int32, (tm, LANES), 1)
    big = jnp.int32(LANES)
    g_mask = lane < N_GROUPS
    lg = jnp.where(g_mask, logits, -jnp.inf)
    g_max = jnp.max(lg, axis=1, keepdims=True)
    g_exp = jnp.exp(lg - g_max)
    g_prob = g_exp / jnp.sum(g_exp, axis=1, keepdims=True)
    g_top = jnp.max(g_prob, axis=1, keepdims=True)
    g_idx = jnp.min(jnp.where(g_mask & (g_prob == g_top), lane, big), axis=1, keepdims=True)
    lo = N_GROUPS + EXPERTS_PER_GROUP * g_idx
    e_mask = (lane >= lo) & (lane < lo + EXPERTS_PER_GROUP)
    le = jnp.where(e_mask, logits, -jnp.inf)
    e_max = jnp.max(le, axis=1, keepdims=True)
    e_exp = jnp.exp(le - e_max)
    e_prob = e_exp / jnp.sum(e_exp, axis=1, keepdims=True)
    p1 = jnp.max(e_prob, axis=1, keepdims=True)
    i1 = jnp.min(jnp.where(e_mask & (e_prob == p1), lane, big), axis=1, keepdims=True)
    rest = e_mask & (lane != i1)
    p2 = jnp.max(jnp.where(rest, e_prob, -1.0), axis=1, keepdims=True)
    i2 = jnp.min(jnp.where(rest & (e_prob == p2), lane, big), axis=1, keepdims=True)
    tot = p1 + p2
    w1 = p1 / tot * g_top
    w2 = p2 / tot * g_top
    first_lower = i1 < i2
    e_lo = jnp.minimum(i1, i2) - lo
    e_hi = jnp.maximum(i1, i2) - lo
    pair = jnp.where(e_lo == 0, 0, jnp.where(e_lo == 1, 3, 5)) + (e_hi - e_lo - 1)
    cls = g_idx * N_PAIRS + pair
    return jnp.where(first_lower, w1, w2), jnp.where(first_lower, w2, w1), cls


def _post_kernel(of_ref, os_ref, gates_ref, x_ref, mod_ref, wbf_ref, wbs_ref, wo_ref, g2_ref,
                 wr_hi_ref, wr_lo_ref, rb_ref, x1_ref, slab_ref, meta_ref, cnt_ref, carry_ref):
    first = (pl.program_id(0) == 0) & (pl.program_id(1) == 0)

    @pl.when(first)
    def _():
        carry_ref[...] = jnp.zeros_like(carry_ref)

    tm = x_ref.shape[0]
    d = x_ref.shape[1]
    bf = _dot(of_ref[...], wbf_ref[...])
    bs = _dot(os_ref[...], wbs_ref[...])
    merged = jax.nn.sigmoid(gates_ref[:, :d]) * bf + jax.nn.sigmoid(gates_ref[:, d:]) * bs
    y = _dot(merged.astype(BF16), wo_ref[...])
    x1 = x_ref[...] + mod_ref[2:3, :] * y
    x1_ref[...] = x1
    h2 = _rms_scale(x1) * g2_ref[...] * (1.0 + mod_ref[4:5, :]) + mod_ref[3:4, :]
    h_hi, h_lo = _split2(h2)
    w_hi = wr_hi_ref[...]
    logits = _dot(h_hi, w_hi) + _dot(h_hi, wr_lo_ref[...]) + _dot(h_lo, w_hi) + rb_ref[...]
    w_first, w_second, cls = _route(logits)

    lane = lax.broadcasted_iota(jnp.int32, (tm, LANES), 1)
    onehot = lane == cls
    oh = jnp.where(onehot, 1.0, 0.0)
    row = lax.broadcasted_iota(jnp.int32, (tm, tm), 0)
    col = lax.broadcasted_iota(jnp.int32, (tm, tm), 1)
    earlier = jnp.where(col < row, 1.0, 0.0).astype(BF16)
    before = _dot(earlier, oh.astype(BF16)) + carry_ref[0:1, :]
    rank = jnp.sum(jnp.where(onehot, before, 0.0), axis=1, keepdims=True)
    counts = carry_ref[0:1, :] + jnp.sum(oh, axis=0, keepdims=True)
    carry_ref[...] = jnp.broadcast_to(counts, carry_ref.shape)
    cnt_ref[...] = jnp.broadcast_to(counts, cnt_ref.shape)

    meta = (jnp.where(lane == 0, w_first, 0.0) + jnp.where(lane == 1, w_second, 0.0)
            + jnp.where(lane == 2, cls.astype(F32), 0.0) + jnp.where(lane == 3, rank, 0.0))
    meta_ref[...] = meta
    h_rows = d // LANES
    pitch = slab_ref.shape[0] // tm
    for s in range(pitch):
        if s < h_rows:
            val = h2[:, s * LANES:(s + 1) * LANES]
        else:
            val = meta if s == h_rows else jnp.zeros((tm, LANES), F32)
        slab_ref[pl.ds(s, tm, stride=pitch), :] = val


def _post_call(o_fox, o_sb, gates, x, mod3, w_bf, w_bs, w_o, norm_g, wr_hi, wr_lo, rb, tm):
    b, s, d = x.shape
    wdt = o_fox.shape[2]
    pitch = _slab_pitch(d // LANES + 1)
    const = lambda shape: pl.BlockSpec(shape, lambda bi, i: (0,) * len(shape),
                                       pipeline_mode=pl.Buffered(1))
    tile = lambda w: pl.BlockSpec((None, tm, w), lambda bi, i: (bi, i, 0))
    return pl.pallas_call(
        _post_kernel,
        out_shape=(jax.ShapeDtypeStruct((b, s, d), F32),
                   jax.ShapeDtypeStruct((b, s * pitch, LANES), F32),
                   jax.ShapeDtypeStruct((b, s, LANES), F32),
                   jax.ShapeDtypeStruct((8, LANES), F32)),
        grid=(b, s // tm),
        in_specs=[
            tile(wdt), tile(wdt), tile(2 * d), tile(d),
            pl.BlockSpec((None, N_MOD, d), lambda bi, i: (bi, 0, 0)),
            const((wdt, d)), const((wdt, d)), const((d, d)), const((1, d)),
            const((d, LANES)), const((d, LANES)), const((1, LANES)),
        ],
        out_specs=(tile(d),
                   pl.BlockSpec((None, tm * pitch, LANES), lambda bi, i: (bi, i, 0)),
                   tile(LANES),
                   pl.BlockSpec((8, LANES), lambda bi, i: (0, 0))),
        scratch_shapes=[pltpu.VMEM((8, LANES), F32)],
        compiler_params=pltpu.CompilerParams(
            dimension_semantics=("arbitrary", "arbitrary"), vmem_limit_bytes=VMEM_LIMIT),
        name="post",
    )(o_fox, o_sb, gates, x, mod3, w_bf, w_bs, w_o, norm_g, wr_hi, wr_lo, rb)


def _slab_pitch(rows):
    pitch = -(-rows // 4) * 4
    return pitch if (pitch // 4) % 2 else pitch + 4


def _dispatch_kernel(pitch, pos_ref, src_ref, init_ref, dst_ref, sem):
    del init_ref
    rows = src_ref.shape[0] // pitch
    base = pl.program_id(0) * rows

    def issue(r, carry):
        pltpu.make_async_copy(src_ref.at[pl.ds(r * pitch, pitch), :],
                              dst_ref.at[pl.ds(pos_ref[base + r] * pitch, pitch), :], sem).start()
        return carry

    lax.fori_loop(0, rows, issue, 0, unroll=8)
    pltpu.make_async_copy(src_ref, dst_ref.at[pl.ds(0, rows * pitch), :], sem).wait()


def _dispatch_call(slabs, pos, n_pad, rows, pitch):
    n = slabs.shape[0] // pitch
    any_spec = pl.BlockSpec(memory_space=pl.ANY)
    return pl.pallas_call(
        functools.partial(_dispatch_kernel, pitch),
        out_shape=jax.ShapeDtypeStruct((n_pad * pitch, LANES), slabs.dtype),
        grid_spec=pltpu.PrefetchScalarGridSpec(
            num_scalar_prefetch=1, grid=(n // rows,),
            in_specs=[pl.BlockSpec((rows * pitch, LANES), lambda t, pos: (t, 0)), any_spec],
            out_specs=any_spec,
            scratch_shapes=[pltpu.SemaphoreType.DMA(())]),
        input_output_aliases={2: 0},
        compiler_params=pltpu.CompilerParams(
            dimension_semantics=("arbitrary",), has_side_effects=True,
            vmem_limit_bytes=VMEM_LIMIT),
        name="dispatch",
    )(pos, slabs, jnp.zeros((n_pad * pitch, LANES), slabs.dtype))


def _moe_kernel(tm, elo_ref, ehi_ref, tix_ref, nv_ref, slab_ref, wg0_ref, wu0_ref, wd0_ref,
                wg1_ref, wu1_ref, wd1_ref, y_ref):
    y_rows = wd0_ref.shape[1] // LANES
    pitch = slab_ref.shape[0] // tm
    y_pitch = y_ref.shape[0] // tm
    valid = pl.program_id(0) < nv_ref[0]

    @pl.when(jnp.logical_not(valid))
    def _():
        y_ref[...] = jnp.zeros_like(y_ref)

    @pl.when(valid)
    def _():
        h = jnp.concatenate([slab_ref[pl.ds(s, tm, stride=pitch), :] for s in range(y_rows)],
                            axis=1).astype(BF16)
        meta = slab_ref[pl.ds(y_rows, tm, stride=pitch), :]

        def expert(wg_ref, wu_ref, wd_ref):
            g = _dot(h, wg_ref[...])
            u = _dot(h, wu_ref[...])
            act = (g * jax.nn.sigmoid(g) * u).astype(BF16)
            return _dot(act, wd_ref[...])

        y = (meta[:, 0:1] * expert(wg0_ref, wu0_ref, wd0_ref)
             + meta[:, 1:2] * expert(wg1_ref, wu1_ref, wd1_ref))
        for s in range(y_pitch):
            val = y[:, s * LANES:(s + 1) * LANES] if s < y_rows else jnp.zeros((tm, LANES), F32)
            y_ref[pl.ds(s, tm, stride=y_pitch), :] = val


def _moe_call(slabs, tile_elo, tile_ehi, tile_ix, n_valid, w_gate, w_up, w_down, tm, pitch,
              y_pitch):
    n_pad = slabs.shape[0] // pitch
    _, d, f = w_gate.shape
    lo = lambda t, elo, ehi, tix, nv: (elo[t], 0, 0)
    hi = lambda t, elo, ehi, tix, nv: (ehi[t], 0, 0)
    return pl.pallas_call(
        functools.partial(_moe_kernel, tm),
        out_shape=jax.ShapeDtypeStruct((n_pad * y_pitch, LANES), F32),
        grid_spec=pltpu.PrefetchScalarGridSpec(
            num_scalar_prefetch=4, grid=(n_pad // tm,),
            in_specs=[
                pl.BlockSpec((tm * pitch, LANES), lambda t, elo, ehi, tix, nv: (tix[t], 0)),
                pl.BlockSpec((None, d, f), lo), pl.BlockSpec((None, d, f), lo),
                pl.BlockSpec((None, f, d), lo),
                pl.BlockSpec((None, d, f), hi), pl.BlockSpec((None, d, f), hi),
                pl.BlockSpec((None, f, d), hi),
            ],
            out_specs=pl.BlockSpec((tm * y_pitch, LANES),
                                   lambda t, elo, ehi, tix, nv: (t, 0))),
        compiler_params=pltpu.CompilerParams(
            dimension_semantics=("arbitrary",), vmem_limit_bytes=VMEM_LIMIT),
        name="moe",
    )(tile_elo, tile_ehi, tile_ix, n_valid, slabs, w_gate, w_up, w_down, w_gate, w_up, w_down)


def _combine_kernel(pos_ref, ys_ref, x1_ref, mod_ref, fg_ref, o_ref, ybuf, sem):
    rows, d = x1_ref.shape
    y_rows = d // LANES
    y_pitch = ybuf.shape[1] // rows
    t = pl.program_id(0)

    def issue(step, slot):
        base = step * rows

        def body(r, carry):
            pltpu.make_async_copy(ys_ref.at[pl.ds(pos_ref[base + r] * y_pitch, y_rows), :],
                                  ybuf.at[slot, pl.ds(r * y_pitch, y_rows), :],
                                  sem.at[slot]).start()
            return carry

        lax.fori_loop(0, rows, body, 0, unroll=8)

    @pl.when(t == 0)
    def _():
        issue(0, 0)

    @pl.when(t + 1 < pl.num_programs(0))
    def _():
        issue(t + 1, (t + 1) % 2)

    slot = t % 2
    pltpu.make_async_copy(ys_ref.at[pl.ds(0, rows * y_rows), :],
                          ybuf.at[slot, pl.ds(0, rows * y_rows), :], sem.at[slot]).wait()
    ycur = ybuf.at[slot]
    y = jnp.concatenate([ycur[pl.ds(s, rows, stride=y_pitch), :] for s in range(y_rows)], axis=1)
    x2 = x1_ref[...] + mod_ref[5:6, :] * y
    o_ref[...] = _rms_scale(x2) * fg_ref[...]


def _combine_call(y_sorted, pos, x1, mod3, final_g, seq, rows, y_pitch):
    n, d = x1.shape
    per_seq = seq // rows
    return pl.pallas_call(
        _combine_kernel,
        out_shape=jax.ShapeDtypeStruct((n, d), F32),
        grid_spec=pltpu.PrefetchScalarGridSpec(
            num_scalar_prefetch=1, grid=(n // rows,),
            in_specs=[
                pl.BlockSpec(memory_space=pl.ANY),
                pl.BlockSpec((rows, d), lambda t, pos: (t, 0)),
                pl.BlockSpec((None, N_MOD, d), lambda t, pos: (t // per_seq, 0, 0)),
                pl.BlockSpec((1, d), lambda t, pos: (0, 0)),
            ],
            out_specs=pl.BlockSpec((rows, d), lambda t, pos: (t, 0)),
            scratch_shapes=[pltpu.VMEM((2, rows * y_pitch, LANES), F32),
                            pltpu.SemaphoreType.DMA((2,))]),
        compiler_params=pltpu.CompilerParams(
            dimension_semantics=("arbitrary",), vmem_limit_bytes=VMEM_LIMIT),
        name="combine",
    )(pos, y_sorted, x1, mod3, final_g)


def _pad_lanes(a):
    return jnp.pad(a.astype(F32), ((0, 0), (0, LANES - a.shape[1])))


def _bias_selectors(n_heads):
    rows = N_SPLIT * LANES
    n_pairs = n_heads // 2
    sel_q = np.zeros((rows, n_pairs * LANES), np.float32)
    sel_k = np.zeros((rows, n_pairs * LANES), np.float32)
    ones_row = n_heads
    for h in range(n_heads):
        base = (h // 2) * LANES + (h % 2) * 2 * N_SPLIT
        for j in range(N_SPLIT):
            sel_q[j * LANES + h, base + j] = 1.0
            sel_q[ones_row, base + N_SPLIT + j] = 1.0
            sel_k[ones_row, base + j] = 1.0
            sel_k[j * LANES + h, base + N_SPLIT + j] = -1.0
    return sel_q, sel_k


def kernel(x, c, ada_w, ada_b, norm1_g, w_in, b_forget, w_branch_fox, w_branch_sb, w_out,
           norm2_g, router_group_w, router_group_b, router_expert_w, router_expert_b,
           expert_w_gate, expert_w_up, expert_w_down, final_g):
    b, s, d = x.shape
    depth = ada_w.shape[0]
    hf = b_forget.shape[1]
    fw = hf * HEAD_DIM
    sw = w_branch_sb.shape[1]
    hs = sw // HEAD_DIM
    scale = HEAD_DIM ** -0.5
    n_exp = expert_w_gate.shape[1]

    assert depth == 1, "the final RMSNorm is fused into the single layer's MoE kernel"
    for l in range(depth):
        mod = _mod_call(c, ada_w[l], ada_b[l])
        mod3 = mod.reshape(b, N_MOD, d)

        wl = w_in[l]
        o = 0
        wq_a = wl[:, o:o + fw]; o += fw
        wk_a = wl[:, o:o + fw]; o += fw
        wv_a = wl[:, o:o + fw]; o += fw
        wf = wl[:, o:o + hf]; o += hf
        wq_b = wl[:, o:o + sw]; o += sw
        wk_b = wl[:, o:o + sw]; o += sw
        wv_b = wl[:, o:o + sw]; o += sw
        w_gates = wl[:, o:].astype(BF16)
        w_all = jnp.concatenate([wq_a * scale, wk_a, wq_b * scale, wk_b, wv_a, wv_b],
                                axis=1).astype(BF16)
        sel_q, sel_k = _bias_selectors(hf)
        sel = jnp.concatenate([sel_q, sel_k], axis=1).astype(BF16)
        w_f = _pad_lanes(wf).astype(BF16)
        b_f = _pad_lanes(b_forget[l].reshape(1, hf))

        q_a_off = 0
        k_a_off = fw
        q_b_off = 2 * fw
        k_b_off = q_b_off + sw
        v_a_off = k_b_off + sw
        v_b_off = v_a_off + fw
        qe_off = v_b_off + sw
        ke_off = qe_off + hf // 2 * LANES

        tq = min(s, 512)
        proj, gates, kstats = _in_call(x, mod3, norm1_g[l].reshape(1, d), w_all, w_gates, w_f, b_f,
                                       sel, hf, tm=tq)
        o_fox = _fox_call(proj, kstats, hf, q_a_off, qe_off, k_a_off, ke_off, v_a_off, tq, tq)
        o_sb = _sb_call(proj, hs, q_b_off, k_b_off, v_b_off, min(s, 512), min(s, 256), min(s, 512))

        wr = _pad_lanes(jnp.concatenate([router_group_w[l], router_expert_w[l]], axis=1))
        wr_hi = wr.astype(BF16)
        wr_lo = (wr - wr_hi.astype(F32)).astype(BF16)
        rb = _pad_lanes(jnp.concatenate([router_group_b[l], router_expert_b[l]]).reshape(1, -1))

        x1, slabs, meta, counts = _post_call(
            o_fox, o_sb, gates, x, mod3, w_branch_fox[l].astype(BF16), w_branch_sb[l].astype(BF16),
            w_out[l].astype(BF16), norm2_g[l].reshape(1, d), wr_hi, wr_lo, rb, tm=min(s, 512))

        n = b * s
        tm_e = MOE_TILE
        n_cls = N_GROUPS * N_PAIRS
        n_pad = n + n_cls * tm_e
        cls = meta[..., 2].reshape(n).astype(jnp.int32)
        rank = meta[..., 3].reshape(n).astype(jnp.int32)
        cnt = counts[0, :n_cls].astype(jnp.int32)
        padded = (cnt + tm_e - 1) // tm_e * tm_e
        ends = jnp.cumsum(padded)
        starts = ends - padded
        cls_ids = jnp.arange(n_cls, dtype=jnp.int32)
        pos = jnp.sum(jnp.where(cls[:, None] == cls_ids[None, :], starts[None, :], 0), axis=1) + rank
        n_valid = ends[-1] // tm_e
        tile_ix = jnp.minimum(jnp.arange(n_pad // tm_e, dtype=jnp.int32), n_valid - 1)
        tile_cls = jnp.sum((tile_ix * tm_e)[:, None] >= ends[None, :], axis=1)
        pair_lo = jnp.asarray([0, 0, 0, 1, 1, 2], jnp.int32)
        pair_hi = jnp.asarray([1, 2, 3, 2, 3, 3], jnp.int32)
        tile_grp = tile_cls // N_PAIRS * EXPERTS_PER_GROUP
        tile_elo = tile_grp + pair_lo[tile_cls % N_PAIRS]
        tile_ehi = tile_grp + pair_hi[tile_cls % N_PAIRS]

        rows = min(s, PERMUTE_ROWS)
        pitch = slabs.shape[1] // s
        y_pitch = _slab_pitch(d // LANES)
        sorted_slabs = _dispatch_call(slabs.reshape(n * pitch, LANES), pos, n_pad, rows, pitch)
        y_sorted = _moe_call(sorted_slabs, tile_elo, tile_ehi, tile_ix, n_valid.reshape(1),
                             expert_w_gate[l].astype(BF16), expert_w_up[l].astype(BF16),
                             expert_w_down[l].astype(BF16), tm_e, pitch, y_pitch)
        out = _combine_call(y_sorted, pos, x1.reshape(n, d), mod3, final_g.reshape(1, d), s, rows,
                            y_pitch)
        x = out.reshape(b, s, d)
    return x
```

```python
import functools

import jax
import jax.numpy as jnp
import numpy as np
from jax import lax
from jax.experimental import pallas as pl
from jax.experimental.pallas import tpu as pltpu

F32 = jnp.float32
BF16 = jnp.bfloat16

HEAD_DIM = 64
LANES = 128
ROW_TILE = 512
SB_KEY_TILE = 256
RMS_EPS = 1e-6
N_MOD = 6
N_GROUPS = 4
EXPERTS_PER_GROUP = 4
N_PAIRS = 6
N_SPLIT = 3
MOE_TILE = 256
DISPATCH_ROWS = 2048
COMBINE_ROWS = 1024
NEG_BIG = -1e30
LOG2E = 1.4426950408889634
SB_ZERO_BELOW = -104.0
FOX_ZERO_BELOW = -110.0
NORM_SLACK = 1.02
VMEM_LIMIT = 56 * 1024 * 1024


def _dot(a, b):
    return jnp.dot(a, b, preferred_element_type=F32)


def _dot_nt(a, b):
    return lax.dot_general(a, b, (((1,), (1,)), ((), ())), preferred_element_type=F32)


def _split2(x):
    hi = x.astype(BF16)
    lo = (x - hi.astype(F32)).astype(BF16)
    return hi, lo


def _split3(x):
    hi = x.astype(BF16)
    r = x - hi.astype(F32)
    mid = r.astype(BF16)
    lo = (r - mid.astype(F32)).astype(BF16)
    return hi, mid, lo


def _log_sigmoid(z):
    return jnp.minimum(z, 0.0) - jnp.log1p(jnp.exp(-jnp.abs(z)))


def _rms_scale(x):
    return x * lax.rsqrt(jnp.mean(x * x, axis=-1, keepdims=True) + RMS_EPS)


def _mod_kernel(c_ref, w_ref, b_ref, o_ref):
    c = c_ref[...]
    ca = c * jax.nn.sigmoid(c)
    c_hi, c_mid, c_lo = _split3(ca)
    w_hi, w_mid, w_lo = _split3(w_ref[...])
    acc = _dot(c_hi, w_hi) + _dot(c_hi, w_mid) + _dot(c_mid, w_hi)
    acc += _dot(c_mid, w_mid) + _dot(c_hi, w_lo) + _dot(c_lo, w_hi)
    o_ref[...] = acc + b_ref[...]


def _mod_call(c, ada_w, ada_b):
    b, d = c.shape
    n = ada_w.shape[1]
    bp = 16
    tn = n // N_MOD
    c_pad = jnp.pad(c, ((0, bp - b), (0, 0)))
    out = pl.pallas_call(
        _mod_kernel,
        out_shape=jax.ShapeDtypeStruct((bp, n), F32),
        grid=(n // tn,),
        in_specs=[
            pl.BlockSpec((bp, d), lambda j: (0, 0)),
            pl.BlockSpec((d, tn), lambda j: (0, j)),
            pl.BlockSpec((1, tn), lambda j: (0, j)),
        ],
        out_specs=pl.BlockSpec((bp, tn), lambda j: (0, j)),
        compiler_params=pltpu.CompilerParams(
            dimension_semantics=("arbitrary",), vmem_limit_bytes=VMEM_LIMIT),
        name="mod",
    )(c_pad, ada_w, ada_b.reshape(1, n))
    return out[:b]


def _in_kernel(n_heads, chunk, x_ref, mod_ref, g_ref, w_ref, wg_ref, wf_ref, bf_ref,
               sel_ref, proj_ref, gates_ref, stats_ref, carry_ref):
    i = pl.program_id(1)
    tm = x_ref.shape[0]

    @pl.when(i == 0)
    def _():
        carry_ref[...] = jnp.zeros_like(carry_ref)

    x = x_ref[...]
    shift = mod_ref[0:1, :]
    scale = mod_ref[1:2, :]
    h = _rms_scale(x) * g_ref[...] * (1.0 + scale) + shift
    hb = h.astype(BF16)

    lane = lax.broadcasted_iota(jnp.int32, (tm, LANES), 1)
    lf = _log_sigmoid(_dot(hb, wf_ref[...]) + bf_ref[...])
    lf = jnp.where(lane < n_heads, lf, 0.0)
    row = lax.broadcasted_iota(jnp.int32, (tm, tm), 0)
    col = lax.broadcasted_iota(jnp.int32, (tm, tm), 1)
    tri = jnp.where(col <= row, 1.0, 0.0).astype(BF16)
    l_hi, l_mid, l_lo = _split3(lf)
    fcum = _dot(tri, l_hi) + _dot(tri, l_mid) + _dot(tri, l_lo) + carry_ref[0:1, :]
    carry_ref[...] = jnp.broadcast_to(fcum[tm - 1:tm, :], carry_ref.shape)
    fx = jnp.where(lane == n_heads, 1.0, fcum)
    f_hi, f_mid, f_lo = _split3(fx)
    hml = jnp.concatenate([f_hi, f_mid, f_lo], axis=1)

    srow = lax.broadcasted_iota(jnp.int32, (8, LANES), 0)
    slane = lax.broadcasted_iota(jnp.int32, (8, LANES), 1)
    stats = jnp.where(srow == 1, jnp.broadcast_to(fcum[tm - 1:tm, :], (8, LANES)), 0.0)
    k_off = n_heads * HEAD_DIM

    n_w = w_ref.shape[1]
    for c0 in range(0, n_w, chunk):
        acc = _dot(hb, w_ref[:, c0:c0 + chunk])
        proj_ref[:, c0:c0 + chunk] = acc.astype(BF16)
        if k_off <= c0 < 2 * k_off:
            for bl in range(chunk // LANES):
                kb = acc[:, bl * LANES:(bl + 1) * LANES]
                kb2 = kb * kb
                for half in range(2):
                    in_head = (lane >= half * HEAD_DIM) & (lane < (half + 1) * HEAD_DIM)
                    sq = jnp.sum(jnp.where(in_head, kb2, 0.0), axis=1, keepdims=True)
                    head = (c0 - k_off) // HEAD_DIM + 2 * bl + half
                    top = jnp.max(sq, axis=0, keepdims=True)
                    stats = jnp.where((srow == 0) & (slane == head), top, stats)
    stats_ref[...] = stats
    for c0 in range(0, sel_ref.shape[1], chunk):
        proj_ref[:, n_w + c0:n_w + c0 + chunk] = _dot(hml, sel_ref[:, c0:c0 + chunk]).astype(BF16)
    n_gate = wg_ref.shape[1]
    for c0 in range(0, n_gate, chunk):
        gates_ref[:, c0:c0 + chunk] = _dot(hb, wg_ref[:, c0:c0 + chunk])


def _in_call(x, mod3, norm_g, w_all, w_gates, w_f, b_f, sel, n_heads, tm):
    b, s, d = x.shape
    n_w = w_all.shape[1]
    n_ext = sel.shape[1]
    n_proj = n_w + n_ext
    n_gate = w_gates.shape[1]
    chunk = 512
    assert n_heads * HEAD_DIM % chunk == 0 and n_w % chunk == 0 and n_ext % chunk == 0
    const = lambda shape: pl.BlockSpec(shape, lambda bi, i: (0,) * len(shape),
                                       pipeline_mode=pl.Buffered(1))
    return pl.pallas_call(
        functools.partial(_in_kernel, n_heads, chunk),
        out_shape=(jax.ShapeDtypeStruct((b, s, n_proj), BF16),
                   jax.ShapeDtypeStruct((b, s, n_gate), F32),
                   jax.ShapeDtypeStruct((b, s // tm, 8, LANES), F32)),
        grid=(b, s // tm),
        in_specs=[
            pl.BlockSpec((None, tm, d), lambda bi, i: (bi, i, 0)),
            pl.BlockSpec((None, N_MOD, d), lambda bi, i: (bi, 0, 0)),
            const((1, d)),
            const((d, n_w)),
            const((d, n_gate)),
            const((d, LANES)),
            const((1, LANES)),
            const((3 * LANES, n_ext)),
        ],
        out_specs=(pl.BlockSpec((None, tm, n_proj), lambda bi, i: (bi, i, 0)),
                   pl.BlockSpec((None, tm, n_gate), lambda bi, i: (bi, i, 0)),
                   pl.BlockSpec((None, None, 8, LANES), lambda bi, i: (bi, i, 0, 0))),
        scratch_shapes=[pltpu.VMEM((8, LANES), F32)],
        compiler_params=pltpu.CompilerParams(
            dimension_semantics=("arbitrary", "arbitrary"), vmem_limit_bytes=VMEM_LIMIT),
        name="in_proj",
    )(x, mod3, norm_g, w_all, w_gates, w_f, b_f, sel)


def _head_lanes(lane, hh):
    return (lane >= hh * HEAD_DIM) & (lane < (hh + 1) * HEAD_DIM)


def _bias_lanes(lane, hh, lo=0, hi=2 * N_SPLIT):
    return (lane >= hh * 2 * N_SPLIT + lo) & (lane < hh * 2 * N_SPLIT + hi)


def _fox_kernel(tk, n_heads, kn_ref, fe_ref, q_ref, qe_ref, k_ref, ke_ref, v_ref, o_ref):
    bi = pl.program_id(0)
    pair = pl.program_id(1)
    i = pl.program_id(2)
    tq = q_ref.shape[0]
    n_diag = tq // tk
    n_kt = k_ref.shape[0] // tk
    row = lax.broadcasted_iota(jnp.int32, (tq, tk), 0)
    col = lax.broadcasted_iota(jnp.int32, (tq, tk), 1)
    lane = lax.broadcasted_iota(jnp.int32, (tq, LANES), 1)
    q_heads = [jnp.concatenate([jnp.where(_head_lanes(lane, hh), q_ref[...], 0),
                                jnp.where(_bias_lanes(lane, hh), qe_ref[...], 0)], axis=1)
               for hh in range(2)]

    qf = q_ref[...].astype(F32)
    qef = qe_ref[...].astype(F32)
    q_norm, q_f = [], []
    for hh in range(2):
        q_norm.append(jnp.sqrt(jnp.sum(jnp.where(_head_lanes(lane, hh), qf * qf, 0.0),
                                       axis=1, keepdims=True)))
        q_f.append(jnp.sum(jnp.where(_bias_lanes(lane, hh, 0, N_SPLIT), qef, 0.0),
                           axis=1, keepdims=True))

    def may_contribute(j, states):
        worst = []
        for hh, (m, _, _) in enumerate(states):
            idx = (bi * n_kt + j) * n_heads + 2 * pair + hh
            bound = q_norm[hh] * (kn_ref[idx] * NORM_SLACK) + (q_f[hh] - fe_ref[idx]) - m
            worst.append(jnp.max(bound))
        return jnp.maximum(worst[0], worst[1]) >= FOX_ZERO_BELOW

    def head_step(hh, j, state, masked):
        m, l, acc = state
        k0 = pl.multiple_of(j * tk, tk)
        k_all = jnp.concatenate([k_ref[pl.ds(k0, tk), :], ke_ref[pl.ds(k0, tk), :]], axis=1)
        s = _dot_nt(q_heads[hh], k_all)
        if masked:
            s = jnp.where(col + (j - i * n_diag) * tk <= row, s, NEG_BIG)
        m_new = jnp.maximum(m, jnp.max(s, axis=1, keepdims=True))
        alpha = jnp.exp(m - m_new)
        p = jnp.exp(s - m_new)
        l = alpha * l + jnp.sum(p, axis=1, keepdims=True)
        acc = alpha * acc + _dot(p.astype(BF16), v_ref[pl.ds(k0, tk), :])
        return m_new, l, acc

    def pair_step(j, states, masked):
        return tuple(head_step(hh, j, st, masked) for hh, st in enumerate(states))

    init = (jnp.full((tq, 1), NEG_BIG, F32), jnp.zeros((tq, 1), F32),
            jnp.zeros((tq, LANES), F32))
    states = (init, init)
    for dj in range(n_diag):
        states = pair_step(i * n_diag + dj, states, True)

    n_full = i * n_diag

    def cond(carry):
        t, go, _ = carry
        return (t < n_full) & go

    def body(carry):
        t, _, sts = carry
        sts = pair_step(n_full - 1 - t, sts, False)
        return t + 1, may_contribute(jnp.maximum(n_full - 2 - t, 0), sts), sts

    go0 = may_contribute(jnp.maximum(n_full - 1, 0), states)
    _, _, states = lax.while_loop(cond, body, (jnp.int32(0), go0, states))
    outs = [acc / l for _, l, acc in states]
    o_ref[...] = jnp.where(lane < HEAD_DIM, outs[0], outs[1]).astype(o_ref.dtype)


def _fox_call(proj, stats, n_heads, q_off, qe_off, k_off, ke_off, v_off, tq, tk):
    b, s, _ = proj.shape
    assert stats.shape[1] == s // tk
    n_pairs = n_heads // 2
    k_norm = lax.cummax(jnp.sqrt(stats[:, :, 0, :n_heads]), axis=1).reshape(-1)
    f_end = stats[:, :, 1, :n_heads].reshape(-1)
    q_tile = lambda off: pl.BlockSpec(
        (None, tq, LANES), lambda bi, p, i, kn, fe: (bi, i, off // LANES + p))
    kv_all = lambda off: pl.BlockSpec(
        (None, s, LANES), lambda bi, p, i, kn, fe: (bi, 0, off // LANES + p))
    return pl.pallas_call(
        functools.partial(_fox_kernel, tk, n_heads),
        out_shape=jax.ShapeDtypeStruct((b, s, n_heads * HEAD_DIM), BF16),
        grid_spec=pltpu.PrefetchScalarGridSpec(
            num_scalar_prefetch=2, grid=(b, n_pairs, s // tq),
            in_specs=[q_tile(q_off), q_tile(qe_off), kv_all(k_off), kv_all(ke_off),
                      kv_all(v_off)],
            out_specs=pl.BlockSpec((None, tq, LANES), lambda bi, p, i, kn, fe: (bi, i, p))),
        compiler_params=pltpu.CompilerParams(
            dimension_semantics=("arbitrary", "arbitrary", "arbitrary"),
            vmem_limit_bytes=VMEM_LIMIT),
        name="fox_attn",
    )(k_norm, f_end, proj, proj, proj, proj, proj)


def _sb_kernel(tk, tr, q_ref, k_ref, v_ref, o_ref):
    i = pl.program_id(2)
    tq = q_ref.shape[0]
    n_chunks = tq // tr
    n_diag = tr // tk
    row = lax.broadcasted_iota(jnp.int32, (tr, tk), 0)
    col = lax.broadcasted_iota(jnp.int32, (tr, tk), 1)
    lane = lax.broadcasted_iota(jnp.int32, (tq, LANES), 1)
    klane = lax.broadcasted_iota(jnp.int32, (tk, LANES), 1)
    ur = lax.broadcasted_iota(jnp.int32, (2 * tk, tk), 0)
    uc = lax.broadcasted_iota(jnp.int32, (2 * tk, tk), 1)
    upper2 = jnp.where((ur & (tk - 1)) > uc, 1.0, 0.0).astype(BF16)
    chains = [(hh, r) for r in range(n_chunks) for hh in range(2)]

    def chain_step(hh, r, j, state, dj=None):
        masked = dj is not None
        c, acc = state
        q = q_ref[r * tr:(r + 1) * tr, :]
        k0 = pl.multiple_of(j * tk, tk)
        k = jnp.where(_head_lanes(klane, hh), k_ref[pl.ds(k0, tk), :], 0)
        z = _dot_nt(q, k)
        t = jnp.exp2(jnp.abs(z) * (-LOG2E))
        ls = jnp.minimum(z, 0.0) - jnp.log(1.0 + t)
        lom = ls - z
        if masked:
            strict = col + dj * tk < row
            lom = jnp.where(strict, lom, 0.0)
        hi = pltpu.bitcast(pltpu.bitcast(lom, jnp.uint32) & jnp.uint32(0xFFFF0000), F32)
        parts = jnp.concatenate([hi.astype(BF16), (lom - hi).astype(BF16)], axis=1)
        within = _dot(parts, upper2)
        a = jnp.exp2((ls + within + c) * LOG2E)
        if masked:
            a = jnp.where(strict, a, 0.0)
        acc = acc + _dot(a.astype(BF16), v_ref[pl.ds(k0, tk), :])
        c = c + within[:, 0:1] + lom[:, 0:1]
        return c, acc

    base = i * (tq // tk)
    states = []
    for hh, r in chains:
        st = (jnp.zeros((tr, 1), F32), jnp.zeros((tr, LANES), F32))
        for dj in reversed(range(n_diag)):
            st = chain_step(hh, r, base + r * n_diag + dj, st, dj)
        for j in reversed(range(r * n_diag)):
            st = chain_step(hh, r, base + j, st)
        states.append(st)

    def c_max(sts):
        return functools.reduce(jnp.maximum, [jnp.max(st[0]) for st in sts])

    def cond(carry):
        t, cm, _ = carry
        return (t < base) & (cm >= SB_ZERO_BELOW)

    def body(carry):
        t, _, sts = carry
        j = base - 1 - t
        sts = tuple(chain_step(hh, r, j, st) for (hh, r), st in zip(chains, sts))
        return t + 1, c_max(sts), sts

    states = tuple(states)
    _, _, states = lax.while_loop(cond, body, (jnp.int32(0), c_max(states), states))
    acc = {ch: st[1] for ch, st in zip(chains, states)}
    o0 = jnp.concatenate([acc[(0, r)] for r in range(n_chunks)], axis=0)
    o1 = jnp.concatenate([acc[(1, r)] for r in range(n_chunks)], axis=0)
    o_ref[...] = jnp.where(lane < HEAD_DIM, o0, o1).astype(o_ref.dtype)


def _sb_call(proj, n_heads, q_off, k_off, v_off, tq, tk, tr):
    b, s, _ = proj.shape
    n_pairs = n_heads // 2
    return pl.pallas_call(
        functools.partial(_sb_kernel, tk, tr),
        out_shape=jax.ShapeDtypeStruct((b, s, n_heads * HEAD_DIM), BF16),
        grid=(b, n_pairs, s // tq),
        in_specs=[
            pl.BlockSpec((None, tq, LANES), lambda bi, p, i: (bi, i, q_off // LANES + p)),
            pl.BlockSpec((None, s, LANES), lambda bi, p, i: (bi, 0, k_off // LANES + p)),
            pl.BlockSpec((None, s, LANES), lambda bi, p, i: (bi, 0, v_off // LANES + p)),
        ],
        out_specs=pl.BlockSpec((None, tq, LANES), lambda bi, p, i: (bi, i, p)),
        compiler_params=pltpu.CompilerParams(
            dimension_semantics=("arbitrary", "arbitrary", "arbitrary"),
            vmem_limit_bytes=VMEM_LIMIT),
        name="sb_attn",
    )(proj, proj, proj)


def _route(logits):
    tm = logits.shape[0]
    lane = lax.broadcasted_iota(jnp.int32, (tm, LANES), 1)
    big = jnp.int32(LANES)
    g_mask = lane < N_GROUPS
    lg = jnp.where(g_mask, logits, -jnp.inf)
    g_max = jnp.max(lg, axis=1, keepdims=True)
    g_exp = jnp.exp(lg - g_max)
    g_prob = g_exp / jnp.sum(g_exp, axis=1, keepdims=True)
    g_top = jnp.max(g_prob, axis=1, keepdims=True)
    g_idx = jnp.min(jnp.where(g_mask & (g_prob == g_top), lane, big), axis=1, keepdims=True)
    lo = N_GROUPS + EXPERTS_PER_GROUP * g_idx
    e_mask = (lane >= lo) & (lane < lo + EXPERTS_PER_GROUP)
    le = jnp.where(e_mask, logits, -jnp.inf)
    e_max = jnp.max(le, axis=1, keepdims=True)
    e_exp = jnp.exp(le - e_max)
    e_prob = e_exp / jnp.sum(e_exp, axis=1, keepdims=True)
    p1 = jnp.max(e_prob, axis=1, keepdims=True)
    i1 = jnp.min(jnp.where(e_mask & (e_prob == p1), lane, big), axis=1, keepdims=True)
    rest = e_mask & (lane != i1)
    p2 = jnp.max(jnp.where(rest, e_prob, -1.0), axis=1, keepdims=True)
    i2 = jnp.min(jnp.where(rest & (e_prob == p2), lane, big), axis=1, keepdims=True)
    tot = p1 + p2
    w1 = p1 / tot * g_top
    w2 = p2 / tot * g_top
    first_lower = i1 < i2
    e_lo = jnp.minimum(i1, i2) - lo
    e_hi = jnp.maximum(i1, i2) - lo
    pair = jnp.where(e_lo == 0, 0, jnp.where(e_lo == 1, 3, 5)) + (e_hi - e_lo - 1)
    cls = g_idx * N_PAIRS + pair
    return jnp.where(first_lower, w1, w2), jnp.where(first_lower, w2, w1), cls


def _post_kernel(of_ref, os_ref, gates_ref, x_ref, mod_ref, wbf_ref, wbs_ref, wo_ref, g2_ref,
                 wr_hi_ref, wr_lo_ref, rb_ref, x1_ref, slab_ref, meta_ref, cnt_ref, carry_ref):
    first = (pl.program_id(0) == 0) & (pl.program_id(1) == 0)

    @pl.when(first)
    def _():
        carry_ref[...] = jnp.zeros_like(carry_ref)

    tm = x_ref.shape[0]
    d = x_ref.shape[1]
    bf = _dot(of_ref[...], wbf_ref[...])
    bs = _dot(os_ref[...], wbs_ref[...])
    merged = jax.nn.sigmoid(gates_ref[:, :d]) * bf + jax.nn.sigmoid(gates_ref[:, d:]) * bs
    y = _dot(merged.astype(BF16), wo_ref[...])
    x1 = x_ref[...] + mod_ref[2:3, :] * y
    x1_ref[...] = x1
    h2 = _rms_scale(x1) * g2_ref[...] * (1.0 + mod_ref[4:5, :]) + mod_ref[3:4, :]
    h_hi, h_lo = _split2(h2)
    w_hi = wr_hi_ref[...]
    logits = _dot(h_hi, w_hi) + _dot(h_hi, wr_lo_ref[...]) + _dot(h_lo, w_hi) + rb_ref[...]
    w_first, w_second, cls = _route(logits)

    lane = lax.broadcasted_iota(jnp.int32, (tm, LANES), 1)
    onehot = lane == cls
    oh = jnp.where(onehot, 1.0, 0.0)
    row = lax.broadcasted_iota(jnp.int32, (tm, tm), 0)
    col = lax.broadcasted_iota(jnp.int32, (tm, tm), 1)
    earlier = jnp.where(col < row, 1.0, 0.0).astype(BF16)
    before = _dot(earlier, oh.astype(BF16)) + carry_ref[0:1, :]
    rank = jnp.sum(jnp.where(onehot, before, 0.0), axis=1, keepdims=True)
    counts = carry_ref[0:1, :] + jnp.sum(oh, axis=0, keepdims=True)
    carry_ref[...] = jnp.broadcast_to(counts, carry_ref.shape)
    cnt_ref[...] = jnp.broadcast_to(counts, cnt_ref.shape)

    meta = (jnp.where(lane == 0, w_first, 0.0) + jnp.where(lane == 1, w_second, 0.0)
            + jnp.where(lane == 2, cls.astype(F32), 0.0) + jnp.where(lane == 3, rank, 0.0))
    meta_ref[...] = meta
    h_rows = d // LANES
    pitch = slab_ref.shape[0] // tm
    for s in range(pitch):
        if s < h_rows:
            val = h2[:, s * LANES:(s + 1) * LANES]
        else:
            val = meta if s == h_rows else jnp.zeros((tm, LANES), F32)
        slab_ref[pl.ds(s, tm, stride=pitch), :] = val


def _post_call(o_fox, o_sb, gates, x, mod3, w_bf, w_bs, w_o, norm_g, wr_hi, wr_lo, rb, tm):
    b, s, d = x.shape
    wdt = o_fox.shape[2]
    pitch = _slab_pitch(d // LANES + 1)
    const = lambda shape: pl.BlockSpec(shape, lambda bi, i: (0,) * len(shape),
                                       pipeline_mode=pl.Buffered(1))
    tile = lambda w: pl.BlockSpec((None, tm, w), lambda bi, i: (bi, i, 0))
    return pl.pallas_call(
        _post_kernel,
        out_shape=(jax.ShapeDtypeStruct((b, s, d), F32),
                   jax.ShapeDtypeStruct((b, s * pitch, LANES), F32),
                   jax.ShapeDtypeStruct((b, s, LANES), F32),
                   jax.ShapeDtypeStruct((8, LANES), F32)),
        grid=(b, s // tm),
        in_specs=[
            tile(wdt), tile(wdt), tile(2 * d), tile(d),
            pl.BlockSpec((None, N_MOD, d), lambda bi, i: (bi, 0, 0)),
            const((wdt, d)), const((wdt, d)), const((d, d)), const((1, d)),
            const((d, LANES)), const((d, LANES)), const((1, LANES)),
        ],
        out_specs=(tile(d),
                   pl.BlockSpec((None, tm * pitch, LANES), lambda bi, i: (bi, i, 0)),
                   tile(LANES),
                   pl.BlockSpec((8, LANES), lambda bi, i: (0, 0))),
        scratch_shapes=[pltpu.VMEM((8, LANES), F32)],
        compiler_params=pltpu.CompilerParams(
            dimension_semantics=("arbitrary", "arbitrary"), vmem_limit_bytes=VMEM_LIMIT),
        name="post",
    )(o_fox, o_sb, gates, x, mod3, w_bf, w_bs, w_o, norm_g, wr_hi, wr_lo, rb)


def _slab_pitch(rows):
    pitch = -(-rows // 4) * 4
    return pitch if (pitch // 4) % 2 else pitch + 4


def _dispatch_kernel(pitch, pos_ref, src_ref, init_ref, dst_ref, sem):
    del init_ref
    rows = src_ref.shape[0] // pitch
    base = pl.program_id(0) * rows

    def issue(r, carry):
        pltpu.make_async_copy(src_ref.at[pl.ds(r * pitch, pitch), :],
                              dst_ref.at[pl.ds(pos_ref[base + r] * pitch, pitch), :], sem).start()
        return carry

    lax.fori_loop(0, rows, issue, 0, unroll=8)
    pltpu.make_async_copy(src_ref, dst_ref.at[pl.ds(0, rows * pitch), :], sem).wait()


def _dispatch_call(slabs, pos, n_pad, rows, pitch):
    n = slabs.shape[0] // pitch
    any_spec = pl.BlockSpec(memory_space=pl.ANY)
    return pl.pallas_call(
        functools.partial(_dispatch_kernel, pitch),
        out_shape=jax.ShapeDtypeStruct((n_pad * pitch, LANES), slabs.dtype),
        grid_spec=pltpu.PrefetchScalarGridSpec(
            num_scalar_prefetch=1, grid=(n // rows,),
            in_specs=[pl.BlockSpec((rows * pitch, LANES), lambda t, pos: (t, 0)), any_spec],
            out_specs=any_spec,
            scratch_shapes=[pltpu.SemaphoreType.DMA(())]),
        input_output_aliases={2: 0},
        compiler_params=pltpu.CompilerParams(
            dimension_semantics=("arbitrary",), has_side_effects=True,
            vmem_limit_bytes=VMEM_LIMIT),
        name="dispatch",
    )(pos, slabs, jnp.zeros((n_pad * pitch, LANES), slabs.dtype))


def _moe_kernel(tm, elo_ref, ehi_ref, tix_ref, nv_ref, slab_ref, wg0_ref, wu0_ref, wd0_ref,
                wg1_ref, wu1_ref, wd1_ref, y_ref):
    y_rows = wd0_ref.shape[1] // LANES
    pitch = slab_ref.shape[0] // tm
    y_pitch = y_ref.shape[0] // tm
    valid = pl.program_id(0) < nv_ref[0]

    @pl.when(jnp.logical_not(valid))
    def _():
        y_ref[...] = jnp.zeros_like(y_ref)

    @pl.when(valid)
    def _():
        h = jnp.concatenate([slab_ref[pl.ds(s, tm, stride=pitch), :] for s in range(y_rows)],
                            axis=1).astype(BF16)
        meta = slab_ref[pl.ds(y_rows, tm, stride=pitch), :]

        def expert(wg_ref, wu_ref, wd_ref):
            g = _dot(h, wg_ref[...])
            u = _dot(h, wu_ref[...])
            act = (g * jax.nn.sigmoid(g) * u).astype(BF16)
            return _dot(act, wd_ref[...])

        y = (meta[:, 0:1] * expert(wg0_ref, wu0_ref, wd0_ref)
             + meta[:, 1:2] * expert(wg1_ref, wu1_ref, wd1_ref))
        for s in range(y_pitch):
            val = y[:, s * LANES:(s + 1) * LANES] if s < y_rows else jnp.zeros((tm, LANES), F32)
            y_ref[pl.ds(s, tm, stride=y_pitch), :] = val


def _moe_call(slabs, tile_elo, tile_ehi, tile_ix, n_valid, w_gate, w_up, w_down, tm, pitch,
              y_pitch):
    n_pad = slabs.shape[0] // pitch
    _, d, f = w_gate.shape
    lo = lambda t, elo, ehi, tix, nv: (elo[t], 0, 0)
    hi = lambda t, elo, ehi, tix, nv: (ehi[t], 0, 0)
    return pl.pallas_call(
        functools.partial(_moe_kernel, tm),
        out_shape=jax.ShapeDtypeStruct((n_pad * y_pitch, LANES), F32),
        grid_spec=pltpu.PrefetchScalarGridSpec(
            num_scalar_prefetch=4, grid=(n_pad // tm,),
            in_specs=[
                pl.BlockSpec((tm * pitch, LANES), lambda t, elo, ehi, tix, nv: (tix[t], 0)),
                pl.BlockSpec((None, d, f), lo), pl.BlockSpec((None, d, f), lo),
                pl.BlockSpec((None, f, d), lo),
                pl.BlockSpec((None, d, f), hi), pl.BlockSpec((None, d, f), hi),
                pl.BlockSpec((None, f, d), hi),
            ],
            out_specs=pl.BlockSpec((tm * y_pitch, LANES),
                                   lambda t, elo, ehi, tix, nv: (t, 0))),
        compiler_params=pltpu.CompilerParams(
            dimension_semantics=("arbitrary",), vmem_limit_bytes=VMEM_LIMIT),
        name="moe",
    )(tile_elo, tile_ehi, tile_ix, n_valid, slabs, w_gate, w_up, w_down, w_gate, w_up, w_down)


def _combine_kernel(pos_ref, ys_ref, x1_ref, mod_ref, fg_ref, o_ref, ybuf, sem):
    rows, d = x1_ref.shape
    y_rows = d // LANES
    y_pitch = ybuf.shape[1] // rows
    t = pl.program_id(0)

    def issue(step, slot):
        base = step * rows

        def body(r, carry):
            pltpu.make_async_copy(ys_ref.at[pl.ds(pos_ref[base + r] * y_pitch, y_rows), :],
                                  ybuf.at[slot, pl.ds(r * y_pitch, y_rows), :],
                                  sem.at[slot]).start()
            return carry

        lax.fori_loop(0, rows, body, 0, unroll=8)

    @pl.when(t == 0)
    def _():
        issue(0, 0)

    @pl.when(t + 1 < pl.num_programs(0))
    def _():
        issue(t + 1, (t + 1) % 2)

    slot = t % 2
    pltpu.make_async_copy(ys_ref.at[pl.ds(0, rows * y_rows), :],
                          ybuf.at[slot, pl.ds(0, rows * y_rows), :], sem.at[slot]).wait()
    ycur = ybuf.at[slot]
    y = jnp.concatenate([ycur[pl.ds(s, rows, stride=y_pitch), :] for s in range(y_rows)], axis=1)
    x2 = x1_ref[...] + mod_ref[5:6, :] * y
    o_ref[...] = _rms_scale(x2) * fg_ref[...]


def _combine_call(y_sorted, pos, x1, mod3, final_g, seq, rows, y_pitch):
    n, d = x1.shape
    per_seq = seq // rows
    return pl.pallas_call(
        _combine_kernel,
        out_shape=jax.ShapeDtypeStruct((n, d), F32),
        grid_spec=pltpu.PrefetchScalarGridSpec(
            num_scalar_prefetch=1, grid=(n // rows,),
            in_specs=[
                pl.BlockSpec(memory_space=pl.ANY),
                pl.BlockSpec((rows, d), lambda t, pos: (t, 0)),
                pl.BlockSpec((None, N_MOD, d), lambda t, pos: (t // per_seq, 0, 0)),
                pl.BlockSpec((1, d), lambda t, pos: (0, 0)),
            ],
            out_specs=pl.BlockSpec((rows, d), lambda t, pos: (t, 0)),
            scratch_shapes=[pltpu.VMEM((2, rows * y_pitch, LANES), F32),
                            pltpu.SemaphoreType.DMA((2,))]),
        compiler_params=pltpu.CompilerParams(
            dimension_semantics=("arbitrary",), vmem_limit_bytes=VMEM_LIMIT),
        name="combine",
    )(pos, y_sorted, x1, mod3, final_g)


def _pad_lanes(a):
    return jnp.pad(a.astype(F32), ((0, 0), (0, LANES - a.shape[1])))


def _bias_selectors(n_heads):
    rows = N_SPLIT * LANES
    n_pairs = n_heads // 2
    sel_q = np.zeros((rows, n_pairs * LANES), np.float32)
    sel_k = np.zeros((rows, n_pairs * LANES), np.float32)
    ones_row = n_heads
    for h in range(n_heads):
        base = (h // 2) * LANES + (h % 2) * 2 * N_SPLIT
        for j in range(N_SPLIT):
            sel_q[j * LANES + h, base + j] = 1.0
            sel_q[ones_row, base + N_SPLIT + j] = 1.0
            sel_k[ones_row, base + j] = 1.0
            sel_k[j * LANES + h, base + N_SPLIT + j] = -1.0
    return sel_q, sel_k


def kernel(x, c, ada_w, ada_b, norm1_g, w_in, b_forget, w_branch_fox, w_branch_sb, w_out,
           norm2_g, router_group_w, router_group_b, router_expert_w, router_expert_b,
           expert_w_gate, expert_w_up, expert_w_down, final_g):
    b, s, d = x.shape
    depth = ada_w.shape[0]
    hf = b_forget.shape[1]
    fw = hf * HEAD_DIM
    sw = w_branch_sb.shape[1]
    hs = sw // HEAD_DIM
    scale = HEAD_DIM ** -0.5
    n_exp = expert_w_gate.shape[1]

    assert depth == 1, "the final RMSNorm is fused into the single layer's MoE kernel"
    for l in range(depth):
        mod = _mod_call(c, ada_w[l], ada_b[l])
        mod3 = mod.reshape(b, N_MOD, d)

        wl = w_in[l]
        o = 0
        wq_a = wl[:, o:o + fw]; o += fw
        wk_a = wl[:, o:o + fw]; o += fw
        wv_a = wl[:, o:o + fw]; o += fw
        wf = wl[:, o:o + hf]; o += hf
        wq_b = wl[:, o:o + sw]; o += sw
        wk_b = wl[:, o:o + sw]; o += sw
        wv_b = wl[:, o:o + sw]; o += sw
        w_gates = wl[:, o:].astype(BF16)
        w_all = jnp.concatenate([wq_a * scale, wk_a, wq_b * scale, wk_b, wv_a, wv_b],
                                axis=1).astype(BF16)
        sel_q, sel_k = _bias_selectors(hf)
        sel = jnp.concatenate([sel_q, sel_k], axis=1).astype(BF16)
        w_f = _pad_lanes(wf).astype(BF16)
        b_f = _pad_lanes(b_forget[l].reshape(1, hf))

        q_a_off = 0
        k_a_off = fw
        q_b_off = 2 * fw
        k_b_off = q_b_off + sw
        v_a_off = k_b_off + sw
        v_b_off = v_a_off + fw
        qe_off = v_b_off + sw
        ke_off = qe_off + hf // 2 * LANES

        tq = min(s, ROW_TILE)
        proj, gates, kstats = _in_call(x, mod3, norm1_g[l].reshape(1, d), w_all, w_gates, w_f, b_f,
                                       sel, hf, tm=tq)
        o_fox = _fox_call(proj, kstats, hf, q_a_off, qe_off, k_a_off, ke_off, v_a_off, tq, tq)
        o_sb = _sb_call(proj, hs, q_b_off, k_b_off, v_b_off, tq, min(s, SB_KEY_TILE), tq)

        wr = _pad_lanes(jnp.concatenate([router_group_w[l], router_expert_w[l]], axis=1))
        wr_hi = wr.astype(BF16)
        wr_lo = (wr - wr_hi.astype(F32)).astype(BF16)
        rb = _pad_lanes(jnp.concatenate([router_group_b[l], router_expert_b[l]]).reshape(1, -1))

        x1, slabs, meta, counts = _post_call(
            o_fox, o_sb, gates, x, mod3, w_branch_fox[l].astype(BF16), w_branch_sb[l].astype(BF16),
            w_out[l].astype(BF16), norm2_g[l].reshape(1, d), wr_hi, wr_lo, rb, tm=tq)

        n = b * s
        tm_e = MOE_TILE
        n_cls = N_GROUPS * N_PAIRS
        n_pad = n + n_cls * tm_e
        cls = meta[..., 2].reshape(n).astype(jnp.int32)
        rank = meta[..., 3].reshape(n).astype(jnp.int32)
        cnt = counts[0, :n_cls].astype(jnp.int32)
        padded = (cnt + tm_e - 1) // tm_e * tm_e
        ends = jnp.cumsum(padded)
        starts = ends - padded
        cls_ids = jnp.arange(n_cls, dtype=jnp.int32)
        pos = jnp.sum(jnp.where(cls[:, None] == cls_ids[None, :], starts[None, :], 0), axis=1) + rank
        n_valid = ends[-1] // tm_e
        tile_ix = jnp.minimum(jnp.arange(n_pad // tm_e, dtype=jnp.int32), n_valid - 1)
        tile_cls = jnp.sum((tile_ix * tm_e)[:, None] >= ends[None, :], axis=1)
        pair_lo = jnp.asarray([0, 0, 0, 1, 1, 2], jnp.int32)
        pair_hi = jnp.asarray([1, 2, 3, 2, 3, 3], jnp.int32)
        tile_grp = tile_cls // N_PAIRS * EXPERTS_PER_GROUP
        tile_elo = tile_grp + pair_lo[tile_cls % N_PAIRS]
        tile_ehi = tile_grp + pair_hi[tile_cls % N_PAIRS]

        rows = min(s, COMBINE_ROWS)
        pitch = slabs.shape[1] // s
        y_pitch = _slab_pitch(d // LANES)
        sorted_slabs = _dispatch_call(slabs.reshape(n * pitch, LANES), pos, n_pad,
                                      min(n, DISPATCH_ROWS), pitch)
        y_sorted = _moe_call(sorted_slabs, tile_elo, tile_ehi, tile_ix, n_valid.reshape(1),
                             expert_w_gate[l].astype(BF16), expert_w_up[l].astype(BF16),
                             expert_w_down[l].astype(BF16), tm_e, pitch, y_pitch)
        out = _combine_call(y_sorted, pos, x1.reshape(n, d), mod3, final_g.reshape(1, d), s, rows,
                            y_pitch)
        x = out.reshape(b, s, d)
    return x
```

```python
import functools

import jax
import jax.numpy as jnp
import numpy as np
from jax import lax
from jax.experimental import pallas as pl
from jax.experimental.pallas import tpu as pltpu

F32 = jnp.float32
BF16 = jnp.bfloat16

HEAD_DIM = 64
LANES = 128
ROW_TILE = 512
SB_KEY_TILE = 256
RMS_EPS = 1e-6
N_MOD = 6
N_GROUPS = 4
EXPERTS_PER_GROUP = 4
N_PAIRS = 6
N_SPLIT = 3
MOE_TILE = 256
DISPATCH_ROWS = 2048
COMBINE_ROWS = 1024
NEG_BIG = -1e30
LOG2E = 1.4426950408889634
SB_ZERO_BELOW = -104.0
FOX_ZERO_BELOW = -110.0
NORM_SLACK = 1.02
VMEM_LIMIT = 56 * 1024 * 1024


def _dot(a, b):
    return jnp.dot(a, b, preferred_element_type=F32)


def _dot_nt(a, b):
    return lax.dot_general(a, b, (((1,), (1,)), ((), ())), preferred_element_type=F32)


def _split2(x):
    hi = x.astype(BF16)
    lo = (x - hi.astype(F32)).astype(BF16)
    return hi, lo


def _split3(x):
    hi = x.astype(BF16)
    r = x - hi.astype(F32)
    mid = r.astype(BF16)
    lo = (r - mid.astype(F32)).astype(BF16)
    return hi, mid, lo


def _log_sigmoid(z):
    return jnp.minimum(z, 0.0) - jnp.log1p(jnp.exp(-jnp.abs(z)))


def _rms_scale(x):
    return x * lax.rsqrt(jnp.mean(x * x, axis=-1, keepdims=True) + RMS_EPS)


def _mod_kernel(c_ref, w_ref, b_ref, o_ref):
    c = c_ref[...]
    ca = c * jax.nn.sigmoid(c)
    c_hi, c_mid, c_lo = _split3(ca)
    w_hi, w_mid, w_lo = _split3(w_ref[...])
    acc = _dot(c_hi, w_hi) + _dot(c_hi, w_mid) + _dot(c_mid, w_hi)
    acc += _dot(c_mid, w_mid) + _dot(c_hi, w_lo) + _dot(c_lo, w_hi)
    o_ref[...] = acc + b_ref[...]


def _mod_call(c, ada_w, ada_b):
    b, d = c.shape
    n = ada_w.shape[1]
    bp = 16
    tn = n // N_MOD
    c_pad = jnp.pad(c, ((0, bp - b), (0, 0)))
    out = pl.pallas_call(
        _mod_kernel,
        out_shape=jax.ShapeDtypeStruct((bp, n), F32),
        grid=(n // tn,),
        in_specs=[
            pl.BlockSpec((bp, d), lambda j: (0, 0)),
            pl.BlockSpec((d, tn), lambda j: (0, j)),
            pl.BlockSpec((1, tn), lambda j: (0, j)),
        ],
        out_specs=pl.BlockSpec((bp, tn), lambda j: (0, j)),
        compiler_params=pltpu.CompilerParams(
            dimension_semantics=("arbitrary",), vmem_limit_bytes=VMEM_LIMIT),
        name="mod",
    )(c_pad, ada_w, ada_b.reshape(1, n))
    return out[:b]


def _in_kernel(n_heads, chunk, x_ref, mod_ref, g_ref, w_ref, wg_ref, wf_ref, bf_ref,
               sel_ref, proj_ref, gates_ref, stats_ref, carry_ref):
    i = pl.program_id(1)
    tm = x_ref.shape[0]

    @pl.when(i == 0)
    def _():
        carry_ref[...] = jnp.zeros_like(carry_ref)

    x = x_ref[...]
    shift = mod_ref[0:1, :]
    scale = mod_ref[1:2, :]
    h = _rms_scale(x) * g_ref[...] * (1.0 + scale) + shift
    hb = h.astype(BF16)

    lane = lax.broadcasted_iota(jnp.int32, (tm, LANES), 1)
    lf = _log_sigmoid(_dot(hb, wf_ref[...]) + bf_ref[...])
    lf = jnp.where(lane < n_heads, lf, 0.0)
    row = lax.broadcasted_iota(jnp.int32, (tm, tm), 0)
    col = lax.broadcasted_iota(jnp.int32, (tm, tm), 1)
    tri = jnp.where(col <= row, 1.0, 0.0).astype(BF16)
    l_hi, l_mid, l_lo = _split3(lf)
    fcum = _dot(tri, l_hi) + _dot(tri, l_mid) + _dot(tri, l_lo) + carry_ref[0:1, :]
    carry_ref[...] = jnp.broadcast_to(fcum[tm - 1:tm, :], carry_ref.shape)
    fx = jnp.where(lane == n_heads, 1.0, fcum)
    f_hi, f_mid, f_lo = _split3(fx)
    hml = jnp.concatenate([f_hi, f_mid, f_lo], axis=1)

    srow = lax.broadcasted_iota(jnp.int32, (8, LANES), 0)
    slane = lax.broadcasted_iota(jnp.int32, (8, LANES), 1)
    stats = jnp.where(srow == 1, jnp.broadcast_to(fcum[tm - 1:tm, :], (8, LANES)), 0.0)
    k_off = n_heads * HEAD_DIM

    n_w = w_ref.shape[1]
    for c0 in range(0, n_w, chunk):
        acc = _dot(hb, w_ref[:, c0:c0 + chunk])
        proj_ref[:, c0:c0 + chunk] = acc.astype(BF16)
        if k_off <= c0 < 2 * k_off:
            for bl in range(chunk // LANES):
                kb = acc[:, bl * LANES:(bl + 1) * LANES]
                kb2 = kb * kb
                for half in range(2):
                    in_head = (lane >= half * HEAD_DIM) & (lane < (half + 1) * HEAD_DIM)
                    sq = jnp.sum(jnp.where(in_head, kb2, 0.0), axis=1, keepdims=True)
                    head = (c0 - k_off) // HEAD_DIM + 2 * bl + half
                    top = jnp.max(sq, axis=0, keepdims=True)
                    stats = jnp.where((srow == 0) & (slane == head), top, stats)
    stats_ref[...] = stats
    for c0 in range(0, sel_ref.shape[1], chunk):
        proj_ref[:, n_w + c0:n_w + c0 + chunk] = _dot(hml, sel_ref[:, c0:c0 + chunk]).astype(BF16)
    n_gate = wg_ref.shape[1]
    for c0 in range(0, n_gate, chunk):
        gates_ref[:, c0:c0 + chunk] = _dot(hb, wg_ref[:, c0:c0 + chunk])


def _in_call(x, mod3, norm_g, w_all, w_gates, w_f, b_f, sel, n_heads, tm):
    b, s, d = x.shape
    n_w = w_all.shape[1]
    n_ext = sel.shape[1]
    n_proj = n_w + n_ext
    n_gate = w_gates.shape[1]
    chunk = 512
    assert n_heads * HEAD_DIM % chunk == 0 and n_w % chunk == 0 and n_ext % chunk == 0
    const = lambda shape: pl.BlockSpec(shape, lambda bi, i: (0,) * len(shape),
                                       pipeline_mode=pl.Buffered(1))
    return pl.pallas_call(
        functools.partial(_in_kernel, n_heads, chunk),
        out_shape=(jax.ShapeDtypeStruct((b, s, n_proj), BF16),
                   jax.ShapeDtypeStruct((b, s, n_gate), F32),
                   jax.ShapeDtypeStruct((b, s // tm, 8, LANES), F32)),
        grid=(b, s // tm),
        in_specs=[
            pl.BlockSpec((None, tm, d), lambda bi, i: (bi, i, 0)),
            pl.BlockSpec((None, N_MOD, d), lambda bi, i: (bi, 0, 0)),
            const((1, d)),
            const((d, n_w)),
            const((d, n_gate)),
            const((d, LANES)),
            const((1, LANES)),
            const((3 * LANES, n_ext)),
        ],
        out_specs=(pl.BlockSpec((None, tm, n_proj), lambda bi, i: (bi, i, 0)),
                   pl.BlockSpec((None, tm, n_gate), lambda bi, i: (bi, i, 0)),
                   pl.BlockSpec((None, None, 8, LANES), lambda bi, i: (bi, i, 0, 0))),
        scratch_shapes=[pltpu.VMEM((8, LANES), F32)],
        compiler_params=pltpu.CompilerParams(
            dimension_semantics=("arbitrary", "arbitrary"), vmem_limit_bytes=VMEM_LIMIT),
        name="in_proj",
    )(x, mod3, norm_g, w_all, w_gates, w_f, b_f, sel)


def _head_lanes(lane, hh):
    return (lane >= hh * HEAD_DIM) & (lane < (hh + 1) * HEAD_DIM)


def _bias_lanes(lane, hh, lo=0, hi=2 * N_SPLIT):
    return (lane >= hh * 2 * N_SPLIT + lo) & (lane < hh * 2 * N_SPLIT + hi)


def _fox_kernel(tk, n_heads, kn_ref, fe_ref, q_ref, qe_ref, k_ref, ke_ref, v_ref, o_ref):
    bi = pl.program_id(0)
    pair = pl.program_id(1)
    i = pl.program_id(2)
    tq = q_ref.shape[0]
    n_diag = tq // tk
    n_kt = k_ref.shape[0] // tk
    row = lax.broadcasted_iota(jnp.int32, (tq, tk), 0)
    col = lax.broadcasted_iota(jnp.int32, (tq, tk), 1)
    lane = lax.broadcasted_iota(jnp.int32, (tq, LANES), 1)
    q_heads = [jnp.concatenate([jnp.where(_head_lanes(lane, hh), q_ref[...], 0),
                                jnp.where(_bias_lanes(lane, hh), qe_ref[...], 0)], axis=1)
               for hh in range(2)]

    qf = q_ref[...].astype(F32)
    qef = qe_ref[...].astype(F32)
    q_norm, q_f = [], []
    for hh in range(2):
        q_norm.append(jnp.sqrt(jnp.sum(jnp.where(_head_lanes(lane, hh), qf * qf, 0.0),
                                       axis=1, keepdims=True)))
        q_f.append(jnp.sum(jnp.where(_bias_lanes(lane, hh, 0, N_SPLIT), qef, 0.0),
                           axis=1, keepdims=True))

    def may_contribute(j, states):
        worst = []
        for hh, (m, _, _) in enumerate(states):
            idx = (bi * n_kt + j) * n_heads + 2 * pair + hh
            bound = q_norm[hh] * (kn_ref[idx] * NORM_SLACK) + (q_f[hh] - fe_ref[idx]) - m
            worst.append(jnp.max(bound))
        return jnp.maximum(worst[0], worst[1]) >= FOX_ZERO_BELOW

    def head_step(hh, j, state, masked):
        m, l, acc = state
        k0 = pl.multiple_of(j * tk, tk)
        k_all = jnp.concatenate([k_ref[pl.ds(k0, tk), :], ke_ref[pl.ds(k0, tk), :]], axis=1)
        s = _dot_nt(q_heads[hh], k_all)
        if masked:
            s = jnp.where(col + (j - i * n_diag) * tk <= row, s, NEG_BIG)
        m_new = jnp.maximum(m, jnp.max(s, axis=1, keepdims=True))
        alpha = jnp.exp(m - m_new)
        p = jnp.exp(s - m_new)
        l = alpha * l + jnp.sum(p, axis=1, keepdims=True)
        acc = alpha * acc + _dot(p.astype(BF16), v_ref[pl.ds(k0, tk), :])
        return m_new, l, acc

    def pair_step(j, states, masked):
        return tuple(head_step(hh, j, st, masked) for hh, st in enumerate(states))

    init = (jnp.full((tq, 1), NEG_BIG, F32), jnp.zeros((tq, 1), F32),
            jnp.zeros((tq, LANES), F32))
    states = (init, init)
    for dj in range(n_diag):
        states = pair_step(i * n_diag + dj, states, True)

    n_full = i * n_diag

    def cond(carry):
        t, go, _ = carry
        return (t < n_full) & go

    def body(carry):
        t, _, sts = carry
        sts = pair_step(n_full - 1 - t, sts, False)
        return t + 1, may_contribute(jnp.maximum(n_full - 2 - t, 0), sts), sts

    go0 = may_contribute(jnp.maximum(n_full - 1, 0), states)
    _, _, states = lax.while_loop(cond, body, (jnp.int32(0), go0, states))
    outs = [acc / l for _, l, acc in states]
    o_ref[...] = jnp.where(lane < HEAD_DIM, outs[0], outs[1]).astype(o_ref.dtype)


def _fox_call(proj, stats, n_heads, q_off, qe_off, k_off, ke_off, v_off, tq, tk):
    b, s, _ = proj.shape
    assert stats.shape[1] == s // tk
    n_pairs = n_heads // 2
    k_norm = lax.cummax(jnp.sqrt(stats[:, :, 0, :n_heads]), axis=1).reshape(-1)
    f_end = stats[:, :, 1, :n_heads].reshape(-1)
    q_tile = lambda off: pl.BlockSpec(
        (None, tq, LANES), lambda bi, p, i, kn, fe: (bi, i, off // LANES + p))
    kv_all = lambda off: pl.BlockSpec(
        (None, s, LANES), lambda bi, p, i, kn, fe: (bi, 0, off // LANES + p))
    return pl.pallas_call(
        functools.partial(_fox_kernel, tk, n_heads),
        out_shape=jax.ShapeDtypeStruct((b, s, n_heads * HEAD_DIM), BF16),
        grid_spec=pltpu.PrefetchScalarGridSpec(
            num_scalar_prefetch=2, grid=(b, n_pairs, s // tq),
            in_specs=[q_tile(q_off), q_tile(qe_off), kv_all(k_off), kv_all(ke_off),
                      kv_all(v_off)],
            out_specs=pl.BlockSpec((None, tq, LANES), lambda bi, p, i, kn, fe: (bi, i, p))),
        compiler_params=pltpu.CompilerParams(
            dimension_semantics=("arbitrary", "arbitrary", "arbitrary"),
            vmem_limit_bytes=VMEM_LIMIT),
        name="fox_attn",
    )(k_norm, f_end, proj, proj, proj, proj, proj)


def _sb_kernel(tk, tr, q_ref, k_ref, v_ref, o_ref):
    i = pl.program_id(2)
    tq = q_ref.shape[0]
    n_chunks = tq // tr
    n_diag = tr // tk
    row = lax.broadcasted_iota(jnp.int32, (tr, tk), 0)
    col = lax.broadcasted_iota(jnp.int32, (tr, tk), 1)
    lane = lax.broadcasted_iota(jnp.int32, (tq, LANES), 1)
    klane = lax.broadcasted_iota(jnp.int32, (tk, LANES), 1)
    ur = lax.broadcasted_iota(jnp.int32, (2 * tk, tk), 0)
    uc = lax.broadcasted_iota(jnp.int32, (2 * tk, tk), 1)
    upper2 = jnp.where((ur & (tk - 1)) > uc, 1.0, 0.0).astype(BF16)
    chains = [(hh, r) for r in range(n_chunks) for hh in range(2)]

    def chain_step(hh, r, j, state, dj=None):
        masked = dj is not None
        c, acc = state
        q = q_ref[r * tr:(r + 1) * tr, :]
        k0 = pl.multiple_of(j * tk, tk)
        k = jnp.where(_head_lanes(klane, hh), k_ref[pl.ds(k0, tk), :], 0)
        z = _dot_nt(q, k)
        t = jnp.exp2(jnp.abs(z) * (-LOG2E))
        ls = jnp.minimum(z, 0.0) - jnp.log(1.0 + t)
        lom = ls - z
        if masked:
            strict = col + dj * tk < row
            lom = jnp.where(strict, lom, 0.0)
        hi = pltpu.bitcast(pltpu.bitcast(lom, jnp.uint32) & jnp.uint32(0xFFFF0000), F32)
        parts = jnp.concatenate([hi.astype(BF16), (lom - hi).astype(BF16)], axis=1)
        within = _dot(parts, upper2)
        a = jnp.exp2((ls + within + c) * LOG2E)
        if masked:
            a = jnp.where(strict, a, 0.0)
        acc = acc + _dot(a.astype(BF16), v_ref[pl.ds(k0, tk), :])
        c = c + within[:, 0:1] + lom[:, 0:1]
        return c, acc

    base = i * (tq // tk)
    states = []
    for hh, r in chains:
        st = (jnp.zeros((tr, 1), F32), jnp.zeros((tr, LANES), F32))
        for dj in reversed(range(n_diag)):
            st = chain_step(hh, r, base + r * n_diag + dj, st, dj)
        for j in reversed(range(r * n_diag)):
            st = chain_step(hh, r, base + j, st)
        states.append(st)

    def c_max(sts):
        return functools.reduce(jnp.maximum, [jnp.max(st[0]) for st in sts])

    def cond(carry):
        t, cm, _ = carry
        return (t < base) & (cm >= SB_ZERO_BELOW)

    def body(carry):
        t, _, sts = carry
        j = base - 1 - t
        sts = tuple(chain_step(hh, r, j, st) for (hh, r), st in zip(chains, sts))
        return t + 1, c_max(sts), sts

    states = tuple(states)
    _, _, states = lax.while_loop(cond, body, (jnp.int32(0), c_max(states), states))
    acc = {ch: st[1] for ch, st in zip(chains, states)}
    o0 = jnp.concatenate([acc[(0, r)] for r in range(n_chunks)], axis=0)
    o1 = jnp.concatenate([acc[(1, r)] for r in range(n_chunks)], axis=0)
    o_ref[...] = jnp.where(lane < HEAD_DIM, o0, o1).astype(o_ref.dtype)


def _sb_call(proj, n_heads, q_off, k_off, v_off, tq, tk, tr):
    b, s, _ = proj.shape
    n_pairs = n_heads // 2
    return pl.pallas_call(
        functools.partial(_sb_kernel, tk, tr),
        out_shape=jax.ShapeDtypeStruct((b, s, n_heads * HEAD_DIM), BF16),
        grid=(b, n_pairs, s // tq),
        in_specs=[
            pl.BlockSpec((None, tq, LANES), lambda bi, p, i: (bi, i, q_off // LANES + p)),
            pl.BlockSpec((None, s, LANES), lambda bi, p, i: (bi, 0, k_off // LANES + p)),
            pl.BlockSpec((None, s, LANES), lambda bi, p, i: (bi, 0, v_off // LANES + p)),
        ],
        out_specs=pl.BlockSpec((None, tq, LANES), lambda bi, p, i: (bi, i, p)),
        compiler_params=pltpu.CompilerParams(
            dimension_semantics=("arbitrary", "arbitrary", "arbitrary"),
            vmem_limit_bytes=VMEM_LIMIT),
        name="sb_attn",
    )(proj, proj, proj)


def _route(logits):
    tm = logits.shape[0]
    lane = lax.broadcasted_iota(jnp.int32, (tm, LANES), 1)
    big = jnp.int32(LANES)
    g_mask = lane < N_GROUPS
    lg = jnp.where(g_mask, logits, -jnp.inf)
    g_max = jnp.max(lg, axis=1, keepdims=True)
    g_exp = jnp.exp(lg - g_max)
    g_prob = g_exp / jnp.sum(g_exp, axis=1, keepdims=True)
    g_top = jnp.max(g_prob, axis=1, keepdims=True)
    g_idx = jnp.min(jnp.where(g_mask & (g_prob == g_top), lane, big), axis=1, keepdims=True)
    lo = N_GROUPS + EXPERTS_PER_GROUP * g_idx
    e_mask = (lane >= lo) & (lane < lo + EXPERTS_PER_GROUP)
    le = jnp.where(e_mask, logits, -jnp.inf)
    e_max = jnp.max(le, axis=1, keepdims=True)
    e_exp = jnp.exp(le - e_max)
    e_prob = e_exp / jnp.sum(e_exp, axis=1, keepdims=True)
    p1 = jnp.max(e_prob, axis=1, keepdims=True)
    i1 = jnp.min(jnp.where(e_mask & (e_prob == p1), lane, big), axis=1, keepdims=True)
    rest = e_mask & (lane != i1)
    p2 = jnp.max(jnp.where(rest, e_prob, -1.0), axis=1, keepdims=True)
    i2 = jnp.min(jnp.where(rest & (e_prob == p2), lane, big), axis=1, keepdims=True)
    tot = p1 + p2
    w1 = p1 / tot * g_top
    w2 = p2 / tot * g_top
    first_lower = i1 < i2
    e_lo = jnp.minimum(i1, i2) - lo
    e_hi = jnp.maximum(i1, i2) - lo
    pair = jnp.where(e_lo == 0, 0, jnp.where(e_lo == 1, 3, 5)) + (e_hi - e_lo - 1)
    cls = g_idx * N_PAIRS + pair
    return jnp.where(first_lower, w1, w2), jnp.where(first_lower, w2, w1), cls


def _post_kernel(of_ref, os_ref, gates_ref, x_ref, mod_ref, wbf_ref, wbs_ref, wo_ref, g2_ref,
                 wr_hi_ref, wr_lo_ref, rb_ref, x1_ref, slab_ref, meta_ref, cnt_ref, carry_ref):
    first = (pl.program_id(0) == 0) & (pl.program_id(1) == 0)

    @pl.when(first)
    def _():
        carry_ref[...] = jnp.zeros_like(carry_ref)

    tm = x_ref.shape[0]
    d = x_ref.shape[1]
    bf = _dot(of_ref[...], wbf_ref[...])
    bs = _dot(os_ref[...], wbs_ref[...])
    merged = jax.nn.sigmoid(gates_ref[:, :d]) * bf + jax.nn.sigmoid(gates_ref[:, d:]) * bs
    y = _dot(merged.astype(BF16), wo_ref[...])
    x1 = x_ref[...] + mod_ref[2:3, :] * y
    x1_ref[...] = x1
    h2 = _rms_scale(x1) * g2_ref[...] * (1.0 + mod_ref[4:5, :]) + mod_ref[3:4, :]
    h_hi, h_lo = _split2(h2)
    w_hi = wr_hi_ref[...]
    logits = _dot(h_hi, w_hi) + _dot(h_hi, wr_lo_ref[...]) + _dot(h_lo, w_hi) + rb_ref[...]
    w_first, w_second, cls = _route(logits)

    lane = lax.broadcasted_iota(jnp.int32, (tm, LANES), 1)
    onehot = lane == cls
    oh = jnp.where(onehot, 1.0, 0.0)
    row = lax.broadcasted_iota(jnp.int32, (tm, tm), 0)
    col = lax.broadcasted_iota(jnp.int32, (tm, tm), 1)
    earlier = jnp.where(col < row, 1.0, 0.0).astype(BF16)
    before = _dot(earlier, oh.astype(BF16)) + carry_ref[0:1, :]
    rank = jnp.sum(jnp.where(onehot, before, 0.0), axis=1, keepdims=True)
    counts = carry_ref[0:1, :] + jnp.sum(oh, axis=0, keepdims=True)
    carry_ref[...] = jnp.broadcast_to(counts, carry_ref.shape)
    cnt_ref[...] = jnp.broadcast_to(counts, cnt_ref.shape)

    meta = (jnp.where(lane == 0, w_first, 0.0) + jnp.where(lane == 1, w_second, 0.0)
            + jnp.where(lane == 2, cls.astype(F32), 0.0) + jnp.where(lane == 3, rank, 0.0))
    meta_ref[...] = meta
    h_rows = d // LANES
    pitch = slab_ref.shape[0] // tm
    for s in range(pitch):
        if s < h_rows:
            val = h2[:, s * LANES:(s + 1) * LANES]
        else:
            val = meta if s == h_rows else jnp.zeros((tm, LANES), F32)
        slab_ref[pl.ds(s, tm, stride=pitch), :] = val


def _post_call(o_fox, o_sb, gates, x, mod3, w_bf, w_bs, w_o, norm_g, wr_hi, wr_lo, rb, tm):
    b, s, d = x.shape
    wdt = o_fox.shape[2]
    pitch = _slab_pitch(d // LANES + 1)
    const = lambda shape: pl.BlockSpec(shape, lambda bi, i: (0,) * len(shape),
                                       pipeline_mode=pl.Buffered(1))
    tile = lambda w: pl.BlockSpec((None, tm, w), lambda bi, i: (bi, i, 0))
    return pl.pallas_call(
        _post_kernel,
        out_shape=(jax.ShapeDtypeStruct((b, s, d), F32),
                   jax.ShapeDtypeStruct((b, s * pitch, LANES), F32),
                   jax.ShapeDtypeStruct((b, s, LANES), F32),
                   jax.ShapeDtypeStruct((8, LANES), F32)),
        grid=(b, s // tm),
        in_specs=[
            tile(wdt), tile(wdt), tile(2 * d), tile(d),
            pl.BlockSpec((None, N_MOD, d), lambda bi, i: (bi, 0, 0)),
            const((wdt, d)), const((wdt, d)), const((d, d)), const((1, d)),
            const((d, LANES)), const((d, LANES)), const((1, LANES)),
        ],
        out_specs=(tile(d),
                   pl.BlockSpec((None, tm * pitch, LANES), lambda bi, i: (bi, i, 0)),
                   tile(LANES),
                   pl.BlockSpec((8, LANES), lambda bi, i: (0, 0))),
        scratch_shapes=[pltpu.VMEM((8, LANES), F32)],
        compiler_params=pltpu.CompilerParams(
            dimension_semantics=("arbitrary", "arbitrary"), vmem_limit_bytes=VMEM_LIMIT),
        name="post",
    )(o_fox, o_sb, gates, x, mod3, w_bf, w_bs, w_o, norm_g, wr_hi, wr_lo, rb)


def _slab_pitch(rows):
    pitch = -(-rows // 4) * 4
    return pitch if (pitch // 4) % 2 else pitch + 4


def _dispatch_kernel(pitch, tile, n_cls, pos_ref, pad_lo_ref, pad_n_ref, nv_ref, src_ref,
                     dst_ref, zeros, sem, zsem):
    rows = src_ref.shape[0] // pitch
    t = pl.program_id(0)
    base = t * rows
    n_tiles = dst_ref.shape[0] // (tile * pitch)
    zero_slab = zeros.at[pl.ds(0, pitch), :]

    def pad_copy(p):
        return pltpu.make_async_copy(zero_slab, dst_ref.at[pl.ds(p * pitch, pitch), :], zsem)

    def tile_copy(j):
        return pltpu.make_async_copy(zeros, dst_ref.at[pl.ds(j * tile * pitch, tile * pitch), :],
                                     zsem)

    def for_each_zero_copy(act):
        for c in range(n_cls):
            def pad_body(k, carry, c=c):
                act(pad_copy(pad_lo_ref[c] + k))
                return carry

            lax.fori_loop(0, pad_n_ref[c], pad_body, 0)

        def tile_body(j, carry):
            act(tile_copy(j))
            return carry

        lax.fori_loop(nv_ref[0], n_tiles, tile_body, 0)

    @pl.when(t == 0)
    def _():
        zeros[...] = jnp.zeros_like(zeros)
        for_each_zero_copy(lambda cp: cp.start())

    def issue(r, carry):
        pltpu.make_async_copy(src_ref.at[pl.ds(r * pitch, pitch), :],
                              dst_ref.at[pl.ds(pos_ref[base + r] * pitch, pitch), :], sem).start()
        return carry

    lax.fori_loop(0, rows, issue, 0, unroll=8)
    pltpu.make_async_copy(src_ref, dst_ref.at[pl.ds(0, rows * pitch), :], sem).wait()

    @pl.when(t == 0)
    def _():
        for_each_zero_copy(lambda cp: cp.wait())


def _dispatch_call(slabs, pos, pad_lo, pad_n, n_valid, n_pad, rows, pitch, tile):
    n = slabs.shape[0] // pitch
    any_spec = pl.BlockSpec(memory_space=pl.ANY)
    return pl.pallas_call(
        functools.partial(_dispatch_kernel, pitch, tile, pad_lo.shape[0]),
        out_shape=jax.ShapeDtypeStruct((n_pad * pitch, LANES), slabs.dtype),
        grid_spec=pltpu.PrefetchScalarGridSpec(
            num_scalar_prefetch=4, grid=(n // rows,),
            in_specs=[pl.BlockSpec((rows * pitch, LANES), lambda t, *_: (t, 0))],
            out_specs=any_spec,
            scratch_shapes=[pltpu.VMEM((tile * pitch, LANES), slabs.dtype),
                            pltpu.SemaphoreType.DMA(()), pltpu.SemaphoreType.DMA(())]),
        compiler_params=pltpu.CompilerParams(
            dimension_semantics=("arbitrary",), has_side_effects=True,
            vmem_limit_bytes=VMEM_LIMIT),
        name="dispatch",
    )(pos, pad_lo, pad_n, n_valid, slabs)


def _moe_kernel(tm, elo_ref, ehi_ref, tix_ref, nv_ref, slab_ref, wg0_ref, wu0_ref, wd0_ref,
                wg1_ref, wu1_ref, wd1_ref, y_ref):
    y_rows = wd0_ref.shape[1] // LANES
    pitch = slab_ref.shape[0] // tm
    y_pitch = y_ref.shape[0] // tm
    valid = pl.program_id(0) < nv_ref[0]

    @pl.when(jnp.logical_not(valid))
    def _():
        y_ref[...] = jnp.zeros_like(y_ref)

    @pl.when(valid)
    def _():
        h = jnp.concatenate([slab_ref[pl.ds(s, tm, stride=pitch), :] for s in range(y_rows)],
                            axis=1).astype(BF16)
        meta = slab_ref[pl.ds(y_rows, tm, stride=pitch), :]

        def expert(wg_ref, wu_ref, wd_ref):
            g = _dot(h, wg_ref[...])
            u = _dot(h, wu_ref[...])
            act = (g * jax.nn.sigmoid(g) * u).astype(BF16)
            return _dot(act, wd_ref[...])

        y = (meta[:, 0:1] * expert(wg0_ref, wu0_ref, wd0_ref)
             + meta[:, 1:2] * expert(wg1_ref, wu1_ref, wd1_ref))
        for s in range(y_pitch):
            val = y[:, s * LANES:(s + 1) * LANES] if s < y_rows else jnp.zeros((tm, LANES), F32)
            y_ref[pl.ds(s, tm, stride=y_pitch), :] = val


def _moe_call(slabs, tile_elo, tile_ehi, tile_ix, n_valid, w_gate, w_up, w_down, tm, pitch,
              y_pitch):
    n_pad = slabs.shape[0] // pitch
    _, d, f = w_gate.shape
    lo = lambda t, elo, ehi, tix, nv: (elo[t], 0, 0)
    hi = lambda t, elo, ehi, tix, nv: (ehi[t], 0, 0)
    return pl.pallas_call(
        functools.partial(_moe_kernel, tm),
        out_shape=jax.ShapeDtypeStruct((n_pad * y_pitch, LANES), F32),
        grid_spec=pltpu.PrefetchScalarGridSpec(
            num_scalar_prefetch=4, grid=(n_pad // tm,),
            in_specs=[
                pl.BlockSpec((tm * pitch, LANES), lambda t, elo, ehi, tix, nv: (tix[t], 0)),
                pl.BlockSpec((None, d, f), lo), pl.BlockSpec((None, d, f), lo),
                pl.BlockSpec((None, f, d), lo),
                pl.BlockSpec((None, d, f), hi), pl.BlockSpec((None, d, f), hi),
                pl.BlockSpec((None, f, d), hi),
            ],
            out_specs=pl.BlockSpec((tm * y_pitch, LANES),
                                   lambda t, elo, ehi, tix, nv: (t, 0))),
        compiler_params=pltpu.CompilerParams(
            dimension_semantics=("arbitrary",), vmem_limit_bytes=VMEM_LIMIT),
        name="moe",
    )(tile_elo, tile_ehi, tile_ix, n_valid, slabs, w_gate, w_up, w_down, w_gate, w_up, w_down)


def _combine_kernel(pos_ref, ys_ref, x1_ref, mod_ref, fg_ref, o_ref, ybuf, sem):
    rows, d = x1_ref.shape
    y_rows = d // LANES
    y_pitch = ybuf.shape[1] // rows
    t = pl.program_id(0)

    def issue(step, slot):
        base = step * rows

        def body(r, carry):
            pltpu.make_async_copy(ys_ref.at[pl.ds(pos_ref[base + r] * y_pitch, y_rows), :],
                                  ybuf.at[slot, pl.ds(r * y_pitch, y_rows), :],
                                  sem.at[slot]).start()
            return carry

        lax.fori_loop(0, rows, body, 0, unroll=8)

    @pl.when(t == 0)
    def _():
        issue(0, 0)

    @pl.when(t + 1 < pl.num_programs(0))
    def _():
        issue(t + 1, (t + 1) % 2)

    slot = t % 2
    pltpu.make_async_copy(ys_ref.at[pl.ds(0, rows * y_rows), :],
                          ybuf.at[slot, pl.ds(0, rows * y_rows), :], sem.at[slot]).wait()
    ycur = ybuf.at[slot]
    y = jnp.concatenate([ycur[pl.ds(s, rows, stride=y_pitch), :] for s in range(y_rows)], axis=1)
    x2 = x1_ref[...] + mod_ref[5:6, :] * y
    o_ref[...] = _rms_scale(x2) * fg_ref[...]


def _combine_call(y_sorted, pos, x1, mod3, final_g, seq, rows, y_pitch):
    n, d = x1.shape
    per_seq = seq // rows
    return pl.pallas_call(
        _combine_kernel,
        out_shape=jax.ShapeDtypeStruct((n, d), F32),
        grid_spec=pltpu.PrefetchScalarGridSpec(
            num_scalar_prefetch=1, grid=(n // rows,),
            in_specs=[
                pl.BlockSpec(memory_space=pl.ANY),
                pl.BlockSpec((rows, d), lambda t, pos: (t, 0)),
                pl.BlockSpec((None, N_MOD, d), lambda t, pos: (t // per_seq, 0, 0)),
                pl.BlockSpec((1, d), lambda t, pos: (0, 0)),
            ],
            out_specs=pl.BlockSpec((rows, d), lambda t, pos: (t, 0)),
            scratch_shapes=[pltpu.VMEM((2, rows * y_pitch, LANES), F32),
                            pltpu.SemaphoreType.DMA((2,))]),
        compiler_params=pltpu.CompilerParams(
            dimension_semantics=("arbitrary",), vmem_limit_bytes=VMEM_LIMIT),
        name="combine",
    )(pos, y_sorted, x1, mod3, final_g)


def _pad_lanes(a):
    return jnp.pad(a.astype(F32), ((0, 0), (0, LANES - a.shape[1])))


def _bias_selectors(n_heads):
    rows = N_SPLIT * LANES
    n_pairs = n_heads // 2
    sel_q = np.zeros((rows, n_pairs * LANES), np.float32)
    sel_k = np.zeros((rows, n_pairs * LANES), np.float32)
    ones_row = n_heads
    for h in range(n_heads):
        base = (h // 2) * LANES + (h % 2) * 2 * N_SPLIT
        for j in range(N_SPLIT):
            sel_q[j * LANES + h, base + j] = 1.0
            sel_q[ones_row, base + N_SPLIT + j] = 1.0
            sel_k[ones_row, base + j] = 1.0
            sel_k[j * LANES + h, base + N_SPLIT + j] = -1.0
    return sel_q, sel_k


def kernel(x, c, ada_w, ada_b, norm1_g, w_in, b_forget, w_branch_fox, w_branch_sb, w_out,
           norm2_g, router_group_w, router_group_b, router_expert_w, router_expert_b,
           expert_w_gate, expert_w_up, expert_w_down, final_g):
    b, s, d = x.shape
    depth = ada_w.shape[0]
    hf = b_forget.shape[1]
    fw = hf * HEAD_DIM
    sw = w_branch_sb.shape[1]
    hs = sw // HEAD_DIM
    scale = HEAD_DIM ** -0.5
    n_exp = expert_w_gate.shape[1]

    assert depth == 1, "the final RMSNorm is fused into the single layer's MoE kernel"
    for l in range(depth):
        mod = _mod_call(c, ada_w[l], ada_b[l])
        mod3 = mod.reshape(b, N_MOD, d)

        wl = w_in[l]
        o = 0
        wq_a = wl[:, o:o + fw]; o += fw
        wk_a = wl[:, o:o + fw]; o += fw
        wv_a = wl[:, o:o + fw]; o += fw
        wf = wl[:, o:o + hf]; o += hf
        wq_b = wl[:, o:o + sw]; o += sw
        wk_b = wl[:, o:o + sw]; o += sw
        wv_b = wl[:, o:o + sw]; o += sw
        w_gates = wl[:, o:].astype(BF16)
        w_all = jnp.concatenate([wq_a * scale, wk_a, wq_b * scale, wk_b, wv_a, wv_b],
                                axis=1).astype(BF16)
        sel_q, sel_k = _bias_selectors(hf)
        sel = jnp.concatenate([sel_q, sel_k], axis=1).astype(BF16)
        w_f = _pad_lanes(wf).astype(BF16)
        b_f = _pad_lanes(b_forget[l].reshape(1, hf))

        q_a_off = 0
        k_a_off = fw
        q_b_off = 2 * fw
        k_b_off = q_b_off + sw
        v_a_off = k_b_off + sw
        v_b_off = v_a_off + fw
        qe_off = v_b_off + sw
        ke_off = qe_off + hf // 2 * LANES

        tq = min(s, ROW_TILE)
        proj, gates, kstats = _in_call(x, mod3, norm1_g[l].reshape(1, d), w_all, w_gates, w_f, b_f,
                                       sel, hf, tm=tq)
        o_fox = _fox_call(proj, kstats, hf, q_a_off, qe_off, k_a_off, ke_off, v_a_off, tq, tq)
        o_sb = _sb_call(proj, hs, q_b_off, k_b_off, v_b_off, tq, min(s, SB_KEY_TILE), tq)

        wr = _pad_lanes(jnp.concatenate([router_group_w[l], router_expert_w[l]], axis=1))
        wr_hi = wr.astype(BF16)
        wr_lo = (wr - wr_hi.astype(F32)).astype(BF16)
        rb = _pad_lanes(jnp.concatenate([router_group_b[l], router_expert_b[l]]).reshape(1, -1))

        x1, slabs, meta, counts = _post_call(
            o_fox, o_sb, gates, x, mod3, w_branch_fox[l].astype(BF16), w_branch_sb[l].astype(BF16),
            w_out[l].astype(BF16), norm2_g[l].reshape(1, d), wr_hi, wr_lo, rb, tm=tq)

        n = b * s
        tm_e = MOE_TILE
        n_cls = N_GROUPS * N_PAIRS
        n_pad = n + n_cls * tm_e
        cls = meta[..., 2].reshape(n).astype(jnp.int32)
        rank = meta[..., 3].reshape(n).astype(jnp.int32)
        cnt = counts[0, :n_cls].astype(jnp.int32)
        padded = (cnt + tm_e - 1) // tm_e * tm_e
        ends = jnp.cumsum(padded)
        starts = ends - padded
        cls_ids = jnp.arange(n_cls, dtype=jnp.int32)
        pos = jnp.sum(jnp.where(cls[:, None] == cls_ids[None, :], starts[None, :], 0), axis=1) + rank
        n_valid = ends[-1] // tm_e
        tile_ix = jnp.minimum(jnp.arange(n_pad // tm_e, dtype=jnp.int32), n_valid - 1)
        tile_cls = jnp.sum((tile_ix * tm_e)[:, None] >= ends[None, :], axis=1)
        pair_lo = jnp.asarray([0, 0, 0, 1, 1, 2], jnp.int32)
        pair_hi = jnp.asarray([1, 2, 3, 2, 3, 3], jnp.int32)
        tile_grp = tile_cls // N_PAIRS * EXPERTS_PER_GROUP
        tile_elo = tile_grp + pair_lo[tile_cls % N_PAIRS]
        tile_ehi = tile_grp + pair_hi[tile_cls % N_PAIRS]

        rows = min(s, COMBINE_ROWS)
        pitch = slabs.shape[1] // s
        y_pitch = _slab_pitch(d // LANES)
        sorted_slabs = _dispatch_call(slabs.reshape(n * pitch, LANES), pos, starts + cnt,
                                      padded - cnt, n_valid.reshape(1), n_pad,
                                      min(n, DISPATCH_ROWS), pitch, tm_e)
        y_sorted = _moe_call(sorted_slabs, tile_elo, tile_ehi, tile_ix, n_valid.reshape(1),
                             expert_w_gate[l].astype(BF16), expert_w_up[l].astype(BF16),
                             expert_w_down[l].astype(BF16), tm_e, pitch, y_pitch)
        out = _combine_call(y_sorted, pos, x1.reshape(n, d), mod3, final_g.reshape(1, d), s, rows,
                            y_pitch)
        x = out.reshape(b, s, d)
    return x
```

```python
import functools

import jax
import jax.numpy as jnp
import numpy as np
from jax import lax
from jax.experimental import pallas as pl
from jax.experimental.pallas import tpu as pltpu

F32 = jnp.float32
BF16 = jnp.bfloat16

HEAD_DIM = 64
LANES = 128
ROW_TILE = 512
SB_KEY_TILE = 256
RMS_EPS = 1e-6
N_MOD = 6
N_GROUPS = 4
EXPERTS_PER_GROUP = 4
N_PAIRS = 6
N_SPLIT = 3
MOE_TILE = 256
DISPATCH_ROWS = 2048
COMBINE_ROWS = 1024
NEG_BIG = -1e30
LOG2E = 1.4426950408889634
SB_ZERO_BELOW = -104.0
FOX_ZERO_BELOW = -110.0
NORM_SLACK = 1.02
VMEM_LIMIT = 56 * 1024 * 1024


def _dot(a, b):
    return jnp.dot(a, b, preferred_element_type=F32)


def _dot_nt(a, b):
    return lax.dot_general(a, b, (((1,), (1,)), ((), ())), preferred_element_type=F32)


def _split2(x):
    hi = x.astype(BF16)
    lo = (x - hi.astype(F32)).astype(BF16)
    return hi, lo


def _split3(x):
    hi = x.astype(BF16)
    r = x - hi.astype(F32)
    mid = r.astype(BF16)
    lo = (r - mid.astype(F32)).astype(BF16)
    return hi, mid, lo


def _log_sigmoid(z):
    return jnp.minimum(z, 0.0) - jnp.log1p(jnp.exp(-jnp.abs(z)))


def _rms_scale(x):
    return x * lax.rsqrt(jnp.mean(x * x, axis=-1, keepdims=True) + RMS_EPS)


def _mod_kernel(c_ref, w_ref, b_ref, o_ref):
    c = c_ref[...]
    ca = c * jax.nn.sigmoid(c)
    c_hi, c_mid, c_lo = _split3(ca)
    w_hi, w_mid, w_lo = _split3(w_ref[...])
    acc = _dot(c_hi, w_hi) + _dot(c_hi, w_mid) + _dot(c_mid, w_hi)
    acc += _dot(c_mid, w_mid) + _dot(c_hi, w_lo) + _dot(c_lo, w_hi)
    o_ref[...] = acc + b_ref[...]


def _mod_call(c, ada_w, ada_b):
    b, d = c.shape
    n = ada_w.shape[1]
    bp = 16
    tn = n // N_MOD
    c_pad = jnp.pad(c, ((0, bp - b), (0, 0)))
    out = pl.pallas_call(
        _mod_kernel,
        out_shape=jax.ShapeDtypeStruct((bp, n), F32),
        grid=(n // tn,),
        in_specs=[
            pl.BlockSpec((bp, d), lambda j: (0, 0)),
            pl.BlockSpec((d, tn), lambda j: (0, j)),
            pl.BlockSpec((1, tn), lambda j: (0, j)),
        ],
        out_specs=pl.BlockSpec((bp, tn), lambda j: (0, j)),
        compiler_params=pltpu.CompilerParams(
            dimension_semantics=("arbitrary",), vmem_limit_bytes=VMEM_LIMIT),
        name="mod",
    )(c_pad, ada_w, ada_b.reshape(1, n))
    return out[:b]


def _in_kernel(n_heads, chunk, x_ref, mod_ref, g_ref, w_ref, wg_ref, wf_ref, bf_ref,
               sel_ref, proj_ref, gates_ref, stats_ref, carry_ref):
    i = pl.program_id(1)
    tm = x_ref.shape[0]

    @pl.when(i == 0)
    def _():
        carry_ref[...] = jnp.zeros_like(carry_ref)

    x = x_ref[...]
    shift = mod_ref[0:1, :]
    scale = mod_ref[1:2, :]
    h = _rms_scale(x) * g_ref[...] * (1.0 + scale) + shift
    hb = h.astype(BF16)

    lane = lax.broadcasted_iota(jnp.int32, (tm, LANES), 1)
    lf = _log_sigmoid(_dot(hb, wf_ref[...]) + bf_ref[...])
    lf = jnp.where(lane < n_heads, lf, 0.0)
    row = lax.broadcasted_iota(jnp.int32, (tm, tm), 0)
    col = lax.broadcasted_iota(jnp.int32, (tm, tm), 1)
    tri = jnp.where(col <= row, 1.0, 0.0).astype(BF16)
    l_hi, l_mid, l_lo = _split3(lf)
    fcum = _dot(tri, l_hi) + _dot(tri, l_mid) + _dot(tri, l_lo) + carry_ref[0:1, :]
    carry_ref[...] = jnp.broadcast_to(fcum[tm - 1:tm, :], carry_ref.shape)
    fx = jnp.where(lane == n_heads, 1.0, fcum)
    f_hi, f_mid, f_lo = _split3(fx)
    hml = jnp.concatenate([f_hi, f_mid, f_lo], axis=1)

    srow = lax.broadcasted_iota(jnp.int32, (8, LANES), 0)
    slane = lax.broadcasted_iota(jnp.int32, (8, LANES), 1)
    stats = jnp.where(srow == 1, jnp.broadcast_to(fcum[tm - 1:tm, :], (8, LANES)), 0.0)
    k_off = n_heads * HEAD_DIM

    n_w = w_ref.shape[1]
    for c0 in range(0, n_w, chunk):
        acc = _dot(hb, w_ref[:, c0:c0 + chunk])
        proj_ref[:, c0:c0 + chunk] = acc.astype(BF16)
        if k_off <= c0 < 2 * k_off:
            for bl in range(chunk // LANES):
                kb = acc[:, bl * LANES:(bl + 1) * LANES]
                kb2 = kb * kb
                for half in range(2):
                    in_head = (lane >= half * HEAD_DIM) & (lane < (half + 1) * HEAD_DIM)
                    sq = jnp.sum(jnp.where(in_head, kb2, 0.0), axis=1, keepdims=True)
                    head = (c0 - k_off) // HEAD_DIM + 2 * bl + half
                    top = jnp.max(sq, axis=0, keepdims=True)
                    stats = jnp.where((srow == 0) & (slane == head), top, stats)
    stats_ref[...] = stats
    for c0 in range(0, sel_ref.shape[1], chunk):
        proj_ref[:, n_w + c0:n_w + c0 + chunk] = _dot(hml, sel_ref[:, c0:c0 + chunk]).astype(BF16)
    n_gate = wg_ref.shape[1]
    for c0 in range(0, n_gate, chunk):
        gates_ref[:, c0:c0 + chunk] = _dot(hb, wg_ref[:, c0:c0 + chunk])


def _in_call(x, mod3, norm_g, w_all, w_gates, w_f, b_f, sel, n_heads, tm):
    b, s, d = x.shape
    n_w = w_all.shape[1]
    n_ext = sel.shape[1]
    n_proj = n_w + n_ext
    n_gate = w_gates.shape[1]
    chunk = 512
    assert n_heads * HEAD_DIM % chunk == 0 and n_w % chunk == 0 and n_ext % chunk == 0
    const = lambda shape: pl.BlockSpec(shape, lambda bi, i: (0,) * len(shape),
                                       pipeline_mode=pl.Buffered(1))
    return pl.pallas_call(
        functools.partial(_in_kernel, n_heads, chunk),
        out_shape=(jax.ShapeDtypeStruct((b, s, n_proj), BF16),
                   jax.ShapeDtypeStruct((b, s, n_gate), F32),
                   jax.ShapeDtypeStruct((b, s // tm, 8, LANES), F32)),
        grid=(b, s // tm),
        in_specs=[
            pl.BlockSpec((None, tm, d), lambda bi, i: (bi, i, 0)),
            pl.BlockSpec((None, N_MOD, d), lambda bi, i: (bi, 0, 0)),
            const((1, d)),
            const((d, n_w)),
            const((d, n_gate)),
            const((d, LANES)),
            const((1, LANES)),
            const((3 * LANES, n_ext)),
        ],
        out_specs=(pl.BlockSpec((None, tm, n_proj), lambda bi, i: (bi, i, 0)),
                   pl.BlockSpec((None, tm, n_gate), lambda bi, i: (bi, i, 0)),
                   pl.BlockSpec((None, None, 8, LANES), lambda bi, i: (bi, i, 0, 0))),
        scratch_shapes=[pltpu.VMEM((8, LANES), F32)],
        compiler_params=pltpu.CompilerParams(
            dimension_semantics=("arbitrary", "arbitrary"), vmem_limit_bytes=VMEM_LIMIT),
        name="in_proj",
    )(x, mod3, norm_g, w_all, w_gates, w_f, b_f, sel)


def _head_lanes(lane, hh):
    return (lane >= hh * HEAD_DIM) & (lane < (hh + 1) * HEAD_DIM)


def _bias_lanes(lane, hh, lo=0, hi=2 * N_SPLIT):
    return (lane >= hh * 2 * N_SPLIT + lo) & (lane < hh * 2 * N_SPLIT + hi)


def _fox_kernel(tk, n_heads, kn_ref, fe_ref, q_ref, qe_ref, k_ref, ke_ref, v_ref, o_ref):
    bi = pl.program_id(0)
    pair = pl.program_id(1)
    i = pl.program_id(2)
    tq = q_ref.shape[0]
    n_diag = tq // tk
    n_kt = k_ref.shape[0] // tk
    row = lax.broadcasted_iota(jnp.int32, (tq, tk), 0)
    col = lax.broadcasted_iota(jnp.int32, (tq, tk), 1)
    lane = lax.broadcasted_iota(jnp.int32, (tq, LANES), 1)
    q_heads = [jnp.concatenate([jnp.where(_head_lanes(lane, hh), q_ref[...], 0),
                                jnp.where(_bias_lanes(lane, hh), qe_ref[...], 0)], axis=1)
               for hh in range(2)]

    qf = q_ref[...].astype(F32)
    qef = qe_ref[...].astype(F32)
    q_norm, q_f = [], []
    for hh in range(2):
        q_norm.append(jnp.sqrt(jnp.sum(jnp.where(_head_lanes(lane, hh), qf * qf, 0.0),
                                       axis=1, keepdims=True)))
        q_f.append(jnp.sum(jnp.where(_bias_lanes(lane, hh, 0, N_SPLIT), qef, 0.0),
                           axis=1, keepdims=True))

    def may_contribute(j, states):
        worst = []
        for hh, (m, _, _) in enumerate(states):
            idx = (bi * n_kt + j) * n_heads + 2 * pair + hh
            bound = q_norm[hh] * (kn_ref[idx] * NORM_SLACK) + (q_f[hh] - fe_ref[idx]) - m
            worst.append(jnp.max(bound))
        return jnp.maximum(worst[0], worst[1]) >= FOX_ZERO_BELOW

    def head_step(hh, j, state, masked):
        m, l, acc = state
        k0 = pl.multiple_of(j * tk, tk)
        k_all = jnp.concatenate([k_ref[pl.ds(k0, tk), :], ke_ref[pl.ds(k0, tk), :]], axis=1)
        s = _dot_nt(q_heads[hh], k_all)
        if masked:
            s = jnp.where(col + (j - i * n_diag) * tk <= row, s, NEG_BIG)
        m_new = jnp.maximum(m, jnp.max(s, axis=1, keepdims=True))
        alpha = jnp.exp(m - m_new)
        p = jnp.exp(s - m_new)
        l = alpha * l + jnp.sum(p, axis=1, keepdims=True)
        acc = alpha * acc + _dot(p.astype(BF16), v_ref[pl.ds(k0, tk), :])
        return m_new, l, acc

    def pair_step(j, states, masked):
        return tuple(head_step(hh, j, st, masked) for hh, st in enumerate(states))

    init = (jnp.full((tq, 1), NEG_BIG, F32), jnp.zeros((tq, 1), F32),
            jnp.zeros((tq, LANES), F32))
    states = (init, init)
    for dj in range(n_diag):
        states = pair_step(i * n_diag + dj, states, True)

    n_full = i * n_diag

    def cond(carry):
        t, go, _ = carry
        return (t < n_full) & go

    def body(carry):
        t, _, sts = carry
        sts = pair_step(n_full - 1 - t, sts, False)
        return t + 1, may_contribute(jnp.maximum(n_full - 2 - t, 0), sts), sts

    go0 = may_contribute(jnp.maximum(n_full - 1, 0), states)
    _, _, states = lax.while_loop(cond, body, (jnp.int32(0), go0, states))
    outs = [acc / l for _, l, acc in states]
    o_ref[...] = jnp.where(lane < HEAD_DIM, outs[0], outs[1]).astype(o_ref.dtype)


def _fox_call(proj, stats, n_heads, q_off, qe_off, k_off, ke_off, v_off, tq, tk):
    b, s, _ = proj.shape
    assert stats.shape[1] == s // tk
    n_pairs = n_heads // 2
    k_norm = lax.cummax(jnp.sqrt(stats[:, :, 0, :n_heads]), axis=1).reshape(-1)
    f_end = stats[:, :, 1, :n_heads].reshape(-1)
    q_tile = lambda off: pl.BlockSpec(
        (None, tq, LANES), lambda bi, p, i, kn, fe: (bi, i, off // LANES + p))
    kv_all = lambda off: pl.BlockSpec(
        (None, s, LANES), lambda bi, p, i, kn, fe: (bi, 0, off // LANES + p))
    return pl.pallas_call(
        functools.partial(_fox_kernel, tk, n_heads),
        out_shape=jax.ShapeDtypeStruct((b, s, n_heads * HEAD_DIM), BF16),
        grid_spec=pltpu.PrefetchScalarGridSpec(
            num_scalar_prefetch=2, grid=(b, n_pairs, s // tq),
            in_specs=[q_tile(q_off), q_tile(qe_off), kv_all(k_off), kv_all(ke_off),
                      kv_all(v_off)],
            out_specs=pl.BlockSpec((None, tq, LANES), lambda bi, p, i, kn, fe: (bi, i, p))),
        compiler_params=pltpu.CompilerParams(
            dimension_semantics=("arbitrary", "arbitrary", "arbitrary"),
            vmem_limit_bytes=VMEM_LIMIT),
        name="fox_attn",
    )(k_norm, f_end, proj, proj, proj, proj, proj)


def _sb_kernel(tk, tr, q_ref, k_ref, v_ref, o_ref):
    i = pl.program_id(2)
    tq = q_ref.shape[0]
    n_chunks = tq // tr
    n_diag = tr // tk
    row = lax.broadcasted_iota(jnp.int32, (tr, tk), 0)
    col = lax.broadcasted_iota(jnp.int32, (tr, tk), 1)
    lane = lax.broadcasted_iota(jnp.int32, (tq, LANES), 1)
    klane = lax.broadcasted_iota(jnp.int32, (tk, LANES), 1)
    ur = lax.broadcasted_iota(jnp.int32, (2 * tk, tk), 0)
    uc = lax.broadcasted_iota(jnp.int32, (2 * tk, tk), 1)
    upper2 = jnp.where((ur & (tk - 1)) > uc, 1.0, 0.0).astype(BF16)
    chains = [(hh, r) for r in range(n_chunks) for hh in range(2)]

    def chain_step(hh, r, j, state, dj=None):
        masked = dj is not None
        c, acc = state
        q = q_ref[r * tr:(r + 1) * tr, :]
        k0 = pl.multiple_of(j * tk, tk)
        k = jnp.where(_head_lanes(klane, hh), k_ref[pl.ds(k0, tk), :], 0)
        z = _dot_nt(q, k)
        t = jnp.exp2(jnp.abs(z) * (-LOG2E))
        ls = jnp.minimum(z, 0.0) - jnp.log(1.0 + t)
        lom = ls - z
        if masked:
            strict = col + dj * tk < row
            lom = jnp.where(strict, lom, 0.0)
        hi = pltpu.bitcast(pltpu.bitcast(lom, jnp.uint32) & jnp.uint32(0xFFFF0000), F32)
        parts = jnp.concatenate([hi.astype(BF16), (lom - hi).astype(BF16)], axis=1)
        within = _dot(parts, upper2)
        a = jnp.exp2((ls + within + c) * LOG2E)
        if masked:
            a = jnp.where(strict, a, 0.0)
        acc = acc + _dot(a.astype(BF16), v_ref[pl.ds(k0, tk), :])
        c = c + within[:, 0:1] + lom[:, 0:1]
        return c, acc

    base = i * (tq // tk)
    states = []
    for hh, r in chains:
        st = (jnp.zeros((tr, 1), F32), jnp.zeros((tr, LANES), F32))
        for dj in reversed(range(n_diag)):
            st = chain_step(hh, r, base + r * n_diag + dj, st, dj)
        for j in reversed(range(r * n_diag)):
            st = chain_step(hh, r, base + j, st)
        states.append(st)

    def c_max(sts):
        return functools.reduce(jnp.maximum, [jnp.max(st[0]) for st in sts])

    def cond(carry):
        t, cm, _ = carry
        return (t < base) & (cm >= SB_ZERO_BELOW)

    def body(carry):
        t, _, sts = carry
        j = base - 1 - t
        sts = tuple(chain_step(hh, r, j, st) for (hh, r), st in zip(chains, sts))
        return t + 1, c_max(sts), sts

    states = tuple(states)
    _, _, states = lax.while_loop(cond, body, (jnp.int32(0), c_max(states), states))
    acc = {ch: st[1] for ch, st in zip(chains, states)}
    o0 = jnp.concatenate([acc[(0, r)] for r in range(n_chunks)], axis=0)
    o1 = jnp.concatenate([acc[(1, r)] for r in range(n_chunks)], axis=0)
    o_ref[...] = jnp.where(lane < HEAD_DIM, o0, o1).astype(o_ref.dtype)


def _sb_call(proj, n_heads, q_off, k_off, v_off, tq, tk, tr):
    b, s, _ = proj.shape
    n_pairs = n_heads // 2
    return pl.pallas_call(
        functools.partial(_sb_kernel, tk, tr),
        out_shape=jax.ShapeDtypeStruct((b, s, n_heads * HEAD_DIM), BF16),
        grid=(b, n_pairs, s // tq),
        in_specs=[
            pl.BlockSpec((None, tq, LANES), lambda bi, p, i: (bi, i, q_off // LANES + p)),
            pl.BlockSpec((None, s, LANES), lambda bi, p, i: (bi, 0, k_off // LANES + p)),
            pl.BlockSpec((None, s, LANES), lambda bi, p, i: (bi, 0, v_off // LANES + p)),
        ],
        out_specs=pl.BlockSpec((None, tq, LANES), lambda bi, p, i: (bi, i, p)),
        compiler_params=pltpu.CompilerParams(
            dimension_semantics=("arbitrary", "arbitrary", "arbitrary"),
            vmem_limit_bytes=VMEM_LIMIT),
        name="sb_attn",
    )(proj, proj, proj)


def _route(logits):
    tm = logits.shape[0]
    lane = lax.broadcasted_iota(jnp.int32, (tm, LANES), 1)
    big = jnp.int32(LANES)
    g_mask = lane < N_GROUPS
    lg = jnp.where(g_mask, logits, -jnp.inf)
    g_max = jnp.max(lg, axis=1, keepdims=True)
    g_exp = jnp.exp(lg - g_max)
    g_prob = g_exp / jnp.sum(g_exp, axis=1, keepdims=True)
    g_top = jnp.max(g_prob, axis=1, keepdims=True)
    g_idx = jnp.min(jnp.where(g_mask & (g_prob == g_top), lane, big), axis=1, keepdims=True)
    lo = N_GROUPS + EXPERTS_PER_GROUP * g_idx
    e_mask = (lane >= lo) & (lane < lo + EXPERTS_PER_GROUP)
    le = jnp.where(e_mask, logits, -jnp.inf)
    e_max = jnp.max(le, axis=1, keepdims=True)
    e_exp = jnp.exp(le - e_max)
    e_prob = e_exp / jnp.sum(e_exp, axis=1, keepdims=True)
    p1 = jnp.max(e_prob, axis=1, keepdims=True)
    i1 = jnp.min(jnp.where(e_mask & (e_prob == p1), lane, big), axis=1, keepdims=True)
    rest = e_mask & (lane != i1)
    p2 = jnp.max(jnp.where(rest, e_prob, -1.0), axis=1, keepdims=True)
    i2 = jnp.min(jnp.where(rest & (e_prob == p2), lane, big), axis=1, keepdims=True)
    tot = p1 + p2
    w1 = p1 / tot * g_top
    w2 = p2 / tot * g_top
    first_lower = i1 < i2
    e_lo = jnp.minimum(i1, i2) - lo
    e_hi = jnp.maximum(i1, i2) - lo
    pair = jnp.where(e_lo == 0, 0, jnp.where(e_lo == 1, 3, 5)) + (e_hi - e_lo - 1)
    cls = g_idx * N_PAIRS + pair
    return jnp.where(first_lower, w1, w2), jnp.where(first_lower, w2, w1), cls


def _post_kernel(of_ref, os_ref, gates_ref, x_ref, mod_ref, wbf_ref, wbs_ref, wo_ref, g2_ref,
                 wr_hi_ref, wr_lo_ref, rb_ref, x1_ref, slab_ref, meta_ref, cnt_ref, carry_ref):
    first = (pl.program_id(0) == 0) & (pl.program_id(1) == 0)

    @pl.when(first)
    def _():
        carry_ref[...] = jnp.zeros_like(carry_ref)

    tm = x_ref.shape[0]
    d = x_ref.shape[1]
    bf = _dot(of_ref[...], wbf_ref[...])
    bs = _dot(os_ref[...], wbs_ref[...])
    merged = jax.nn.sigmoid(gates_ref[:, :d]) * bf + jax.nn.sigmoid(gates_ref[:, d:]) * bs
    y = _dot(merged.astype(BF16), wo_ref[...])
    x1 = x_ref[...] + mod_ref[2:3, :] * y
    x1_ref[...] = x1
    h2 = _rms_scale(x1) * g2_ref[...] * (1.0 + mod_ref[4:5, :]) + mod_ref[3:4, :]
    h_hi, h_lo = _split2(h2)
    w_hi = wr_hi_ref[...]
    logits = _dot(h_hi, w_hi) + _dot(h_hi, wr_lo_ref[...]) + _dot(h_lo, w_hi) + rb_ref[...]
    w_first, w_second, cls = _route(logits)

    lane = lax.broadcasted_iota(jnp.int32, (tm, LANES), 1)
    onehot = lane == cls
    oh = jnp.where(onehot, 1.0, 0.0)
    row = lax.broadcasted_iota(jnp.int32, (tm, tm), 0)
    col = lax.broadcasted_iota(jnp.int32, (tm, tm), 1)
    earlier = jnp.where(col < row, 1.0, 0.0).astype(BF16)
    before = _dot(earlier, oh.astype(BF16)) + carry_ref[0:1, :]
    rank = jnp.sum(jnp.where(onehot, before, 0.0), axis=1, keepdims=True)
    counts = carry_ref[0:1, :] + jnp.sum(oh, axis=0, keepdims=True)
    carry_ref[...] = jnp.broadcast_to(counts, carry_ref.shape)
    cnt_ref[...] = jnp.broadcast_to(counts, cnt_ref.shape)

    meta = (jnp.where(lane == 0, w_first, 0.0) + jnp.where(lane == 1, w_second, 0.0)
            + jnp.where(lane == 2, cls.astype(F32), 0.0) + jnp.where(lane == 3, rank, 0.0))
    meta_ref[...] = meta
    h_rows = d // LANES
    pitch = slab_ref.shape[0] // tm
    for s in range(pitch):
        if s < h_rows:
            val = h2[:, s * LANES:(s + 1) * LANES]
        else:
            val = meta if s == h_rows else jnp.zeros((tm, LANES), F32)
        slab_ref[pl.ds(s, tm, stride=pitch), :] = val


def _post_call(o_fox, o_sb, gates, x, mod3, w_bf, w_bs, w_o, norm_g, wr_hi, wr_lo, rb, tm):
    b, s, d = x.shape
    wdt = o_fox.shape[2]
    pitch = _slab_pitch(d // LANES + 1)
    const = lambda shape: pl.BlockSpec(shape, lambda bi, i: (0,) * len(shape),
                                       pipeline_mode=pl.Buffered(1))
    tile = lambda w: pl.BlockSpec((None, tm, w), lambda bi, i: (bi, i, 0))
    return pl.pallas_call(
        _post_kernel,
        out_shape=(jax.ShapeDtypeStruct((b, s, d), F32),
                   jax.ShapeDtypeStruct((b, s * pitch, LANES), F32),
                   jax.ShapeDtypeStruct((b, s, LANES), F32),
                   jax.ShapeDtypeStruct((8, LANES), F32)),
        grid=(b, s // tm),
        in_specs=[
            tile(wdt), tile(wdt), tile(2 * d), tile(d),
            pl.BlockSpec((None, N_MOD, d), lambda bi, i: (bi, 0, 0)),
            const((wdt, d)), const((wdt, d)), const((d, d)), const((1, d)),
            const((d, LANES)), const((d, LANES)), const((1, LANES)),
        ],
        out_specs=(tile(d),
                   pl.BlockSpec((None, tm * pitch, LANES), lambda bi, i: (bi, i, 0)),
                   tile(LANES),
                   pl.BlockSpec((8, LANES), lambda bi, i: (0, 0))),
        scratch_shapes=[pltpu.VMEM((8, LANES), F32)],
        compiler_params=pltpu.CompilerParams(
            dimension_semantics=("arbitrary", "arbitrary"), vmem_limit_bytes=VMEM_LIMIT),
        name="post",
    )(o_fox, o_sb, gates, x, mod3, w_bf, w_bs, w_o, norm_g, wr_hi, wr_lo, rb)


def _slab_pitch(rows):
    pitch = -(-rows // 4) * 4
    return pitch if (pitch // 4) % 2 else pitch + 4


def _dispatch_kernel(pitch, tile, n_cls, pos_ref, pad_lo_ref, pad_n_ref, nv_ref, src_ref,
                     dst_ref, zeros, sem, zsem):
    rows = src_ref.shape[0] // pitch
    t = pl.program_id(0)
    base = t * rows
    n_tiles = dst_ref.shape[0] // (tile * pitch)
    zero_slab = zeros.at[pl.ds(0, pitch), :]

    def pad_copy(p):
        return pltpu.make_async_copy(zero_slab, dst_ref.at[pl.ds(p * pitch, pitch), :], zsem)

    def tile_copy(j):
        return pltpu.make_async_copy(zeros, dst_ref.at[pl.ds(j * tile * pitch, tile * pitch), :],
                                     zsem)

    def for_each_zero_copy(act):
        for c in range(n_cls):
            def pad_body(k, carry, c=c):
                act(pad_copy(pad_lo_ref[c] + k))
                return carry

            lax.fori_loop(0, pad_n_ref[c], pad_body, 0)

        def tile_body(j, carry):
            act(tile_copy(j))
            return carry

        lax.fori_loop(nv_ref[0], n_tiles, tile_body, 0)

    @pl.when(t == 0)
    def _():
        zeros[...] = jnp.zeros_like(zeros)
        for_each_zero_copy(lambda cp: cp.start())

    def issue(r, carry):
        pltpu.make_async_copy(src_ref.at[pl.ds(r * pitch, pitch), :],
                              dst_ref.at[pl.ds(pos_ref[base + r] * pitch, pitch), :], sem).start()
        return carry

    lax.fori_loop(0, rows, issue, 0, unroll=8)
    pltpu.make_async_copy(src_ref, dst_ref.at[pl.ds(0, rows * pitch), :], sem).wait()

    @pl.when(t == pl.num_programs(0) - 1)
    def _():
        for_each_zero_copy(lambda cp: cp.wait())


def _dispatch_call(slabs, pos, pad_lo, pad_n, n_valid, n_pad, rows, pitch, tile):
    n = slabs.shape[0] // pitch
    any_spec = pl.BlockSpec(memory_space=pl.ANY)
    return pl.pallas_call(
        functools.partial(_dispatch_kernel, pitch, tile, pad_lo.shape[0]),
        out_shape=jax.ShapeDtypeStruct((n_pad * pitch, LANES), slabs.dtype),
        grid_spec=pltpu.PrefetchScalarGridSpec(
            num_scalar_prefetch=4, grid=(n // rows,),
            in_specs=[pl.BlockSpec((rows * pitch, LANES), lambda t, *_: (t, 0))],
            out_specs=any_spec,
            scratch_shapes=[pltpu.VMEM((tile * pitch, LANES), slabs.dtype),
                            pltpu.SemaphoreType.DMA(()), pltpu.SemaphoreType.DMA(())]),
        compiler_params=pltpu.CompilerParams(
            dimension_semantics=("arbitrary",), has_side_effects=True,
            vmem_limit_bytes=VMEM_LIMIT),
        name="dispatch",
    )(pos, pad_lo, pad_n, n_valid, slabs)


def _moe_kernel(tm, elo_ref, ehi_ref, tix_ref, nv_ref, slab_ref, wg0_ref, wu0_ref, wd0_ref,
                wg1_ref, wu1_ref, wd1_ref, y_ref):
    y_rows = wd0_ref.shape[1] // LANES
    pitch = slab_ref.shape[0] // tm
    y_pitch = y_ref.shape[0] // tm
    valid = pl.program_id(0) < nv_ref[0]

    @pl.when(jnp.logical_not(valid))
    def _():
        y_ref[...] = jnp.zeros_like(y_ref)

    @pl.when(valid)
    def _():
        h = jnp.concatenate([slab_ref[pl.ds(s, tm, stride=pitch), :] for s in range(y_rows)],
                            axis=1).astype(BF16)
        meta = slab_ref[pl.ds(y_rows, tm, stride=pitch), :]

        def expert(wg_ref, wu_ref, wd_ref):
            g = _dot(h, wg_ref[...])
            u = _dot(h, wu_ref[...])
            act = (g * jax.nn.sigmoid(g) * u).astype(BF16)
            return _dot(act, wd_ref[...])

        y = (meta[:, 0:1] * expert(wg0_ref, wu0_ref, wd0_ref)
             + meta[:, 1:2] * expert(wg1_ref, wu1_ref, wd1_ref))
        for s in range(y_pitch):
            val = y[:, s * LANES:(s + 1) * LANES] if s < y_rows else jnp.zeros((tm, LANES), F32)
            y_ref[pl.ds(s, tm, stride=y_pitch), :] = val


def _moe_call(slabs, tile_elo, tile_ehi, tile_ix, n_valid, w_gate, w_up, w_down, tm, pitch,
              y_pitch):
    n_pad = slabs.shape[0] // pitch
    _, d, f = w_gate.shape
    lo = lambda t, elo, ehi, tix, nv: (elo[t], 0, 0)
    hi = lambda t, elo, ehi, tix, nv: (ehi[t], 0, 0)
    return pl.pallas_call(
        functools.partial(_moe_kernel, tm),
        out_shape=jax.ShapeDtypeStruct((n_pad * y_pitch, LANES), F32),
        grid_spec=pltpu.PrefetchScalarGridSpec(
            num_scalar_prefetch=4, grid=(n_pad // tm,),
            in_specs=[
                pl.BlockSpec((tm * pitch, LANES), lambda t, elo, ehi, tix, nv: (tix[t], 0)),
                pl.BlockSpec((None, d, f), lo), pl.BlockSpec((None, d, f), lo),
                pl.BlockSpec((None, f, d), lo),
                pl.BlockSpec((None, d, f), hi), pl.BlockSpec((None, d, f), hi),
                pl.BlockSpec((None, f, d), hi),
            ],
            out_specs=pl.BlockSpec((tm * y_pitch, LANES),
                                   lambda t, elo, ehi, tix, nv: (t, 0))),
        compiler_params=pltpu.CompilerParams(
            dimension_semantics=("arbitrary",), vmem_limit_bytes=VMEM_LIMIT),
        name="moe",
    )(tile_elo, tile_ehi, tile_ix, n_valid, slabs, w_gate, w_up, w_down, w_gate, w_up, w_down)


def _combine_kernel(pos_ref, ys_ref, x1_ref, mod_ref, fg_ref, o_ref, ybuf, sem):
    rows, d = x1_ref.shape
    y_rows = d // LANES
    y_pitch = ybuf.shape[1] // rows
    t = pl.program_id(0)

    def issue(step, slot):
        base = step * rows

        def body(r, carry):
            pltpu.make_async_copy(ys_ref.at[pl.ds(pos_ref[base + r] * y_pitch, y_rows), :],
                                  ybuf.at[slot, pl.ds(r * y_pitch, y_rows), :],
                                  sem.at[slot]).start()
            return carry

        lax.fori_loop(0, rows, body, 0, unroll=8)

    @pl.when(t == 0)
    def _():
        issue(0, 0)

    @pl.when(t + 1 < pl.num_programs(0))
    def _():
        issue(t + 1, (t + 1) % 2)

    slot = t % 2
    pltpu.make_async_copy(ys_ref.at[pl.ds(0, rows * y_rows), :],
                          ybuf.at[slot, pl.ds(0, rows * y_rows), :], sem.at[slot]).wait()
    ycur = ybuf.at[slot]
    y = jnp.concatenate([ycur[pl.ds(s, rows, stride=y_pitch), :] for s in range(y_rows)], axis=1)
    x2 = x1_ref[...] + mod_ref[5:6, :] * y
    o_ref[...] = _rms_scale(x2) * fg_ref[...]


def _combine_call(y_sorted, pos, x1, mod3, final_g, seq, rows, y_pitch):
    n, d = x1.shape
    per_seq = seq // rows
    return pl.pallas_call(
        _combine_kernel,
        out_shape=jax.ShapeDtypeStruct((n, d), F32),
        grid_spec=pltpu.PrefetchScalarGridSpec(
            num_scalar_prefetch=1, grid=(n // rows,),
            in_specs=[
                pl.BlockSpec(memory_space=pl.ANY),
                pl.BlockSpec((rows, d), lambda t, pos: (t, 0)),
                pl.BlockSpec((None, N_MOD, d), lambda t, pos: (t // per_seq, 0, 0)),
                pl.BlockSpec((1, d), lambda t, pos: (0, 0)),
            ],
            out_specs=pl.BlockSpec((rows, d), lambda t, pos: (t, 0)),
            scratch_shapes=[pltpu.VMEM((2, rows * y_pitch, LANES), F32),
                            pltpu.SemaphoreType.DMA((2,))]),
        compiler_params=pltpu.CompilerParams(
            dimension_semantics=("arbitrary",), vmem_limit_bytes=VMEM_LIMIT),
        name="combine",
    )(pos, y_sorted, x1, mod3, final_g)


def _pad_lanes(a):
    return jnp.pad(a.astype(F32), ((0, 0), (0, LANES - a.shape[1])))


def _bias_selectors(n_heads):
    rows = N_SPLIT * LANES
    n_pairs = n_heads // 2
    sel_q = np.zeros((rows, n_pairs * LANES), np.float32)
    sel_k = np.zeros((rows, n_pairs * LANES), np.float32)
    ones_row = n_heads
    for h in range(n_heads):
        base = (h // 2) * LANES + (h % 2) * 2 * N_SPLIT
        for j in range(N_SPLIT):
            sel_q[j * LANES + h, base + j] = 1.0
            sel_q[ones_row, base + N_SPLIT + j] = 1.0
            sel_k[ones_row, base + j] = 1.0
            sel_k[j * LANES + h, base + N_SPLIT + j] = -1.0
    return sel_q, sel_k


def kernel(x, c, ada_w, ada_b, norm1_g, w_in, b_forget, w_branch_fox, w_branch_sb, w_out,
           norm2_g, router_group_w, router_group_b, router_expert_w, router_expert_b,
           expert_w_gate, expert_w_up, expert_w_down, final_g):
    b, s, d = x.shape
    depth = ada_w.shape[0]
    hf = b_forget.shape[1]
    fw = hf * HEAD_DIM
    sw = w_branch_sb.shape[1]
    hs = sw // HEAD_DIM
    scale = HEAD_DIM ** -0.5
    n_exp = expert_w_gate.shape[1]

    assert depth == 1, "the final RMSNorm is fused into the single layer's MoE kernel"
    for l in range(depth):
        mod = _mod_call(c, ada_w[l], ada_b[l])
        mod3 = mod.reshape(b, N_MOD, d)

        wl = w_in[l]
        o = 0
        wq_a = wl[:, o:o + fw]; o += fw
        wk_a = wl[:, o:o + fw]; o += fw
        wv_a = wl[:, o:o + fw]; o += fw
        wf = wl[:, o:o + hf]; o += hf
        wq_b = wl[:, o:o + sw]; o += sw
        wk_b = wl[:, o:o + sw]; o += sw
        wv_b = wl[:, o:o + sw]; o += sw
        w_gates = wl[:, o:].astype(BF16)
        w_all = jnp.concatenate([wq_a * scale, wk_a, wq_b * scale, wk_b, wv_a, wv_b],
                                axis=1).astype(BF16)
        sel_q, sel_k = _bias_selectors(hf)
        sel = jnp.concatenate([sel_q, sel_k], axis=1).astype(BF16)
        w_f = _pad_lanes(wf).astype(BF16)
        b_f = _pad_lanes(b_forget[l].reshape(1, hf))

        q_a_off = 0
        k_a_off = fw
        q_b_off = 2 * fw
        k_b_off = q_b_off + sw
        v_a_off = k_b_off + sw
        v_b_off = v_a_off + fw
        qe_off = v_b_off + sw
        ke_off = qe_off + hf // 2 * LANES

        tq = min(s, ROW_TILE)
        proj, gates, kstats = _in_call(x, mod3, norm1_g[l].reshape(1, d), w_all, w_gates, w_f, b_f,
                                       sel, hf, tm=tq)
        o_fox = _fox_call(proj, kstats, hf, q_a_off, qe_off, k_a_off, ke_off, v_a_off, tq, tq)
        o_sb = _sb_call(proj, hs, q_b_off, k_b_off, v_b_off, tq, min(s, SB_KEY_TILE), tq)

        wr = _pad_lanes(jnp.concatenate([router_group_w[l], router_expert_w[l]], axis=1))
        wr_hi = wr.astype(BF16)
        wr_lo = (wr - wr_hi.astype(F32)).astype(BF16)
        rb = _pad_lanes(jnp.concatenate([router_group_b[l], router_expert_b[l]]).reshape(1, -1))

        x1, slabs, meta, counts = _post_call(
            o_fox, o_sb, gates, x, mod3, w_branch_fox[l].astype(BF16), w_branch_sb[l].astype(BF16),
            w_out[l].astype(BF16), norm2_g[l].reshape(1, d), wr_hi, wr_lo, rb, tm=tq)

        n = b * s
        tm_e = MOE_TILE
        n_cls = N_GROUPS * N_PAIRS
        n_pad = n + n_cls * tm_e
        cls = meta[..., 2].reshape(n).astype(jnp.int32)
        rank = meta[..., 3].reshape(n).astype(jnp.int32)
        cnt = counts[0, :n_cls].astype(jnp.int32)
        padded = (cnt + tm_e - 1) // tm_e * tm_e
        ends = jnp.cumsum(padded)
        starts = ends - padded
        cls_ids = jnp.arange(n_cls, dtype=jnp.int32)
        pos = jnp.sum(jnp.where(cls[:, None] == cls_ids[None, :], starts[None, :], 0), axis=1) + rank
        n_valid = ends[-1] // tm_e
        tile_ix = jnp.minimum(jnp.arange(n_pad // tm_e, dtype=jnp.int32), n_valid - 1)
        tile_cls = jnp.sum((tile_ix * tm_e)[:, None] >= ends[None, :], axis=1)
        pair_lo = jnp.asarray([0, 0, 0, 1, 1, 2], jnp.int32)
        pair_hi = jnp.asarray([1, 2, 3, 2, 3, 3], jnp.int32)
        tile_grp = tile_cls // N_PAIRS * EXPERTS_PER_GROUP
        tile_elo = tile_grp + pair_lo[tile_cls % N_PAIRS]
        tile_ehi = tile_grp + pair_hi[tile_cls % N_PAIRS]

        rows = min(s, COMBINE_ROWS)
        pitch = slabs.shape[1] // s
        y_pitch = _slab_pitch(d // LANES)
        sorted_slabs = _dispatch_call(slabs.reshape(n * pitch, LANES), pos, starts + cnt,
                                      padded - cnt, n_valid.reshape(1), n_pad,
                                      min(n, DISPATCH_ROWS), pitch, tm_e)
        y_sorted = _moe_call(sorted_slabs, tile_elo, tile_ehi, tile_ix, n_valid.reshape(1),
                             expert_w_gate[l].astype(BF16), expert_w_up[l].astype(BF16),
                             expert_w_down[l].astype(BF16), tm_e, pitch, y_pitch)
        out = _combine_call(y_sorted, pos, x1.reshape(n, d), mod3, final_g.reshape(1, d), s, rows,
                            y_pitch)
        x = out.reshape(b, s, d)
    return x
```

```python
import functools

import jax
import jax.numpy as jnp
import numpy as np
from jax import lax
from jax.experimental import pallas as pl
from jax.experimental.pallas import tpu as pltpu

F32 = jnp.float32
BF16 = jnp.bfloat16

HEAD_DIM = 64
LANES = 128
ROW_TILE = 512
SB_KEY_TILE = 256
RMS_EPS = 1e-6
N_MOD = 6
N_GROUPS = 4
EXPERTS_PER_GROUP = 4
N_PAIRS = 6
N_SPLIT = 3
MOE_TILE = 256
DISPATCH_ROWS = 2048
COMBINE_ROWS = 1024
DMA_GROUP = 8
NEG_BIG = -1e30
LOG2E = 1.4426950408889634
SB_ZERO_BELOW = -104.0
FOX_ZERO_BELOW = -110.0
NORM_SLACK = 1.02
VMEM_LIMIT = 56 * 1024 * 1024


def _dot(a, b):
    return jnp.dot(a, b, preferred_element_type=F32)


def _dot_nt(a, b):
    return lax.dot_general(a, b, (((1,), (1,)), ((), ())), preferred_element_type=F32)


def _split2(x):
    hi = x.astype(BF16)
    lo = (x - hi.astype(F32)).astype(BF16)
    return hi, lo


def _split3(x):
    hi = x.astype(BF16)
    r = x - hi.astype(F32)
    mid = r.astype(BF16)
    lo = (r - mid.astype(F32)).astype(BF16)
    return hi, mid, lo


def _log_sigmoid(z):
    return jnp.minimum(z, 0.0) - jnp.log1p(jnp.exp(-jnp.abs(z)))


def _rms_scale(x):
    return x * lax.rsqrt(jnp.mean(x * x, axis=-1, keepdims=True) + RMS_EPS)


def _mod_kernel(c_ref, w_ref, b_ref, o_ref):
    c = c_ref[...]
    ca = c * jax.nn.sigmoid(c)
    c_hi, c_mid, c_lo = _split3(ca)
    w_hi, w_mid, w_lo = _split3(w_ref[...])
    acc = _dot(c_hi, w_hi) + _dot(c_hi, w_mid) + _dot(c_mid, w_hi)
    acc += _dot(c_mid, w_mid) + _dot(c_hi, w_lo) + _dot(c_lo, w_hi)
    o_ref[...] = acc + b_ref[...]


def _mod_call(c, ada_w, ada_b):
    b, d = c.shape
    n = ada_w.shape[1]
    bp = 16
    tn = n // N_MOD
    c_pad = jnp.pad(c, ((0, bp - b), (0, 0)))
    out = pl.pallas_call(
        _mod_kernel,
        out_shape=jax.ShapeDtypeStruct((bp, n), F32),
        grid=(n // tn,),
        in_specs=[
            pl.BlockSpec((bp, d), lambda j: (0, 0)),
            pl.BlockSpec((d, tn), lambda j: (0, j)),
            pl.BlockSpec((1, tn), lambda j: (0, j)),
        ],
        out_specs=pl.BlockSpec((bp, tn), lambda j: (0, j)),
        compiler_params=pltpu.CompilerParams(
            dimension_semantics=("arbitrary",), vmem_limit_bytes=VMEM_LIMIT),
        name="mod",
    )(c_pad, ada_w, ada_b.reshape(1, n))
    return out[:b]


def _in_kernel(n_heads, chunk, x_ref, mod_ref, g_ref, w_ref, wg_ref, wf_ref, bf_ref,
               sel_ref, proj_ref, gates_ref, stats_ref, carry_ref):
    i = pl.program_id(1)
    tm = x_ref.shape[0]

    @pl.when(i == 0)
    def _():
        carry_ref[...] = jnp.zeros_like(carry_ref)

    x = x_ref[...]
    shift = mod_ref[0:1, :]
    scale = mod_ref[1:2, :]
    h = _rms_scale(x) * g_ref[...] * (1.0 + scale) + shift
    hb = h.astype(BF16)

    lane = lax.broadcasted_iota(jnp.int32, (tm, LANES), 1)
    lf = _log_sigmoid(_dot(hb, wf_ref[...]) + bf_ref[...])
    lf = jnp.where(lane < n_heads, lf, 0.0)
    row = lax.broadcasted_iota(jnp.int32, (tm, tm), 0)
    col = lax.broadcasted_iota(jnp.int32, (tm, tm), 1)
    tri = jnp.where(col <= row, 1.0, 0.0).astype(BF16)
    l_hi, l_mid, l_lo = _split3(lf)
    fcum = _dot(tri, l_hi) + _dot(tri, l_mid) + _dot(tri, l_lo) + carry_ref[0:1, :]
    carry_ref[...] = jnp.broadcast_to(fcum[tm - 1:tm, :], carry_ref.shape)
    fx = jnp.where(lane == n_heads, 1.0, fcum)
    f_hi, f_mid, f_lo = _split3(fx)
    hml = jnp.concatenate([f_hi, f_mid, f_lo], axis=1)

    srow = lax.broadcasted_iota(jnp.int32, (8, LANES), 0)
    slane = lax.broadcasted_iota(jnp.int32, (8, LANES), 1)
    stats = jnp.where(srow == 1, jnp.broadcast_to(fcum[tm - 1:tm, :], (8, LANES)), 0.0)
    k_off = n_heads * HEAD_DIM

    n_w = w_ref.shape[1]
    for c0 in range(0, n_w, chunk):
        acc = _dot(hb, w_ref[:, c0:c0 + chunk])
        proj_ref[:, c0:c0 + chunk] = acc.astype(BF16)
        if k_off <= c0 < 2 * k_off:
            for bl in range(chunk // LANES):
                kb = acc[:, bl * LANES:(bl + 1) * LANES]
                kb2 = kb * kb
                for half in range(2):
                    in_head = (lane >= half * HEAD_DIM) & (lane < (half + 1) * HEAD_DIM)
                    sq = jnp.sum(jnp.where(in_head, kb2, 0.0), axis=1, keepdims=True)
                    head = (c0 - k_off) // HEAD_DIM + 2 * bl + half
                    top = jnp.max(sq, axis=0, keepdims=True)
                    stats = jnp.where((srow == 0) & (slane == head), top, stats)
    stats_ref[...] = stats
    for c0 in range(0, sel_ref.shape[1], chunk):
        proj_ref[:, n_w + c0:n_w + c0 + chunk] = _dot(hml, sel_ref[:, c0:c0 + chunk]).astype(BF16)
    n_gate = wg_ref.shape[1]
    for c0 in range(0, n_gate, chunk):
        gates_ref[:, c0:c0 + chunk] = _dot(hb, wg_ref[:, c0:c0 + chunk])


def _in_call(x, mod3, norm_g, w_all, w_gates, w_f, b_f, sel, n_heads, tm):
    b, s, d = x.shape
    n_w = w_all.shape[1]
    n_ext = sel.shape[1]
    n_proj = n_w + n_ext
    n_gate = w_gates.shape[1]
    chunk = 512
    assert n_heads * HEAD_DIM % chunk == 0 and n_w % chunk == 0 and n_ext % chunk == 0
    const = lambda shape: pl.BlockSpec(shape, lambda bi, i: (0,) * len(shape),
                                       pipeline_mode=pl.Buffered(1))
    return pl.pallas_call(
        functools.partial(_in_kernel, n_heads, chunk),
        out_shape=(jax.ShapeDtypeStruct((b, s, n_proj), BF16),
                   jax.ShapeDtypeStruct((b, s, n_gate), F32),
                   jax.ShapeDtypeStruct((b, s // tm, 8, LANES), F32)),
        grid=(b, s // tm),
        in_specs=[
            pl.BlockSpec((None, tm, d), lambda bi, i: (bi, i, 0)),
            pl.BlockSpec((None, N_MOD, d), lambda bi, i: (bi, 0, 0)),
            const((1, d)),
            const((d, n_w)),
            const((d, n_gate)),
            const((d, LANES)),
            const((1, LANES)),
            const((3 * LANES, n_ext)),
        ],
        out_specs=(pl.BlockSpec((None, tm, n_proj), lambda bi, i: (bi, i, 0)),
                   pl.BlockSpec((None, tm, n_gate), lambda bi, i: (bi, i, 0)),
                   pl.BlockSpec((None, None, 8, LANES), lambda bi, i: (bi, i, 0, 0))),
        scratch_shapes=[pltpu.VMEM((8, LANES), F32)],
        compiler_params=pltpu.CompilerParams(
            dimension_semantics=("arbitrary", "arbitrary"), vmem_limit_bytes=VMEM_LIMIT),
        name="in_proj",
    )(x, mod3, norm_g, w_all, w_gates, w_f, b_f, sel)


def _head_lanes(lane, hh):
    return (lane >= hh * HEAD_DIM) & (lane < (hh + 1) * HEAD_DIM)


def _bias_lanes(lane, hh, lo=0, hi=2 * N_SPLIT):
    return (lane >= hh * 2 * N_SPLIT + lo) & (lane < hh * 2 * N_SPLIT + hi)


def _fox_kernel(tk, n_heads, kn_ref, fe_ref, q_ref, qe_ref, k_ref, ke_ref, v_ref, o_ref):
    bi = pl.program_id(0)
    pair = pl.program_id(1)
    i = pl.program_id(2)
    tq = q_ref.shape[0]
    n_diag = tq // tk
    n_kt = k_ref.shape[0] // tk
    row = lax.broadcasted_iota(jnp.int32, (tq, tk), 0)
    col = lax.broadcasted_iota(jnp.int32, (tq, tk), 1)
    lane = lax.broadcasted_iota(jnp.int32, (tq, LANES), 1)
    q_heads = [jnp.concatenate([jnp.where(_head_lanes(lane, hh), q_ref[...], 0),
                                jnp.where(_bias_lanes(lane, hh), qe_ref[...], 0)], axis=1)
               for hh in range(2)]

    qf = q_ref[...].astype(F32)
    qef = qe_ref[...].astype(F32)
    q_norm, q_f = [], []
    for hh in range(2):
        q_norm.append(jnp.sqrt(jnp.sum(jnp.where(_head_lanes(lane, hh), qf * qf, 0.0),
                                       axis=1, keepdims=True)))
        q_f.append(jnp.sum(jnp.where(_bias_lanes(lane, hh, 0, N_SPLIT), qef, 0.0),
                           axis=1, keepdims=True))

    def may_contribute(j, states):
        worst = []
        for hh, (m, _, _) in enumerate(states):
            idx = (bi * n_kt + j) * n_heads + 2 * pair + hh
            bound = q_norm[hh] * (kn_ref[idx] * NORM_SLACK) + (q_f[hh] - fe_ref[idx]) - m
            worst.append(jnp.max(bound))
        return jnp.maximum(worst[0], worst[1]) >= FOX_ZERO_BELOW

    def head_step(hh, j, state, masked):
        m, l, acc = state
        k0 = pl.multiple_of(j * tk, tk)
        k_all = jnp.concatenate([k_ref[pl.ds(k0, tk), :], ke_ref[pl.ds(k0, tk), :]], axis=1)
        s = _dot_nt(q_heads[hh], k_all)
        if masked:
            s = jnp.where(col + (j - i * n_diag) * tk <= row, s, NEG_BIG)
        m_new = jnp.maximum(m, jnp.max(s, axis=1, keepdims=True))
        alpha = jnp.exp(m - m_new)
        p = jnp.exp(s - m_new)
        l = alpha * l + jnp.sum(p, axis=1, keepdims=True)
        acc = alpha * acc + _dot(p.astype(BF16), v_ref[pl.ds(k0, tk), :])
        return m_new, l, acc

    def pair_step(j, states, masked):
        return tuple(head_step(hh, j, st, masked) for hh, st in enumerate(states))

    init = (jnp.full((tq, 1), NEG_BIG, F32), jnp.zeros((tq, 1), F32),
            jnp.zeros((tq, LANES), F32))
    states = (init, init)
    for dj in range(n_diag):
        states = pair_step(i * n_diag + dj, states, True)

    n_full = i * n_diag

    def cond(carry):
        t, go, _ = carry
        return (t < n_full) & go

    def body(carry):
        t, _, sts = carry
        sts = pair_step(n_full - 1 - t, sts, False)
        return t + 1, may_contribute(jnp.maximum(n_full - 2 - t, 0), sts), sts

    go0 = may_contribute(jnp.maximum(n_full - 1, 0), states)
    _, _, states = lax.while_loop(cond, body, (jnp.int32(0), go0, states))
    outs = [acc / l for _, l, acc in states]
    o_ref[...] = jnp.where(lane < HEAD_DIM, outs[0], outs[1]).astype(o_ref.dtype)


def _fox_call(proj, stats, n_heads, q_off, qe_off, k_off, ke_off, v_off, tq, tk):
    b, s, _ = proj.shape
    assert stats.shape[1] == s // tk
    n_pairs = n_heads // 2
    k_norm = lax.cummax(jnp.sqrt(stats[:, :, 0, :n_heads]), axis=1).reshape(-1)
    f_end = stats[:, :, 1, :n_heads].reshape(-1)
    q_tile = lambda off: pl.BlockSpec(
        (None, tq, LANES), lambda bi, p, i, kn, fe: (bi, i, off // LANES + p))
    kv_all = lambda off: pl.BlockSpec(
        (None, s, LANES), lambda bi, p, i, kn, fe: (bi, 0, off // LANES + p))
    return pl.pallas_call(
        functools.partial(_fox_kernel, tk, n_heads),
        out_shape=jax.ShapeDtypeStruct((b, s, n_heads * HEAD_DIM), BF16),
        grid_spec=pltpu.PrefetchScalarGridSpec(
            num_scalar_prefetch=2, grid=(b, n_pairs, s // tq),
            in_specs=[q_tile(q_off), q_tile(qe_off), kv_all(k_off), kv_all(ke_off),
                      kv_all(v_off)],
            out_specs=pl.BlockSpec((None, tq, LANES), lambda bi, p, i, kn, fe: (bi, i, p))),
        compiler_params=pltpu.CompilerParams(
            dimension_semantics=("arbitrary", "arbitrary", "arbitrary"),
            vmem_limit_bytes=VMEM_LIMIT),
        name="fox_attn",
    )(k_norm, f_end, proj, proj, proj, proj, proj)


def _sb_kernel(tk, tr, q_ref, k_ref, v_ref, o_ref):
    i = pl.program_id(2)
    tq = q_ref.shape[0]
    n_chunks = tq // tr
    n_diag = tr // tk
    row = lax.broadcasted_iota(jnp.int32, (tr, tk), 0)
    col = lax.broadcasted_iota(jnp.int32, (tr, tk), 1)
    lane = lax.broadcasted_iota(jnp.int32, (tq, LANES), 1)
    klane = lax.broadcasted_iota(jnp.int32, (tk, LANES), 1)
    ur = lax.broadcasted_iota(jnp.int32, (2 * tk, tk), 0)
    uc = lax.broadcasted_iota(jnp.int32, (2 * tk, tk), 1)
    upper2 = jnp.where((ur & (tk - 1)) > uc, 1.0, 0.0).astype(BF16)
    chains = [(hh, r) for r in range(n_chunks) for hh in range(2)]

    def chain_step(hh, r, j, state, dj=None):
        masked = dj is not None
        c, acc = state
        q = q_ref[r * tr:(r + 1) * tr, :]
        k0 = pl.multiple_of(j * tk, tk)
        k = jnp.where(_head_lanes(klane, hh), k_ref[pl.ds(k0, tk), :], 0)
        z = _dot_nt(q, k)
        t = jnp.exp2(jnp.abs(z) * (-LOG2E))
        ls = jnp.minimum(z, 0.0) - jnp.log(1.0 + t)
        lom = ls - z
        if masked:
            strict = col + dj * tk < row
            lom = jnp.where(strict, lom, 0.0)
        hi = pltpu.bitcast(pltpu.bitcast(lom, jnp.uint32) & jnp.uint32(0xFFFF0000), F32)
        parts = jnp.concatenate([hi.astype(BF16), (lom - hi).astype(BF16)], axis=1)
        within = _dot(parts, upper2)
        a = jnp.exp2((ls + within + c) * LOG2E)
        if masked:
            a = jnp.where(strict, a, 0.0)
        acc = acc + _dot(a.astype(BF16), v_ref[pl.ds(k0, tk), :])
        c = c + within[:, 0:1] + lom[:, 0:1]
        return c, acc

    base = i * (tq // tk)
    states = []
    for hh, r in chains:
        st = (jnp.zeros((tr, 1), F32), jnp.zeros((tr, LANES), F32))
        for dj in reversed(range(n_diag)):
            st = chain_step(hh, r, base + r * n_diag + dj, st, dj)
        for j in reversed(range(r * n_diag)):
            st = chain_step(hh, r, base + j, st)
        states.append(st)

    def c_max(sts):
        return functools.reduce(jnp.maximum, [jnp.max(st[0]) for st in sts])

    def cond(carry):
        t, cm, _ = carry
        return (t < base) & (cm >= SB_ZERO_BELOW)

    def body(carry):
        t, _, sts = carry
        j = base - 1 - t
        sts = tuple(chain_step(hh, r, j, st) for (hh, r), st in zip(chains, sts))
        return t + 1, c_max(sts), sts

    states = tuple(states)
    _, _, states = lax.while_loop(cond, body, (jnp.int32(0), c_max(states), states))
    acc = {ch: st[1] for ch, st in zip(chains, states)}
    o0 = jnp.concatenate([acc[(0, r)] for r in range(n_chunks)], axis=0)
    o1 = jnp.concatenate([acc[(1, r)] for r in range(n_chunks)], axis=0)
    o_ref[...] = jnp.where(lane < HEAD_DIM, o0, o1).astype(o_ref.dtype)


def _sb_call(proj, n_heads, q_off, k_off, v_off, tq, tk, tr):
    b, s, _ = proj.shape
    n_pairs = n_heads // 2
    return pl.pallas_call(
        functools.partial(_sb_kernel, tk, tr),
        out_shape=jax.ShapeDtypeStruct((b, s, n_heads * HEAD_DIM), BF16),
        grid=(b, n_pairs, s // tq),
        in_specs=[
            pl.BlockSpec((None, tq, LANES), lambda bi, p, i: (bi, i, q_off // LANES + p)),
            pl.BlockSpec((None, s, LANES), lambda bi, p, i: (bi, 0, k_off // LANES + p)),
            pl.BlockSpec((None, s, LANES), lambda bi, p, i: (bi, 0, v_off // LANES + p)),
        ],
        out_specs=pl.BlockSpec((None, tq, LANES), lambda bi, p, i: (bi, i, p)),
        compiler_params=pltpu.CompilerParams(
            dimension_semantics=("arbitrary", "arbitrary", "arbitrary"),
            vmem_limit_bytes=VMEM_LIMIT),
        name="sb_attn",
    )(proj, proj, proj)


def _route(logits):
    tm = logits.shape[0]
    lane = lax.broadcasted_iota(jnp.int32, (tm, LANES), 1)
    big = jnp.int32(LANES)
    g_mask = lane < N_GROUPS
    lg = jnp.where(g_mask, logits, -jnp.inf)
    g_max = jnp.max(lg, axis=1, keepdims=True)
    g_exp = jnp.exp(lg - g_max)
    g_prob = g_exp / jnp.sum(g_exp, axis=1, keepdims=True)
    g_top = jnp.max(g_prob, axis=1, keepdims=True)
    g_idx = jnp.min(jnp.where(g_mask & (g_prob == g_top), lane, big), axis=1, keepdims=True)
    lo = N_GROUPS + EXPERTS_PER_GROUP * g_idx
    e_mask = (lane >= lo) & (lane < lo + EXPERTS_PER_GROUP)
    le = jnp.where(e_mask, logits, -jnp.inf)
    e_max = jnp.max(le, axis=1, keepdims=True)
    e_exp = jnp.exp(le - e_max)
    e_prob = e_exp / jnp.sum(e_exp, axis=1, keepdims=True)
    p1 = jnp.max(e_prob, axis=1, keepdims=True)
    i1 = jnp.min(jnp.where(e_mask & (e_prob == p1), lane, big), axis=1, keepdims=True)
    rest = e_mask & (lane != i1)
    p2 = jnp.max(jnp.where(rest, e_prob, -1.0), axis=1, keepdims=True)
    i2 = jnp.min(jnp.where(rest & (e_prob == p2), lane, big), axis=1, keepdims=True)
    tot = p1 + p2
    w1 = p1 / tot * g_top
    w2 = p2 / tot * g_top
    first_lower = i1 < i2
    e_lo = jnp.minimum(i1, i2) - lo
    e_hi = jnp.maximum(i1, i2) - lo
    pair = jnp.where(e_lo == 0, 0, jnp.where(e_lo == 1, 3, 5)) + (e_hi - e_lo - 1)
    cls = g_idx * N_PAIRS + pair
    return jnp.where(first_lower, w1, w2), jnp.where(first_lower, w2, w1), cls


def _post_kernel(of_ref, os_ref, gates_ref, x_ref, mod_ref, wbf_ref, wbs_ref, wo_ref, g2_ref,
                 wr_hi_ref, wr_lo_ref, rb_ref, x1_ref, slab_ref, meta_ref, cnt_ref, carry_ref):
    first = (pl.program_id(0) == 0) & (pl.program_id(1) == 0)

    @pl.when(first)
    def _():
        carry_ref[...] = jnp.zeros_like(carry_ref)

    tm = x_ref.shape[0]
    d = x_ref.shape[1]
    bf = _dot(of_ref[...], wbf_ref[...])
    bs = _dot(os_ref[...], wbs_ref[...])
    merged = jax.nn.sigmoid(gates_ref[:, :d]) * bf + jax.nn.sigmoid(gates_ref[:, d:]) * bs
    y = _dot(merged.astype(BF16), wo_ref[...])
    x1 = x_ref[...] + mod_ref[2:3, :] * y
    x1_ref[...] = x1
    h2 = _rms_scale(x1) * g2_ref[...] * (1.0 + mod_ref[4:5, :]) + mod_ref[3:4, :]
    h_hi, h_lo = _split2(h2)
    w_hi = wr_hi_ref[...]
    logits = _dot(h_hi, w_hi) + _dot(h_hi, wr_lo_ref[...]) + _dot(h_lo, w_hi) + rb_ref[...]
    w_first, w_second, cls = _route(logits)

    lane = lax.broadcasted_iota(jnp.int32, (tm, LANES), 1)
    onehot = lane == cls
    oh = jnp.where(onehot, 1.0, 0.0)
    row = lax.broadcasted_iota(jnp.int32, (tm, tm), 0)
    col = lax.broadcasted_iota(jnp.int32, (tm, tm), 1)
    earlier = jnp.where(col < row, 1.0, 0.0).astype(BF16)
    before = _dot(earlier, oh.astype(BF16)) + carry_ref[0:1, :]
    rank = jnp.sum(jnp.where(onehot, before, 0.0), axis=1, keepdims=True)
    counts = carry_ref[0:1, :] + jnp.sum(oh, axis=0, keepdims=True)
    carry_ref[...] = jnp.broadcast_to(counts, carry_ref.shape)
    cnt_ref[...] = jnp.broadcast_to(counts, cnt_ref.shape)

    meta = (jnp.where(lane == 0, w_first, 0.0) + jnp.where(lane == 1, w_second, 0.0)
            + jnp.where(lane == 2, cls.astype(F32), 0.0) + jnp.where(lane == 3, rank, 0.0))
    meta_ref[...] = meta
    h_rows = d // LANES
    pitch = slab_ref.shape[0] // tm
    for s in range(pitch):
        if s < h_rows:
            val = h2[:, s * LANES:(s + 1) * LANES]
        else:
            val = meta if s == h_rows else jnp.zeros((tm, LANES), F32)
        slab_ref[pl.ds(s, tm, stride=pitch), :] = val


def _post_call(o_fox, o_sb, gates, x, mod3, w_bf, w_bs, w_o, norm_g, wr_hi, wr_lo, rb, tm):
    b, s, d = x.shape
    wdt = o_fox.shape[2]
    pitch = _slab_pitch(d // LANES + 1)
    const = lambda shape: pl.BlockSpec(shape, lambda bi, i: (0,) * len(shape),
                                       pipeline_mode=pl.Buffered(1))
    tile = lambda w: pl.BlockSpec((None, tm, w), lambda bi, i: (bi, i, 0))
    return pl.pallas_call(
        _post_kernel,
        out_shape=(jax.ShapeDtypeStruct((b, s, d), F32),
                   jax.ShapeDtypeStruct((b, s * pitch, LANES), F32),
                   jax.ShapeDtypeStruct((b, s, LANES), F32),
                   jax.ShapeDtypeStruct((8, LANES), F32)),
        grid=(b, s // tm),
        in_specs=[
            tile(wdt), tile(wdt), tile(2 * d), tile(d),
            pl.BlockSpec((None, N_MOD, d), lambda bi, i: (bi, 0, 0)),
            const((wdt, d)), const((wdt, d)), const((d, d)), const((1, d)),
            const((d, LANES)), const((d, LANES)), const((1, LANES)),
        ],
        out_specs=(tile(d),
                   pl.BlockSpec((None, tm * pitch, LANES), lambda bi, i: (bi, i, 0)),
                   tile(LANES),
                   pl.BlockSpec((8, LANES), lambda bi, i: (0, 0))),
        scratch_shapes=[pltpu.VMEM((8, LANES), F32)],
        compiler_params=pltpu.CompilerParams(
            dimension_semantics=("arbitrary", "arbitrary"), vmem_limit_bytes=VMEM_LIMIT),
        name="post",
    )(o_fox, o_sb, gates, x, mod3, w_bf, w_bs, w_o, norm_g, wr_hi, wr_lo, rb)


def _slab_pitch(rows):
    pitch = -(-rows // 4) * 4
    return pitch if (pitch // 4) % 2 else pitch + 4


def _dispatch_kernel(pitch, tile, n_cls, pos_ref, pad_lo_ref, pad_n_ref, nv_ref, src_ref,
                     dst_ref, zeros, sem, zsem):
    rows = src_ref.shape[0] // pitch
    t = pl.program_id(0)
    base = t * rows
    n_tiles = dst_ref.shape[0] // (tile * pitch)
    zero_slab = zeros.at[pl.ds(0, pitch), :]

    def pad_copy(p):
        return pltpu.make_async_copy(zero_slab, dst_ref.at[pl.ds(p * pitch, pitch), :], zsem)

    def tile_copy(j):
        return pltpu.make_async_copy(zeros, dst_ref.at[pl.ds(j * tile * pitch, tile * pitch), :],
                                     zsem)

    def for_each_zero_copy(act):
        for c in range(n_cls):
            def pad_body(k, carry, c=c):
                act(pad_copy(pad_lo_ref[c] + k))
                return carry

            lax.fori_loop(0, pad_n_ref[c], pad_body, 0)

        def tile_body(j, carry):
            act(tile_copy(j))
            return carry

        lax.fori_loop(nv_ref[0], n_tiles, tile_body, 0)

    @pl.when(t == 0)
    def _():
        zeros[...] = jnp.zeros_like(zeros)
        for_each_zero_copy(lambda cp: cp.start())

    def issue(g, carry):
        for u in range(DMA_GROUP):
            r = g * DMA_GROUP + u
            pltpu.make_async_copy(src_ref.at[pl.ds(r * pitch, pitch), :],
                                  dst_ref.at[pl.ds(pos_ref[base + r] * pitch, pitch), :],
                                  sem).start(priority=u % 2)
        return carry

    lax.fori_loop(0, rows // DMA_GROUP, issue, 0)
    pltpu.make_async_copy(src_ref, dst_ref.at[pl.ds(0, rows * pitch), :], sem).wait()

    @pl.when(t == 0)
    def _():
        for_each_zero_copy(lambda cp: cp.wait())


def _dispatch_call(slabs, pos, pad_lo, pad_n, n_valid, n_pad, rows, pitch, tile):
    n = slabs.shape[0] // pitch
    any_spec = pl.BlockSpec(memory_space=pl.ANY)
    return pl.pallas_call(
        functools.partial(_dispatch_kernel, pitch, tile, pad_lo.shape[0]),
        out_shape=jax.ShapeDtypeStruct((n_pad * pitch, LANES), slabs.dtype),
        grid_spec=pltpu.PrefetchScalarGridSpec(
            num_scalar_prefetch=4, grid=(n // rows,),
            in_specs=[pl.BlockSpec((rows * pitch, LANES), lambda t, *_: (t, 0))],
            out_specs=any_spec,
            scratch_shapes=[pltpu.VMEM((tile * pitch, LANES), slabs.dtype),
                            pltpu.SemaphoreType.DMA(()), pltpu.SemaphoreType.DMA(())]),
        compiler_params=pltpu.CompilerParams(
            dimension_semantics=("arbitrary",), has_side_effects=True,
            vmem_limit_bytes=VMEM_LIMIT),
        name="dispatch",
    )(pos, pad_lo, pad_n, n_valid, slabs)


def _moe_kernel(tm, elo_ref, ehi_ref, tix_ref, nv_ref, slab_ref, wg0_ref, wu0_ref, wd0_ref,
                wg1_ref, wu1_ref, wd1_ref, y_ref):
    y_rows = wd0_ref.shape[1] // LANES
    pitch = slab_ref.shape[0] // tm
    y_pitch = y_ref.shape[0] // tm
    valid = pl.program_id(0) < nv_ref[0]

    @pl.when(jnp.logical_not(valid))
    def _():
        y_ref[...] = jnp.zeros_like(y_ref)

    @pl.when(valid)
    def _():
        h = jnp.concatenate([slab_ref[pl.ds(s, tm, stride=pitch), :] for s in range(y_rows)],
                            axis=1).astype(BF16)
        meta = slab_ref[pl.ds(y_rows, tm, stride=pitch), :]

        def expert(wg_ref, wu_ref, wd_ref):
            g = _dot(h, wg_ref[...])
            u = _dot(h, wu_ref[...])
            act = (g * jax.nn.sigmoid(g) * u).astype(BF16)
            return _dot(act, wd_ref[...])

        y = (meta[:, 0:1] * expert(wg0_ref, wu0_ref, wd0_ref)
             + meta[:, 1:2] * expert(wg1_ref, wu1_ref, wd1_ref))
        for s in range(y_pitch):
            val = y[:, s * LANES:(s + 1) * LANES] if s < y_rows else jnp.zeros((tm, LANES), F32)
            y_ref[pl.ds(s, tm, stride=y_pitch), :] = val


def _moe_call(slabs, tile_elo, tile_ehi, tile_ix, n_valid, w_gate, w_up, w_down, tm, pitch,
              y_pitch):
    n_pad = slabs.shape[0] // pitch
    _, d, f = w_gate.shape
    lo = lambda t, elo, ehi, tix, nv: (elo[t], 0, 0)
    hi = lambda t, elo, ehi, tix, nv: (ehi[t], 0, 0)
    return pl.pallas_call(
        functools.partial(_moe_kernel, tm),
        out_shape=jax.ShapeDtypeStruct((n_pad * y_pitch, LANES), F32),
        grid_spec=pltpu.PrefetchScalarGridSpec(
            num_scalar_prefetch=4, grid=(n_pad // tm,),
            in_specs=[
                pl.BlockSpec((tm * pitch, LANES), lambda t, elo, ehi, tix, nv: (tix[t], 0)),
                pl.BlockSpec((None, d, f), lo), pl.BlockSpec((None, d, f), lo),
                pl.BlockSpec((None, f, d), lo),
                pl.BlockSpec((None, d, f), hi), pl.BlockSpec((None, d, f), hi),
                pl.BlockSpec((None, f, d), hi),
            ],
            out_specs=pl.BlockSpec((tm * y_pitch, LANES),
                                   lambda t, elo, ehi, tix, nv: (t, 0))),
        compiler_params=pltpu.CompilerParams(
            dimension_semantics=("arbitrary",), vmem_limit_bytes=VMEM_LIMIT),
        name="moe",
    )(tile_elo, tile_ehi, tile_ix, n_valid, slabs, w_gate, w_up, w_down, w_gate, w_up, w_down)


def _combine_kernel(pos_ref, ys_ref, x1_ref, mod_ref, fg_ref, o_ref, ybuf, sem):
    rows, d = x1_ref.shape
    y_rows = d // LANES
    y_pitch = ybuf.shape[1] // rows
    t = pl.program_id(0)

    def issue(step, slot):
        base = step * rows

        def body(g, carry):
            for u in range(DMA_GROUP):
                r = g * DMA_GROUP + u
                pltpu.make_async_copy(ys_ref.at[pl.ds(pos_ref[base + r] * y_pitch, y_rows), :],
                                      ybuf.at[slot, pl.ds(r * y_pitch, y_rows), :],
                                      sem.at[slot]).start(priority=u % 2)
            return carry

        lax.fori_loop(0, rows // DMA_GROUP, body, 0)

    @pl.when(t == 0)
    def _():
        issue(0, 0)

    @pl.when(t + 1 < pl.num_programs(0))
    def _():
        issue(t + 1, (t + 1) % 2)

    slot = t % 2
    pltpu.make_async_copy(ys_ref.at[pl.ds(0, rows * y_rows), :],
                          ybuf.at[slot, pl.ds(0, rows * y_rows), :], sem.at[slot]).wait()
    ycur = ybuf.at[slot]
    y = jnp.concatenate([ycur[pl.ds(s, rows, stride=y_pitch), :] for s in range(y_rows)], axis=1)
    x2 = x1_ref[...] + mod_ref[5:6, :] * y
    o_ref[...] = _rms_scale(x2) * fg_ref[...]


def _combine_call(y_sorted, pos, x1, mod3, final_g, seq, rows, y_pitch):
    n, d = x1.shape
    per_seq = seq // rows
    return pl.pallas_call(
        _combine_kernel,
        out_shape=jax.ShapeDtypeStruct((n, d), F32),
        grid_spec=pltpu.PrefetchScalarGridSpec(
            num_scalar_prefetch=1, grid=(n // rows,),
            in_specs=[
                pl.BlockSpec(memory_space=pl.ANY),
                pl.BlockSpec((rows, d), lambda t, pos: (t, 0)),
                pl.BlockSpec((None, N_MOD, d), lambda t, pos: (t // per_seq, 0, 0)),
                pl.BlockSpec((1, d), lambda t, pos: (0, 0)),
            ],
            out_specs=pl.BlockSpec((rows, d), lambda t, pos: (t, 0)),
            scratch_shapes=[pltpu.VMEM((2, rows * y_pitch, LANES), F32),
                            pltpu.SemaphoreType.DMA((2,))]),
        compiler_params=pltpu.CompilerParams(
            dimension_semantics=("arbitrary",), vmem_limit_bytes=VMEM_LIMIT),
        name="combine",
    )(pos, y_sorted, x1, mod3, final_g)


def _pad_lanes(a):
    return jnp.pad(a.astype(F32), ((0, 0), (0, LANES - a.shape[1])))


def _bias_selectors(n_heads):
    rows = N_SPLIT * LANES
    n_pairs = n_heads // 2
    sel_q = np.zeros((rows, n_pairs * LANES), np.float32)
    sel_k = np.zeros((rows, n_pairs * LANES), np.float32)
    ones_row = n_heads
    for h in range(n_heads):
        base = (h // 2) * LANES + (h % 2) * 2 * N_SPLIT
        for j in range(N_SPLIT):
            sel_q[j * LANES + h, base + j] = 1.0
            sel_q[ones_row, base + N_SPLIT + j] = 1.0
            sel_k[ones_row, base + j] = 1.0
            sel_k[j * LANES + h, base + N_SPLIT + j] = -1.0
    return sel_q, sel_k


def kernel(x, c, ada_w, ada_b, norm1_g, w_in, b_forget, w_branch_fox, w_branch_sb, w_out,
           norm2_g, router_group_w, router_group_b, router_expert_w, router_expert_b,
           expert_w_gate, expert_w_up, expert_w_down, final_g):
    b, s, d = x.shape
    depth = ada_w.shape[0]
    hf = b_forget.shape[1]
    fw = hf * HEAD_DIM
    sw = w_branch_sb.shape[1]
    hs = sw // HEAD_DIM
    scale = HEAD_DIM ** -0.5
    n_exp = expert_w_gate.shape[1]

    assert depth == 1, "the final RMSNorm is fused into the single layer's MoE kernel"
    for l in range(depth):
        mod = _mod_call(c, ada_w[l], ada_b[l])
        mod3 = mod.reshape(b, N_MOD, d)

        wl = w_in[l]
        o = 0
        wq_a = wl[:, o:o + fw]; o += fw
        wk_a = wl[:, o:o + fw]; o += fw
        wv_a = wl[:, o:o + fw]; o += fw
        wf = wl[:, o:o + hf]; o += hf
        wq_b = wl[:, o:o + sw]; o += sw
        wk_b = wl[:, o:o + sw]; o += sw
        wv_b = wl[:, o:o + sw]; o += sw
        w_gates = wl[:, o:].astype(BF16)
        w_all = jnp.concatenate([wq_a * scale, wk_a, wq_b * scale, wk_b, wv_a, wv_b],
                                axis=1).astype(BF16)
        sel_q, sel_k = _bias_selectors(hf)
        sel = jnp.concatenate([sel_q, sel_k], axis=1).astype(BF16)
        w_f = _pad_lanes(wf).astype(BF16)
        b_f = _pad_lanes(b_forget[l].reshape(1, hf))

        q_a_off = 0
        k_a_off = fw
        q_b_off = 2 * fw
        k_b_off = q_b_off + sw
        v_a_off = k_b_off + sw
        v_b_off = v_a_off + fw
        qe_off = v_b_off + sw
        ke_off = qe_off + hf // 2 * LANES

        tq = min(s, ROW_TILE)
        proj, gates, kstats = _in_call(x, mod3, norm1_g[l].reshape(1, d), w_all, w_gates, w_f, b_f,
                                       sel, hf, tm=tq)
        o_fox = _fox_call(proj, kstats, hf, q_a_off, qe_off, k_a_off, ke_off, v_a_off, tq, tq)
        o_sb = _sb_call(proj, hs, q_b_off, k_b_off, v_b_off, tq, min(s, SB_KEY_TILE), tq)

        wr = _pad_lanes(jnp.concatenate([router_group_w[l], router_expert_w[l]], axis=1))
        wr_hi = wr.astype(BF16)
        wr_lo = (wr - wr_hi.astype(F32)).astype(BF16)
        rb = _pad_lanes(jnp.concatenate([router_group_b[l], router_expert_b[l]]).reshape(1, -1))

        x1, slabs, meta, counts = _post_call(
            o_fox, o_sb, gates, x, mod3, w_branch_fox[l].astype(BF16), w_branch_sb[l].astype(BF16),
            w_out[l].astype(BF16), norm2_g[l].reshape(1, d), wr_hi, wr_lo, rb, tm=tq)

        n = b * s
        tm_e = MOE_TILE
        n_cls = N_GROUPS * N_PAIRS
        n_pad = n + n_cls * tm_e
        cls = meta[..., 2].reshape(n).astype(jnp.int32)
        rank = meta[..., 3].reshape(n).astype(jnp.int32)
        cnt = counts[0, :n_cls].astype(jnp.int32)
        padded = (cnt + tm_e - 1) // tm_e * tm_e
        ends = jnp.cumsum(padded)
        starts = ends - padded
        cls_ids = jnp.arange(n_cls, dtype=jnp.int32)
        pos = jnp.sum(jnp.where(cls[:, None] == cls_ids[None, :], starts[None, :], 0), axis=1) + rank
        n_valid = ends[-1] // tm_e
        tile_ix = jnp.minimum(jnp.arange(n_pad // tm_e, dtype=jnp.int32), n_valid - 1)
        tile_cls = jnp.sum((tile_ix * tm_e)[:, None] >= ends[None, :], axis=1)
        pair_lo = jnp.asarray([0, 0, 0, 1, 1, 2], jnp.int32)
        pair_hi = jnp.asarray([1, 2, 3, 2, 3, 3], jnp.int32)
        tile_grp = tile_cls // N_PAIRS * EXPERTS_PER_GROUP
        tile_elo = tile_grp + pair_lo[tile_cls % N_PAIRS]
        tile_ehi = tile_grp + pair_hi[tile_cls % N_PAIRS]

        rows = min(s, COMBINE_ROWS)
        pitch = slabs.shape[1] // s
        y_pitch = _slab_pitch(d // LANES)
        sorted_slabs = _dispatch_call(slabs.reshape(n * pitch, LANES), pos, starts + cnt,
                                      padded - cnt, n_valid.reshape(1), n_pad,
                                      min(n, DISPATCH_ROWS), pitch, tm_e)
        y_sorted = _moe_call(sorted_slabs, tile_elo, tile_ehi, tile_ix, n_valid.reshape(1),
                             expert_w_gate[l].astype(BF16), expert_w_up[l].astype(BF16),
                             expert_w_down[l].astype(BF16), tm_e, pitch, y_pitch)
        out = _combine_call(y_sorted, pos, x1.reshape(n, d), mod3, final_g.reshape(1, d), s, rows,
                            y_pitch)
        x = out.reshape(b, s, d)
    return x
```
